```python
import functools
import jax, jax.numpy as jnp
from jax import lax
import numpy as np

D_MODEL = 1024
BATCH = 8
SEQ = 2048
DEPTH = 1
DEC_BATCH = 128
DEC_SEQ = 1
PAST_LEN = 8192
PAGE_SIZE = 128

DIL_GROUPS = ((128, 1), (512, 4), (2048, 16))
N_GROUPS = 3
HEADS_PER_GROUP = 4
HEAD_DIM_A = 128
ATT_QKV = N_GROUPS * HEADS_PER_GROUP * HEAD_DIM_A
ATT_OUT = HEADS_PER_GROUP * HEAD_DIM_A
BAND_BLOCK = 128

RET_HEADS = 4
RET_DK = 128
RET_DV = 256
RET_QK = RET_HEADS * RET_DK
RET_V = RET_HEADS * RET_DV
RET_CHUNK = 128
ROPE_BASE = 10000.0

D_FF = 4 * D_MODEL
EPS = 1e-6
NEG_INF = -1e30

COL_SIZES = (ATT_QKV, ATT_QKV, ATT_QKV, RET_QK, RET_QK, RET_V, RET_V, D_MODEL, D_MODEL)
IN_COLS = 3 * ATT_QKV + 2 * RET_QK + 2 * RET_V + 2 * D_MODEL

kernel_name = "dilated_attn_retention_hybrid_step"


def _rmsnorm(x, g):
    xf = x.astype(jnp.float32)
    y = xf * lax.rsqrt(jnp.mean(xf * xf, axis=-1, keepdims=True) + EPS)
    return (y * g.astype(jnp.float32)).astype(x.dtype)


def _rope(x, pos):
    half = x.shape[-1] // 2
    inv = ROPE_BASE ** (-jnp.arange(half, dtype=jnp.float32) / half)
    ang = pos.astype(jnp.float32)[:, None] * inv[None, :]
    cos = jnp.cos(ang)[None, :, None, :]
    sin = jnp.sin(ang)[None, :, None, :]
    x1, x2 = x[..., :half], x[..., half:]
    return jnp.concatenate([x1 * cos - x2 * sin, x1 * sin + x2 * cos], axis=-1)


def _dilated_band(q, k, v, window, dil):
    B, S, H, Dh = q.shape
    n = S // dil
    span = window // dil
    nb = -(-n // BAND_BLOCK)
    n_pad = nb * BAND_BLOCK

    def to_blocks(t):
        t = t.astype(jnp.float32).reshape(B, n, dil, H, Dh).transpose(0, 2, 1, 3, 4)
        t = jnp.pad(t, ((0, 0), (0, 0), (0, n_pad - n), (0, 0), (0, 0)))
        return t.reshape(B, dil, nb, BAND_BLOCK, H, Dh)

    qb, kb, vb = to_blocks(q), to_blocks(k), to_blocks(v)

    def with_prev(t):
        prev = jnp.concatenate([jnp.zeros_like(t[:, :, :1]), t[:, :, :-1]], axis=2)
        return jnp.concatenate([prev, t], axis=3)

    kk, vv = with_prev(kb), with_prev(vb)
    blk = jnp.arange(nb)[:, None, None] * BAND_BLOCK
    qi = blk + jnp.arange(BAND_BLOCK)[None, :, None]
    ki = blk - BAND_BLOCK + jnp.arange(2 * BAND_BLOCK)[None, None, :]
    rel = qi - ki
    mask = (rel >= 0) & (rel <= span) & (ki >= 0)
    s = jnp.einsum('bcnqhd,bcnkhd->bcnhqk', qb, kk) * (Dh ** -0.5)
    s = jnp.where(mask[None, None, :, None], s, NEG_INF)
    lse = jax.nn.logsumexp(s, axis=-1)
    p = jnp.exp(s - lse[..., None])
    o = jnp.einsum('bcnhqk,bcnkhd->bcnqhd', p, vv)
    o = o.reshape(B, dil, n_pad, H, Dh)[:, :, :n].transpose(0, 2, 1, 3, 4).reshape(B, S, H, Dh)
    lse = lse.transpose(0, 1, 2, 4, 3).reshape(B, dil, n_pad, H)[:, :, :n]
    lse = lse.transpose(0, 2, 1, 3).reshape(B, S, H)
    return o, lse


def _dilated_gather(q, kv_new, kv_buf, window, dil):
    L = kv_buf.shape[1]
    T, Dh = q.shape[1], q.shape[-1]
    span = window // dil
    kv_all = jnp.concatenate([kv_buf.astype(jnp.float32), kv_new.astype(jnp.float32)], axis=1)
    pos_q = PAST_LEN + jnp.arange(T)
    key_pos = pos_q[:, None] - dil * jnp.arange(span + 1)[None, :]
    idx = key_pos - (PAST_LEN - L)
    valid = idx >= 0
    g = jnp.take(kv_all, jnp.maximum(idx, 0), axis=1)
    s = jnp.einsum('bthd,btjhd->bthj', q.astype(jnp.float32), g[:, :, :, 0]) * (Dh ** -0.5)
    s = jnp.where(valid[None, :, None, :], s, NEG_INF)
    lse = jax.nn.logsumexp(s, axis=-1)
    p = jnp.exp(s - lse[..., None])
    o = jnp.einsum('bthj,btjhd->bthd', p, g[:, :, :, 1])
    return o, lse


def _combine_groups(outs, lses):
    w = jax.nn.softmax(jnp.stack(lses, axis=0), axis=0)
    return jnp.sum(w[..., None] * jnp.stack(outs, axis=0), axis=0)


def _attend_prompt(qa, ka, va):
    T = qa.shape[1]
    outs, lses, rows = [], [], []
    for g, (win, dil) in enumerate(DIL_GROUPS):
        o, lse = _dilated_band(qa[:, :, g], ka[:, :, g], va[:, :, g], win, dil)
        outs.append(o)
        lses.append(lse)
        kv = jnp.stack([ka[:, :, g], va[:, :, g]], axis=2)
        rows.append(kv[:, T - min(win, T):])
    return _combine_groups(outs, lses), rows


def _attend_sample(qa, ka, va, caches):
    outs, lses, rows = [], [], []
    for g, (win, dil) in enumerate(DIL_GROUPS):
        kv_new = jnp.stack([ka[:, :, g], va[:, :, g]], axis=2)
        o, lse = _dilated_gather(qa[:, :, g], kv_new, caches[g], win, dil)
        outs.append(o)
        lses.append(lse)
        rows.append(kv_new)
    return _combine_groups(outs, lses), rows


def _ret_chunk(S, qkv, log_g):
    q, k, v = qkv
    C = q.shape[1]
    t = jnp.arange(C, dtype=jnp.float32)
    rel = t[:, None] - t[None, :]
    D = jnp.where(rel[None] >= 0, jnp.exp(log_g[:, None, None] * jnp.maximum(rel, 0.0)[None]), 0.0)
    s = jnp.einsum('bqhd,bkhd->bhqk', q, k) * D[None]
    o = jnp.einsum('bhqk,bkhv->bqhv', s, v)
    inner = jnp.exp(log_g[None, :] * (t[:, None] + 1.0))
    o = o + jnp.einsum('bqhd,bhdv->bqhv', q * inner[None, :, :, None], S)
    tail = jnp.exp(log_g[None, :] * (C - 1.0 - t[:, None]))
    S_new = jnp.exp(log_g * C)[None, :, None, None] * S + jnp.einsum('bkhd,bkhv->bhdv', k * tail[None, :, :, None], v)
    return S_new, o


def _retention(q, k, v, S0):
    B, T, H, dk = q.shape
    dv = v.shape[-1]
    C = RET_CHUNK if T % RET_CHUNK == 0 else T
    n = T // C
    log_g = jnp.log1p(-jnp.power(2.0, -5.0 - jnp.arange(H, dtype=jnp.float32)))

    def chunks(t):
        return t.reshape(B, n, C, H, t.shape[-1]).transpose(1, 0, 2, 3, 4)

    S_fin, o = lax.scan(functools.partial(_ret_chunk, log_g=log_g), S0.astype(jnp.float32),
                        (chunks(q), chunks(k), chunks(v)))
    o = o.transpose(1, 0, 2, 3, 4).reshape(B, T, H, dv)
    return o, S_fin


def _layer(x, pos, attend, S0, lw):
    (w_in, w_att_br, w_ret_br, w_out, w_up, w_down,
     g_ret_norm, g_pre_mix, g_post_mix, g_pre_mlp, g_post_mlp) = lw
    B, T, _ = x.shape
    h = _rmsnorm(x, g_pre_mix)
    z = h @ w_in
    splits, acc = [], 0
    for c in COL_SIZES[:-1]:
        acc += c
        splits.append(acc)
    qa, ka, va, qr, kr, vr, gr, gate_a, gate_r = jnp.split(z, splits, axis=-1)
    shp_a = (B, T, N_GROUPS, HEADS_PER_GROUP, HEAD_DIM_A)
    att, kv_rows = attend(qa.reshape(shp_a), ka.reshape(shp_a), va.reshape(shp_a))
    a = att.reshape(B, T, ATT_OUT).astype(x.dtype) @ w_att_br
    qr = _rope(qr.reshape(B, T, RET_HEADS, RET_DK).astype(jnp.float32), pos)
    kr = _rope(kr.reshape(B, T, RET_HEADS, RET_DK).astype(jnp.float32), pos) * (RET_DK ** -0.5)
    vr = vr.reshape(B, T, RET_HEADS, RET_DV).astype(jnp.float32)
    ro, S_new = _retention(qr, kr, vr, S0)
    mu = jnp.mean(ro, axis=-1, keepdims=True)
    var = jnp.mean(jnp.square(ro - mu), axis=-1, keepdims=True)
    rn = ((ro - mu) * lax.rsqrt(var + EPS)).reshape(B, T, RET_V) * g_ret_norm.astype(jnp.float32)
    r = (jax.nn.silu(gr) * rn.astype(x.dtype)) @ w_ret_br
    m = jax.nn.sigmoid(gate_a) * a + jax.nn.sigmoid(gate_r) * r
    x = x + _rmsnorm(m @ w_out, g_post_mix)
    u = jnp.square(jax.nn.relu(_rmsnorm(x, g_pre_mlp) @ w_up))
    x = x + _rmsnorm(u @ w_down, g_post_mlp)
    return x, kv_rows, S_new


def setup_inputs(seed: int = 0) -> dict:
    key = jax.random.key(seed)
    ks = jax.random.split(key, 20)
    f32 = jnp.float32
    nrm = lambda k, shp, sc: jax.random.normal(k, shp, f32) * sc
    L1, L2, L3 = (min(w, PAST_LEN) for w, _ in DIL_GROUPS)
    kv_shape = lambda L: (DEPTH, DEC_BATCH, L, 2, HEADS_PER_GROUP, HEAD_DIM_A)
    return {
        "x_prompt": nrm(ks[0], (BATCH, SEQ, D_MODEL), 1.0),
        "x_sample": nrm(ks[1], (DEC_BATCH, DEC_SEQ, D_MODEL), 1.0),
        "cache_kv_d1": nrm(ks[2], kv_shape(L1), 1.0),
        "cache_kv_d4": nrm(ks[3], kv_shape(L2), 1.0),
        "cache_kv_d16": nrm(ks[4], kv_shape(L3), 1.0),
        "state_ret": nrm(ks[5], (DEPTH, DEC_BATCH, RET_HEADS, RET_DK, RET_DV), 1.0),
        "w_in": nrm(ks[6], (DEPTH, D_MODEL, IN_COLS), D_MODEL ** -0.5),
        "w_att_br": nrm(ks[7], (DEPTH, ATT_OUT, D_MODEL), ATT_OUT ** -0.5),
        "w_ret_br": nrm(ks[8], (DEPTH, RET_V, D_MODEL), RET_V ** -0.5),
        "w_out": nrm(ks[9], (DEPTH, D_MODEL, D_MODEL), D_MODEL ** -0.5),
        "w_up": nrm(ks[10], (DEPTH, D_MODEL, D_FF), D_MODEL ** -0.5),
        "w_down": nrm(ks[11], (DEPTH, D_FF, D_MODEL), D_FF ** -0.5),
        "g_ret_norm": 1.0 + nrm(ks[12], (DEPTH, RET_V), 0.05),
        "g_pre_mix": 1.0 + nrm(ks[13], (DEPTH, D_MODEL), 0.05),
        "g_post_mix": 1.0 + nrm(ks[14], (DEPTH, D_MODEL), 0.05),
        "g_pre_mlp": 1.0 + nrm(ks[15], (DEPTH, D_MODEL), 0.05),
        "g_post_mlp": 1.0 + nrm(ks[16], (DEPTH, D_MODEL), 0.05),
    }


def reference(x_prompt, x_sample, cache_kv_d1, cache_kv_d4, cache_kv_d16, state_ret,
              w_in, w_att_br, w_ret_br, w_out, w_up, w_down,
              g_ret_norm, g_pre_mix, g_post_mix, g_pre_mlp, g_post_mlp):
    xp, xs = x_prompt, x_sample
    pos_p = jnp.arange(SEQ, dtype=jnp.int32)
    pos_s = PAST_LEN + jnp.arange(DEC_SEQ, dtype=jnp.int32)
    p_rows = [[], [], []]
    s_rows = [[], [], []]
    p_states, s_states = [], []
    for l in range(DEPTH):
        lw = (w_in[l], w_att_br[l], w_ret_br[l], w_out[l], w_up[l], w_down[l],
              g_ret_norm[l], g_pre_mix[l], g_post_mix[l], g_pre_mlp[l], g_post_mlp[l])
        S0 = jnp.zeros((BATCH, RET_HEADS, RET_DK, RET_DV), jnp.float32)
        xp, rows_p, Sp = _layer(xp, pos_p, _attend_prompt, S0, lw)
        attend_s = functools.partial(_attend_sample, caches=(cache_kv_d1[l], cache_kv_d4[l], cache_kv_d16[l]))
        xs, rows_s, Ss = _layer(xs, pos_s, attend_s, state_ret[l], lw)
        for g in range(N_GROUPS):
            p_rows[g].append(rows_p[g])
            s_rows[g].append(rows_s[g])
        p_states.append(Sp)
        s_states.append(Ss)
    kv_d1_p = jnp.stack(p_rows[0], axis=0)
    kv_d4_p = jnp.stack(p_rows[1], axis=0)
    kv_d16_p = jnp.stack(p_rows[2], axis=0)
    ret_p = jnp.stack(p_states, axis=0)
    kv_d1_s = jnp.stack(s_rows[0], axis=0)
    kv_d4_s = jnp.stack(s_rows[1], axis=0)
    kv_d16_s = jnp.stack(s_rows[2], axis=0)
    ret_s = jnp.stack(s_states, axis=0)
    return (xp, xs, kv_d1_p, kv_d4_p, kv_d16_p, ret_p, kv_d1_s, kv_d4_s, kv_d16_s, ret_s)
```

```python
import functools

import jax
import jax.numpy as jnp
from jax import lax
from jax.experimental import pallas as pl
from jax.experimental.pallas import tpu as pltpu

F32 = jnp.float32
BF16 = jnp.bfloat16

D_MODEL = 1024
PAST_LEN = 8192

DIL_GROUPS = ((128, 1), (512, 4), (2048, 16))
N_GROUPS = 3
HEADS = 4
HEAD_DIM = 128
ATT_COLS = N_GROUPS * HEADS * HEAD_DIM
KV_COLS = 2 * HEADS * HEAD_DIM
BAND = 128

RET_HEADS = 4
RET_DK = 128
RET_DV = 256
RET_QK = RET_HEADS * RET_DK
RET_V = RET_HEADS * RET_DV
RET_CHUNK = 128
ROPE_BASE = 10000.0

D_FF = 4 * D_MODEL
IN_COLS = 3 * ATT_COLS + 2 * RET_QK + 2 * RET_V + 2 * D_MODEL
EPS = 1e-6
NEG_INF = -1e30

TN = 512
N_COL_TILES = IN_COLS // TN

V7X_VMEM_BYTES = 64 * 1024 * 1024
MIB = 1024 * 1024


def _cparams(n_axes, vmem_mib):
    assert vmem_mib * MIB < V7X_VMEM_BYTES
    return pltpu.CompilerParams(
        dimension_semantics=("arbitrary",) * n_axes,
        vmem_limit_bytes=vmem_mib * MIB,
    )


def _rms(x):
    return x * lax.rsqrt(jnp.mean(x * x, axis=-1, keepdims=True) + EPS)


def _sigmoid(x):
    return 1.0 / (1.0 + jnp.exp(-x))


def _dot(a, b):
    return jnp.dot(a, b, preferred_element_type=F32)


def _dot_nt(a, b):
    return lax.dot_general(a, b, (((1,), (1,)), ((), ())), preferred_element_type=F32)


def _in_proj_kernel(x_ref, g_ref, w_ref, cos_ref, sin_ref,
                    q_ref, kv0_ref, kv1_ref, kv2_ref, qk_ref, vr_ref, gates_ref, h_ref):
    j = pl.program_id(1)

    @pl.when(j == 0)
    def _():
        h_ref[...] = (_rms(x_ref[...]) * g_ref[...]).astype(BF16)

    acc = _dot(h_ref[...], w_ref[...])

    @pl.when(j < 3)
    def _():
        q_ref[...] = acc * (HEAD_DIM ** -0.5)

    for g, kv_ref in enumerate((kv0_ref, kv1_ref, kv2_ref)):
        @pl.when((j == 3 + g) | (j == 6 + g))
        def _(kv_ref=kv_ref):
            kv_ref[...] = acc

    @pl.when((j == 9) | (j == 10))
    def _():
        scale = jnp.where(j == 10, RET_DK ** -0.5, 1.0).astype(F32)
        cos = cos_ref[...]
        sin = sin_ref[...]
        for hh in range(RET_HEADS):
            xh = acc[:, hh * RET_DK:(hh + 1) * RET_DK]
            r = xh * cos + pltpu.roll(xh, RET_DK // 2, 1) * sin
            qk_ref[:, hh * RET_DK:(hh + 1) * RET_DK] = r * scale

    @pl.when((j == 11) | (j == 12))
    def _():
        vr_ref[...] = acc

    @pl.when((j == 13) | (j == 14))
    def _():
        gates_ref[...] = acc * _sigmoid(acc)

    @pl.when(j >= 15)
    def _():
        gates_ref[...] = _sigmoid(acc)


def _in_proj(x2d, g, w_bf, cos_t, sin_t, tm):
    m = x2d.shape[0]
    assert m % tm == 0 and cos_t.shape[0] % tm == 0
    tpb = cos_t.shape[0] // tm
    tile = lambda f: pl.BlockSpec((tm, TN), f)
    kv_spec = lambda g_: tile(lambda i, j: (i, jnp.where(j >= 6 + g_, 1, 0)))
    out_shape = [
        jax.ShapeDtypeStruct((m, ATT_COLS), F32),
        jax.ShapeDtypeStruct((m, KV_COLS), F32),
        jax.ShapeDtypeStruct((m, KV_COLS), F32),
        jax.ShapeDtypeStruct((m, KV_COLS), F32),
        jax.ShapeDtypeStruct((m, 2 * RET_QK), F32),
        jax.ShapeDtypeStruct((m, RET_V), F32),
        jax.ShapeDtypeStruct((m, 3 * D_MODEL), F32),
    ]
    out_specs = [
        tile(lambda i, j: (i, jnp.minimum(j, 2))),
        kv_spec(0), kv_spec(1), kv_spec(2),
        tile(lambda i, j: (i, jnp.clip(j - 9, 0, 1))),
        tile(lambda i, j: (i, jnp.clip(j - 11, 0, 1))),
        tile(lambda i, j: (i, jnp.clip(j - 13, 0, 5))),
    ]
    return pl.pallas_call(
        _in_proj_kernel,
        grid=(m // tm, N_COL_TILES),
        in_specs=[
            pl.BlockSpec((tm, D_MODEL), lambda i, j: (i, 0)),
            pl.BlockSpec((1, D_MODEL), lambda i, j: (0, 0)),
            pl.BlockSpec((D_MODEL, TN), lambda i, j: (0, j)),
            pl.BlockSpec((tm, RET_DK), lambda i, j: (i % tpb, 0)),
            pl.BlockSpec((tm, RET_DK), lambda i, j: (i % tpb, 0)),
        ],
        out_specs=out_specs,
        out_shape=out_shape,
        scratch_shapes=[pltpu.VMEM((tm, D_MODEL), BF16)],
        compiler_params=_cparams(2, 48),
        name="in_proj",
    )(x2d, g, w_bf, cos_t, sin_t)


def _attn_block(q, kc, vc, kp, vp):
    row = lax.broadcasted_iota(jnp.int32, (BAND, BAND), 0)
    col = lax.broadcasted_iota(jnp.int32, (BAND, BAND), 1)
    qb = q.astype(BF16)
    sc = jnp.where(col <= row, _dot_nt(qb, kc.astype(BF16)), NEG_INF)
    m = jnp.max(sc, axis=1, keepdims=True)
    if kp is not None:
        sp = jnp.where(col >= row, _dot_nt(qb, kp.astype(BF16)), NEG_INF)
        m = jnp.maximum(m, jnp.max(sp, axis=1, keepdims=True))
    pc = jnp.exp(sc - m)
    l = jnp.sum(pc, axis=1, keepdims=True)
    u = _dot(pc.astype(BF16), vc.astype(BF16))
    if kp is not None:
        pp = jnp.exp(sp - m)
        l = l + jnp.sum(pp, axis=1, keepdims=True)
        u = u + _dot(pp.astype(BF16), vp.astype(BF16))
    return u, m, l


def _attn_prompt_kernel(q0, q1, q2, k0, v0, k1, v1, k2, v2, o_ref,
                        u0, u1, u2, m0, m1, m2, l0, l1, l2, *, seq):
    qs, ks, vs = (q0, q1, q2), (k0, k1, k2), (v0, v1, v2)
    us, ms, ls = (u0, u1, u2), (m0, m1, m2), (l0, l1, l2)

    for g, (_, dil) in enumerate(DIL_GROUPS):
        nb = seq // (dil * BAND)

        def rows(c, blk, dil=dil):
            if dil == 1:
                return pl.ds(blk * BAND, BAND)
            return pl.ds(c + blk * (BAND * dil), BAND, stride=dil)

        def do_block(c, blk, has_prev, g=g, rows=rows):
            rs = rows(c, blk)
            kp = vp = None
            if has_prev:
                rp = rows(c, blk - 1)
                kp, vp = ks[g][rp, :], vs[g][rp, :]
            u, m, l = _attn_block(qs[g][rs, :], ks[g][rs, :], vs[g][rs, :], kp, vp)
            us[g][rs, :] = u
            ms[g][rs, :] = jnp.broadcast_to(m, (BAND, HEAD_DIM))
            ls[g][rs, :] = jnp.broadcast_to(l, (BAND, HEAD_DIM))

        def per_residue(c, carry, nb=nb, do_block=do_block):
            do_block(c, 0, False)
            if nb > 1:
                def body(blk, cc):
                    do_block(c, blk, True)
                    return cc
                lax.fori_loop(1, nb, body, 0)
            return carry

        if dil == 1:
            per_residue(0, 0)
        else:
            lax.fori_loop(0, dil, per_residue, 0)

    rc = 256

    def combine(i, carry):
        rs = pl.ds(i * rc, rc)
        mg = [m_[rs, :] for m_ in ms]
        m = jnp.maximum(jnp.maximum(mg[0], mg[1]), mg[2])
        num = jnp.zeros((rc, HEAD_DIM), F32)
        den = jnp.zeros((rc, HEAD_DIM), F32)
        for g in range(N_GROUPS):
            w = jnp.exp(mg[g] - m)
            num = num + w * us[g][rs, :]
            den = den + w * ls[g][rs, :]
        o_ref[rs, :] = num / den
        return carry

    lax.fori_loop(0, seq // rc, combine, 0)


def _attn_prompt(q, kvs, batch, seq):
    assert seq % (BAND * DIL_GROUPS[-1][1]) == 0
    q3 = q.reshape(batch, seq, ATT_COLS)
    kv3 = [kv.reshape(batch, seq, KV_COLS) for kv in kvs]
    blk = lambda f: pl.BlockSpec((None, seq, HEAD_DIM), f)
    in_specs = [blk(functools.partial(lambda b, h, g: (b, 0, g * HEADS + h), g=g)) for g in range(N_GROUPS)]
    args = [q3, q3, q3]
    for g in range(N_GROUPS):
        in_specs += [blk(lambda b, h: (b, 0, h)), blk(lambda b, h: (b, 0, HEADS + h))]
        args += [kv3[g], kv3[g]]
    out = pl.pallas_call(
        functools.partial(_attn_prompt_kernel, seq=seq),
        grid=(batch, HEADS),
        in_specs=in_specs,
        out_specs=blk(lambda b, h: (b, 0, h)),
        out_shape=jax.ShapeDtypeStruct((batch, seq, HEADS * HEAD_DIM), F32),
        scratch_shapes=[pltpu.VMEM((seq, HEAD_DIM), F32)] * 9,
        compiler_params=_cparams(2, 40),
        name="attend_prompt",
    )(*args)
    return out.reshape(batch * seq, HEADS * HEAD_DIM)


SAMPLE_ROWS = 16


def _attn_sample_kernel(q_ref, kn0, kn1, kn2, c0, c1, c2, o_ref):
    kns, cs = (kn0, kn1, kn2), (c0, c1, c2)
    bt = q_ref.shape[0]
    hw = HEADS * HEAD_DIM
    row = lax.broadcasted_iota(jnp.int32, (SAMPLE_ROWS, hw), 0)
    lane_head = lax.broadcasted_iota(jnp.int32, (SAMPLE_ROWS, hw), 1) // HEAD_DIM
    diag = row == lane_head

    def per_row(bi, carry):
        b1 = pl.ds(bi, 1)
        us, ms, ls = [], [], []
        for g in range(N_GROUPS):
            qrow = q_ref[b1, g * hw:(g + 1) * hw]
            qd = jnp.where(diag, jnp.broadcast_to(qrow, (SAMPLE_ROWS, hw)), 0.0)
            k_new = kns[g][b1, 0:hw]
            v_new = kns[g][b1, hw:2 * hw]
            k_buf = cs[g][bi, :, 0:hw]
            v_buf = cs[g][bi, :, hw:2 * hw]
            s_buf = _dot_nt(qd.astype(BF16), k_buf.astype(BF16))
            s_new = jnp.sum(qd * k_new, axis=1, keepdims=True)
            m = jnp.maximum(jnp.max(s_buf, axis=1, keepdims=True), s_new)
            p_buf = jnp.exp(s_buf - m)
            p_new = jnp.exp(s_new - m)
            ls.append(jnp.sum(p_buf, axis=1, keepdims=True) + p_new)
            us.append(_dot(p_buf.astype(BF16), v_buf.astype(BF16)) + p_new * v_new)
            ms.append(m)
        m = jnp.maximum(jnp.maximum(ms[0], ms[1]), ms[2])
        num = jnp.zeros((SAMPLE_ROWS, hw), F32)
        den = jnp.zeros((SAMPLE_ROWS, 1), F32)
        for g in range(N_GROUPS):
            w = jnp.exp(ms[g] - m)
            num = num + w * us[g]
            den = den + w * ls[g]
        out = jnp.where(diag, num / den, 0.0)
        o_ref[b1, :] = jnp.sum(out, axis=0, keepdims=True)
        return carry

    for bi in range(bt):
        per_row(bi, 0)


def _attn_sample(q, kv_new, caches, bt):
    db = q.shape[0]
    assert db % bt == 0
    views = []
    for (win, dil), c in zip(DIL_GROUPS, caches):
        assert c.shape[1] == win and win // dil == BAND
        views.append(c.reshape(db, BAND, dil * KV_COLS))
    row_spec = lambda w: pl.BlockSpec((bt, w), lambda i: (i, 0))
    cache_spec = pl.BlockSpec((bt, BAND, KV_COLS), lambda i: (i, 0, 0))
    return pl.pallas_call(
        _attn_sample_kernel,
        grid=(db // bt,),
        in_specs=[row_spec(ATT_COLS)] + [row_spec(KV_COLS)] * 3 + [cache_spec] * 3,
        out_specs=row_spec(HEADS * HEAD_DIM),
        out_shape=jax.ShapeDtypeStruct((db, HEADS * HEAD_DIM), F32),
        compiler_params=_cparams(1, 40),
        name="attend_sample",
    )(q, *kv_new, *views)


def _group_norm_gate(o, gn, gate):
    mu = jnp.mean(o, axis=-1, keepdims=True)
    d = o - mu
    var = jnp.mean(d * d, axis=-1, keepdims=True)
    return gate * (d * lax.rsqrt(var + EPS) * gn)


def _ret_prompt_kernel(lg_ref, q_ref, k_ref, v_ref, gate_ref, gn_ref, o_ref, s_out_ref, s_scr, *, seq):
    c = RET_CHUNK
    lg = lg_ref[pl.program_id(1)]
    t = lax.broadcasted_iota(jnp.int32, (c, 1), 0).astype(F32)
    rel = (lax.broadcasted_iota(jnp.int32, (c, c), 0) - lax.broadcasted_iota(jnp.int32, (c, c), 1)).astype(F32)
    decay = jnp.where(rel >= 0, jnp.exp(lg * jnp.maximum(rel, 0.0)), 0.0)
    inner = jnp.exp(lg * (t + 1.0))
    tail = jnp.exp(lg * (c - 1.0 - t))
    chunk_decay = jnp.exp(lg * jnp.full((1, RET_DV), float(c), F32))
    gn = gn_ref[...]
    s_scr[...] = jnp.zeros_like(s_scr)

    def chunk(ci, carry):
        rs = pl.ds(ci * c, c)
        q, k = q_ref[rs, :], k_ref[rs, :]
        vb = v_ref[rs, :].astype(BF16)
        s = s_scr[...]
        a = _dot_nt(q.astype(BF16), k.astype(BF16)) * decay
        o = _dot(a.astype(BF16), vb) + _dot((q * inner).astype(BF16), s.astype(BF16))
        kt = (k * tail).T.astype(BF16)
        s_scr[...] = chunk_decay * s + _dot(kt, vb)
        o_ref[rs, :] = _group_norm_gate(o, gn, gate_ref[rs, :])
        return carry

    lax.fori_loop(0, seq // c, chunk, 0)
    s_out_ref[...] = s_scr[...]


def _ret_prompt(log_g, qk, v, gates, gn, batch, seq):
    assert seq % RET_CHUNK == 0
    qk3 = qk.reshape(batch, seq, 2 * RET_QK)
    v3 = v.reshape(batch, seq, RET_V)
    g3 = gates.reshape(batch, seq, 3 * D_MODEL)
    o, s = pl.pallas_call(
        functools.partial(_ret_prompt_kernel, seq=seq),
        grid=(batch, RET_HEADS),
        in_specs=[
            pl.BlockSpec(memory_space=pltpu.SMEM),
            pl.BlockSpec((None, seq, RET_DK), lambda b, h: (b, 0, h)),
            pl.BlockSpec((None, seq, RET_DK), lambda b, h: (b, 0, RET_HEADS + h)),
            pl.BlockSpec((None, seq, RET_DV), lambda b, h: (b, 0, h)),
            pl.BlockSpec((None, seq, RET_DV), lambda b, h: (b, 0, h)),
            pl.BlockSpec((1, RET_DV), lambda b, h: (0, h)),
        ],
        out_specs=[
            pl.BlockSpec((None, seq, RET_DV), lambda b, h: (b, 0, h)),
            pl.BlockSpec((None, None, RET_DK, RET_DV), lambda b, h: (b, h, 0, 0)),
        ],
        out_shape=[
            jax.ShapeDtypeStruct((batch, seq, RET_V), F32),
            jax.ShapeDtypeStruct((batch, RET_HEADS, RET_DK, RET_DV), F32),
        ],
        scratch_shapes=[pltpu.VMEM((RET_DK, RET_DV), F32)],
        compiler_params=_cparams(2, 32),
        name="retain_prompt",
    )(log_g, qk3, qk3, v3, g3, gn)
    return o.reshape(batch * seq, RET_V), s


def _ret_sample_kernel(lg_ref, q_ref, k_ref, v_ref, gate_ref, gn_ref, s_ref, o_ref, s_out_ref, o_scr):
    bt = q_ref.shape[0]
    eye = (lax.broadcasted_iota(jnp.int32, (RET_DK, RET_DK), 0)
           == lax.broadcasted_iota(jnp.int32, (RET_DK, RET_DK), 1))

    def column(r):
        return jnp.sum(jnp.where(eye, jnp.broadcast_to(r, (RET_DK, RET_DK)), 0.0), axis=1, keepdims=True)

    def per_row(bi, carry):
        b1 = pl.ds(bi, 1)
        for h in range(RET_HEADS):
            gamma = jnp.exp(lg_ref[h] * jnp.ones((1, RET_DV), F32))
            qc = column(q_ref[b1, h * RET_DK:(h + 1) * RET_DK])
            kc = column(k_ref[b1, h * RET_DK:(h + 1) * RET_DK])
            vrow = v_ref[b1, h * RET_DV:(h + 1) * RET_DV]
            s_new = gamma * s_ref[bi, h] + kc * vrow
            s_out_ref[bi, h] = s_new
            o_scr[b1, h * RET_DV:(h + 1) * RET_DV] = jnp.sum(qc * s_new, axis=0, keepdims=True)
        return carry

    for bi in range(bt):
        per_row(bi, 0)
    for h in range(RET_HEADS):
        seg = slice(h * RET_DV, (h + 1) * RET_DV)
        o_ref[:, seg] = _group_norm_gate(o_scr[:, seg], gn_ref[:, seg], gate_ref[:, seg])


def _ret_sample(log_g, qk, v, gates, gn, state, bt):
    db = qk.shape[0]
    assert db % bt == 0
    row_spec = lambda w, cb=0: pl.BlockSpec((bt, w), lambda i: (i, cb))
    st_spec = pl.BlockSpec((bt, RET_HEADS, RET_DK, RET_DV), lambda i: (i, 0, 0, 0))
    return pl.pallas_call(
        _ret_sample_kernel,
        grid=(db // bt,),
        in_specs=[
            pl.BlockSpec(memory_space=pltpu.SMEM),
            row_spec(RET_QK, 0), row_spec(RET_QK, 1), row_spec(RET_V), row_spec(RET_V),
            pl.BlockSpec((1, RET_V), lambda i: (0, 0)),
            st_spec,
        ],
        out_specs=[row_spec(RET_V), st_spec],
        out_shape=[
            jax.ShapeDtypeStruct((db, RET_V), F32),
            jax.ShapeDtypeStruct(state.shape, F32),
        ],
        scratch_shapes=[pltpu.VMEM((bt, RET_V), F32)],
        compiler_params=_cparams(1, 32),
        name="retain_sample",
    )(log_g, qk, qk, v, gates, gn, state)


def _merge_kernel(att_ref, ret_ref, ga_ref, gr_ref, x_ref, wa_ref, wr_ref, wo_ref, gpost_ref, gpre_ref,
                  x1_ref, h2_ref):
    a = _dot(att_ref[...].astype(BF16), wa_ref[...])
    r = _dot(ret_ref[...].astype(BF16), wr_ref[...])
    m = ga_ref[...] * a + gr_ref[...] * r
    y = _dot(m.astype(BF16), wo_ref[...])
    x1 = x_ref[...] + _rms(y) * gpost_ref[...]
    x1_ref[...] = x1
    h2_ref[...] = (_rms(x1) * gpre_ref[...]).astype(BF16)


def _merge(att, ret, gates, x2d, wa, wr, wo, gpost, gpre, tm):
    m = x2d.shape[0]
    assert m % tm == 0
    rows = lambda w, cb=0: pl.BlockSpec((tm, w), lambda i: (i, cb))
    full = lambda a: pl.BlockSpec(a.shape, lambda i: (0, 0))
    return pl.pallas_call(
        _merge_kernel,
        grid=(m // tm,),
        in_specs=[rows(HEADS * HEAD_DIM), rows(RET_V), rows(D_MODEL, 1), rows(D_MODEL, 2), rows(D_MODEL),
                  full(wa), full(wr), full(wo), full(gpost), full(gpre)],
        out_specs=[rows(D_MODEL), rows(D_MODEL)],
        out_shape=[jax.ShapeDtypeStruct((m, D_MODEL), F32), jax.ShapeDtypeStruct((m, D_MODEL), BF16)],
        compiler_params=_cparams(1, 48),
        name="merge",
    )(att, ret, gates, gates, x2d, wa, wr, wo, gpost, gpre)


FF_CHUNK = 1024


def _mlp_kernel(h_ref, x_ref, wu_ref, wd_ref, gpost_ref, y_ref):
    h = h_ref[...]
    acc = jnp.zeros(x_ref.shape, F32)
    for c in range(D_FF // FF_CHUNK):
        cs = slice(c * FF_CHUNK, (c + 1) * FF_CHUNK)
        u = jnp.maximum(_dot(h, wu_ref[:, cs]), 0.0)
        acc = acc + _dot((u * u).astype(BF16), wd_ref[cs, :])
    y_ref[...] = x_ref[...] + _rms(acc) * gpost_ref[...]


def _mlp(h2, x1, wu, wd, gpost, tm):
    m = x1.shape[0]
    assert m % tm == 0
    rows = pl.BlockSpec((tm, D_MODEL), lambda i: (i, 0))
    full = lambda a: pl.BlockSpec(a.shape, lambda i: (0, 0))
    return pl.pallas_call(
        _mlp_kernel,
        grid=(m // tm,),
        in_specs=[rows, rows, full(wu), full(wd), full(gpost)],
        out_specs=rows,
        out_shape=jax.ShapeDtypeStruct((m, D_MODEL), F32),
        compiler_params=_cparams(1, 56),
        name="mlp",
    )(h2, x1, wu, wd, gpost)


def _rope_tables(pos, rows):
    half = RET_DK // 2
    inv = ROPE_BASE ** (-jnp.arange(half, dtype=F32) / half)
    ang = pos.astype(F32)[:, None] * inv[None, :]
    cos, sin = jnp.cos(ang), jnp.sin(ang)
    cos_t = jnp.concatenate([cos, cos], axis=-1)
    sin_t = jnp.concatenate([-sin, sin], axis=-1)
    return (jnp.broadcast_to(cos_t, (rows, RET_DK)), jnp.broadcast_to(sin_t, (rows, RET_DK)))


PROMPT_TM = 512
SAMPLE_BT = 8


def _layer_tail(att, ret, gates, x2d, lw, tm):
    x1, h2 = _merge(att, ret, gates, x2d, lw["w_att_br"], lw["w_ret_br"], lw["w_out"],
                    lw["g_post_mix"], lw["g_pre_mlp"], tm)
    return _mlp(h2, x1, lw["w_up"], lw["w_down"], lw["g_post_mlp"], tm)


def _layer_prompt(x, lw, log_g):
    batch, seq, _ = x.shape
    x2d = x.reshape(batch * seq, D_MODEL)
    cos_t, sin_t = _rope_tables(jnp.arange(seq, dtype=jnp.int32), seq)
    q, kv0, kv1, kv2, qk, vr, gates = _in_proj(x2d, lw["g_pre_mix"], lw["w_in"], cos_t, sin_t, PROMPT_TM)
    att = _attn_prompt(q, (kv0, kv1, kv2), batch, seq)
    ret, state = _ret_prompt(log_g, qk, vr, gates, lw["g_ret_norm"], batch, seq)
    y = _layer_tail(att, ret, gates, x2d, lw, PROMPT_TM)
    rows = []
    for (win, _), kv in zip(DIL_GROUPS, (kv0, kv1, kv2)):
        keep = min(win, seq)
        kv = kv.reshape(batch, seq, 2, HEADS, HEAD_DIM)
        rows.append(kv if keep == seq else kv[:, seq - keep:])
    return y.reshape(batch, seq, D_MODEL), rows, state


def _layer_sample(x, caches, state, lw, log_g):
    db, t, _ = x.shape
    assert t == 1
    x2d = x.reshape(db, D_MODEL)
    cos_t, sin_t = _rope_tables(PAST_LEN + jnp.arange(1, dtype=jnp.int32), db)
    q, kv0, kv1, kv2, qk, vr, gates = _in_proj(x2d, lw["g_pre_mix"], lw["w_in"], cos_t, sin_t, db)
    att = _attn_sample(q, (kv0, kv1, kv2), caches, SAMPLE_BT)
    ret, state_new = _ret_sample(log_g, qk, vr, gates, lw["g_ret_norm"], state, SAMPLE_BT)
    y = _layer_tail(att, ret, gates, x2d, lw, db)
    rows = [kv.reshape(db, 1, 2, HEADS, HEAD_DIM) for kv in (kv0, kv1, kv2)]
    return y.reshape(db, 1, D_MODEL), rows, state_new


def _stack(xs):
    return xs[0][None] if len(xs) == 1 else jnp.stack(xs, axis=0)


def kernel(x_prompt, x_sample, cache_kv_d1, cache_kv_d4, cache_kv_d16, state_ret, w_in, w_att_br, w_ret_br, w_out, w_up, w_down, g_ret_norm, g_pre_mix, g_post_mix, g_pre_mlp, g_post_mlp):
    depth = w_in.shape[0]
    log_g = jnp.log1p(-jnp.power(2.0, -5.0 - jnp.arange(RET_HEADS, dtype=F32)))
    xp, xs = x_prompt, x_sample
    p_rows, s_rows = [[], [], []], [[], [], []]
    p_states, s_states = [], []
    for l in range(depth):
        lw = {
            "w_in": w_in[l].astype(BF16), "w_att_br": w_att_br[l].astype(BF16),
            "w_ret_br": w_ret_br[l].astype(BF16), "w_out": w_out[l].astype(BF16),
            "w_up": w_up[l].astype(BF16), "w_down": w_down[l].astype(BF16),
            "g_ret_norm": g_ret_norm[l].reshape(1, RET_V), "g_pre_mix": g_pre_mix[l].reshape(1, D_MODEL),
            "g_post_mix": g_post_mix[l].reshape(1, D_MODEL), "g_pre_mlp": g_pre_mlp[l].reshape(1, D_MODEL),
            "g_post_mlp": g_post_mlp[l].reshape(1, D_MODEL),
        }
        xp, rows_p, sp = _layer_prompt(xp, lw, log_g)
        xs, rows_s, ss = _layer_sample(xs, (cache_kv_d1[l], cache_kv_d4[l], cache_kv_d16[l]), state_ret[l], lw, log_g)
        for g in range(N_GROUPS):
            p_rows[g].append(rows_p[g])
            s_rows[g].append(rows_s[g])
        p_states.append(sp)
        s_states.append(ss)
    return (xp, xs, _stack(p_rows[0]), _stack(p_rows[1]), _stack(p_rows[2]), _stack(p_states),
            _stack(s_rows[0]), _stack(s_rows[1]), _stack(s_rows[2]), _stack(s_states))
```

```python
import functools

import jax
import jax.numpy as jnp
from jax import lax
from jax.experimental import pallas as pl
from jax.experimental.pallas import tpu as pltpu

F32 = jnp.float32
BF16 = jnp.bfloat16

D_MODEL = 1024
PAST_LEN = 8192

DIL_GROUPS = ((128, 1), (512, 4), (2048, 16))
N_GROUPS = 3
HEADS = 4
HEAD_DIM = 128
ATT_OUT = HEADS * HEAD_DIM
ATT_COLS = N_GROUPS * ATT_OUT
BAND = 128

RET_HEADS = 4
RET_DK = 128
RET_DV = 256
RET_QK = RET_HEADS * RET_DK
RET_V = RET_HEADS * RET_DV
RET_CHUNK = 128
ROPE_BASE = 10000.0

D_FF = 4 * D_MODEL
IN_COLS = 3 * ATT_COLS + 2 * RET_QK + 2 * RET_V + 2 * D_MODEL
EPS = 1e-6
NEG_INF = -1e30

TN = 512
N_COL_TILES = IN_COLS // TN
COL_Q, COL_K, COL_V, COL_RQ, COL_RK, COL_RV, COL_GSW, COL_GA, COL_GR = 0, 3, 6, 9, 10, 11, 13, 15, 17

V7X_VMEM_BYTES = 64 * 1024 * 1024
MIB = 1024 * 1024


def _cparams(n_axes, vmem_mib):
    assert vmem_mib * MIB < V7X_VMEM_BYTES
    return pltpu.CompilerParams(
        dimension_semantics=("arbitrary",) * n_axes,
        vmem_limit_bytes=vmem_mib * MIB,
    )


def _resident(shape, index_map):
    return pl.BlockSpec(shape, index_map, pipeline_mode=pl.Buffered(1))


def _rms(x):
    return x * lax.rsqrt(jnp.mean(x * x, axis=-1, keepdims=True) + EPS)


def _sigmoid(x):
    return 1.0 / (1.0 + jnp.exp(-x))


def _dot(a, b):
    return jnp.dot(a, b, preferred_element_type=F32)


def _dot_nt(a, b):
    return lax.dot_general(a, b, (((1,), (1,)), ((), ())), preferred_element_type=F32)


def _rope(x, cos, sin):
    return x * cos + pltpu.roll(x, RET_DK // 2, 1) * sin


def _in_proj_prompt_kernel(x_ref, g_ref, w_ref, cos_ref, sin_ref,
                           q0, q1, q2, k0, k1, k2, v0, v1, v2, kvo0, kvo1, kvo2,
                           rq_ref, rk_ref, rv_ref, gsw_ref, ga_ref, gr_ref, scr_q, scr_k, scr_v):
    tm = x_ref.shape[0]
    h = (_rms(x_ref[...]) * g_ref[...]).astype(BF16)

    def tile(j):
        return _dot(h, w_ref[:, j * TN:(j + 1) * TN])

    def head(acc, hh, width=HEAD_DIM):
        return acc[:, hh * width:(hh + 1) * width]

    def scatter_residues(acc, scr, dst, dil):
        if dil == 1:
            for hh in range(HEADS):
                dst[hh, 0] = head(acc, hh).astype(BF16)
            return
        for hh in range(HEADS):
            scr[hh] = head(acc, hh)
        for hh in range(HEADS):
            for c in range(dil):
                dst[hh, c] = scr[hh, pl.ds(c, tm // dil, stride=dil), :].astype(BF16)

    def kv_rows(acc, kvo, which):
        keep = kvo.shape[0]
        for hh in range(HEADS):
            kvo[:, which, hh, :] = head(acc, hh)[tm - keep:, :]

    for g, (_, dil) in enumerate(DIL_GROUPS):
        scatter_residues(tile(COL_Q + g) * (HEAD_DIM ** -0.5), scr_q, (q0, q1, q2)[g], dil)
        acc = tile(COL_K + g)
        scatter_residues(acc, scr_k, (k0, k1, k2)[g], dil)
        kv_rows(acc, (kvo0, kvo1, kvo2)[g], 0)
        acc = tile(COL_V + g)
        scatter_residues(acc, scr_v, (v0, v1, v2)[g], dil)
        kv_rows(acc, (kvo0, kvo1, kvo2)[g], 1)

    cos, sin = cos_ref[...], sin_ref[...]
    acc = tile(COL_RQ)
    for hh in range(RET_HEADS):
        rq_ref[hh] = _rope(head(acc, hh), cos, sin)
    acc = tile(COL_RK)
    for hh in range(RET_HEADS):
        rk_ref[hh] = _rope(head(acc, hh), cos, sin) * (RET_DK ** -0.5)
    for e in range(2):
        acc = tile(COL_RV + e)
        for s in range(2):
            rv_ref[2 * e + s] = head(acc, s, RET_DV).astype(BF16)
        acc = tile(COL_GSW + e)
        for s in range(2):
            gsw_ref[2 * e + s] = head(acc, s, RET_DV)
        ga_ref[:, e * TN:(e + 1) * TN] = tile(COL_GA + e)
        gr_ref[:, e * TN:(e + 1) * TN] = tile(COL_GR + e)


def _in_proj_prompt(x2d, g, w_bf, cos_t, sin_t, batch, seq, tm):
    assert seq % tm == 0 and tm % (16 * DIL_GROUPS[-1][1]) == 0
    tps = seq // tm
    out_shape, out_specs = [], []
    for _ in range(3):
        for _, dil in DIL_GROUPS:
            out_shape.append(jax.ShapeDtypeStruct((batch, HEADS, dil, seq // dil, HEAD_DIM), BF16))
            out_specs.append(pl.BlockSpec((None, HEADS, dil, tm // dil, HEAD_DIM), lambda b, t: (b, 0, 0, t, 0)))
    for win, _ in DIL_GROUPS:
        keep = min(win, seq)
        rows = min(keep, tm)
        assert keep % rows == 0
        first = (seq - keep) // rows
        out_shape.append(jax.ShapeDtypeStruct((batch, keep, 2, HEADS, HEAD_DIM), F32))
        out_specs.append(pl.BlockSpec(
            (None, rows, 2, HEADS, HEAD_DIM),
            functools.partial(lambda b, t, first, per: (b, jnp.maximum((t + 1) * per - 1 - first, 0), 0, 0, 0),
                              first=first, per=tm // rows)))
    hm = lambda width, dt: (jax.ShapeDtypeStruct((batch, RET_HEADS, seq, width), dt),
                            pl.BlockSpec((None, RET_HEADS, tm, width), lambda b, t: (b, 0, t, 0)))
    nat = (jax.ShapeDtypeStruct((batch * seq, D_MODEL), F32),
           pl.BlockSpec((tm, D_MODEL), lambda b, t: (b * tps + t, 0)))
    for shp, spec in (hm(RET_DK, F32), hm(RET_DK, F32), hm(RET_DV, BF16), hm(RET_DV, F32), nat, nat):
        out_shape.append(shp)
        out_specs.append(spec)
    return pl.pallas_call(
        _in_proj_prompt_kernel,
        grid=(batch, tps),
        in_specs=[
            pl.BlockSpec((tm, D_MODEL), lambda b, t: (b * tps + t, 0)),
            _resident((1, D_MODEL), lambda b, t: (0, 0)),
            _resident((D_MODEL, IN_COLS), lambda b, t: (0, 0)),
            pl.BlockSpec((tm, RET_DK), lambda b, t: (t, 0)),
            pl.BlockSpec((tm, RET_DK), lambda b, t: (t, 0)),
        ],
        out_specs=out_specs,
        out_shape=out_shape,
        scratch_shapes=[pltpu.VMEM((HEADS, tm, HEAD_DIM), F32)] * 3,
        compiler_params=_cparams(2, 56),
        name="in_proj_prompt",
    )(x2d, g, w_bf, cos_t, sin_t)


def _in_proj_sample_kernel(x_ref, g_ref, w_ref, z_ref, h_ref):
    @pl.when(pl.program_id(0) == 0)
    def _():
        h_ref[...] = (_rms(x_ref[...]) * g_ref[...]).astype(BF16)

    z_ref[...] = _dot(h_ref[...], w_ref[...])


def _in_proj_sample(x2d, g, w_bf):
    m = x2d.shape[0]
    return pl.pallas_call(
        _in_proj_sample_kernel,
        grid=(N_COL_TILES,),
        in_specs=[
            _resident((m, D_MODEL), lambda j: (0, 0)),
            _resident((1, D_MODEL), lambda j: (0, 0)),
            pl.BlockSpec((D_MODEL, TN), lambda j: (0, j)),
        ],
        out_specs=pl.BlockSpec((m, TN), lambda j: (0, j)),
        out_shape=jax.ShapeDtypeStruct((m, IN_COLS), F32),
        scratch_shapes=[pltpu.VMEM((m, D_MODEL), BF16)],
        compiler_params=_cparams(1, 16),
        name="in_proj_sample",
    )(x2d, g, w_bf)


def _softmax_block(s, v):
    m = jnp.max(s, axis=1, keepdims=True)
    p = jnp.exp(s - m)
    return _dot(p.astype(BF16), v), m, jnp.sum(p, axis=1, keepdims=True)


def _attn_prompt_kernel(q0, q1, q2, k0, k1, k2, v0, v1, v2, o_ref,
                        u0, u1, u2, m0, m1, m2, l0, l1, l2, *, seq):
    qs, ks, vs = (q0, q1, q2), (k0, k1, k2), (v0, v1, v2)
    us, ms, ls = (u0, u1, u2), (m0, m1, m2), (l0, l1, l2)
    row1 = lax.broadcasted_iota(jnp.int32, (BAND, BAND), 0)
    col1 = lax.broadcasted_iota(jnp.int32, (BAND, BAND), 1)
    causal = col1 <= row1
    row2 = lax.broadcasted_iota(jnp.int32, (BAND, 2 * BAND), 0)
    col2 = lax.broadcasted_iota(jnp.int32, (BAND, 2 * BAND), 1)
    band = (col2 >= row2) & (col2 <= row2 + BAND)

    for g, (_, dil) in enumerate(DIL_GROUPS):
        nb = seq // (dil * BAND)

        def store(c, blk, res, g=g, dil=dil):
            u, m, l = res
            if dil == 1:
                rs = pl.ds(pl.multiple_of(blk * BAND, BAND), BAND)
            else:
                rs = pl.ds(c + blk * (BAND * dil), BAND, stride=dil)
            us[g][rs, :] = u
            ms[g][rs, :] = jnp.broadcast_to(m, (BAND, HEAD_DIM))
            ls[g][rs, :] = jnp.broadcast_to(l, (BAND, HEAD_DIM))

        def first_block(c, g=g, store=store):
            s = jnp.where(causal, _dot_nt(qs[g][c, 0:BAND, :], ks[g][c, 0:BAND, :]), NEG_INF)
            store(c, 0, _softmax_block(s, vs[g][c, 0:BAND, :]))

        def later_block(c, blk, g=g, store=store):
            own = pl.ds(pl.multiple_of(blk * BAND, BAND), BAND)
            both = pl.ds(pl.multiple_of((blk - 1) * BAND, BAND), 2 * BAND)
            s = jnp.where(band, _dot_nt(qs[g][c, own, :], ks[g][c, both, :]), NEG_INF)
            store(c, blk, _softmax_block(s, vs[g][c, both, :]))

        if dil == 1:
            first_block(0)
            per = 3
            assert (nb - 1) % per == 0

            def body(i, carry, later_block=later_block, per=per):
                for e in range(per):
                    later_block(0, 1 + i * per + e)
                return carry
            lax.fori_loop(0, (nb - 1) // per, body, 0)
        elif nb > 1:
            def body(c, carry, first_block=first_block, later_block=later_block, nb=nb):
                first_block(c)
                for blk in range(1, nb):
                    later_block(c, blk)
                return carry
            lax.fori_loop(0, dil, body, 0)
        else:
            per = 4
            assert dil % per == 0

            def body(i, carry, first_block=first_block, per=per):
                for e in range(per):
                    first_block(i * per + e)
                return carry
            lax.fori_loop(0, dil // per, body, 0)

    rc = 256

    def combine(i, carry):
        rs = pl.ds(pl.multiple_of(i * rc, rc), rc)
        mg = [m_[rs, :] for m_ in ms]
        m = jnp.maximum(jnp.maximum(mg[0], mg[1]), mg[2])
        num = jnp.zeros((rc, HEAD_DIM), F32)
        den = jnp.zeros((rc, HEAD_DIM), F32)
        for g in range(N_GROUPS):
            w = jnp.exp(mg[g] - m)
            num = num + w * us[g][rs, :]
            den = den + w * ls[g][rs, :]
        o_ref[rs, :] = (num / den).astype(o_ref.dtype)
        return carry

    lax.fori_loop(0, seq // rc, combine, 0)


def _attn_prompt(qkv, batch, seq):
    assert seq % (BAND * DIL_GROUPS[-1][1]) == 0
    in_specs = [pl.BlockSpec((None, None) + a.shape[2:], lambda b, h: (b, h, 0, 0, 0)) for a in qkv]
    return pl.pallas_call(
        functools.partial(_attn_prompt_kernel, seq=seq),
        grid=(batch, HEADS),
        in_specs=in_specs,
        out_specs=pl.BlockSpec((None, None, seq, HEAD_DIM), lambda b, h: (b, h, 0, 0)),
        out_shape=jax.ShapeDtypeStruct((batch, HEADS, seq, HEAD_DIM), BF16),
        scratch_shapes=[pltpu.VMEM((seq, HEAD_DIM), F32)] * 9,
        compiler_params=_cparams(2, 32),
        name="attend_prompt",
    )(*qkv)


SAMPLE_ROWS = 16


def _attn_sample_kernel(z_ref, c0, c1, c2, o_ref):
    cs = (c0, c1, c2)
    bt = o_ref.shape[1]
    row = lax.broadcasted_iota(jnp.int32, (SAMPLE_ROWS, ATT_OUT), 0)
    lane_head = lax.broadcasted_iota(jnp.int32, (SAMPLE_ROWS, ATT_OUT), 1) // HEAD_DIM
    diag = row == lane_head

    def cols(tile0, g):
        return slice((tile0 + g) * TN, (tile0 + g + 1) * TN)

    for bi in range(bt):
        b1 = pl.ds(bi, 1)
        us, ms, ls = [], [], []
        for g in range(N_GROUPS):
            qrow = z_ref[b1, cols(COL_Q, g)] * (HEAD_DIM ** -0.5)
            qd = jnp.where(diag, jnp.broadcast_to(qrow, (SAMPLE_ROWS, ATT_OUT)), 0.0)
            k_new = z_ref[b1, cols(COL_K, g)]
            v_new = z_ref[b1, cols(COL_V, g)]
            k_buf = jnp.concatenate([cs[g][bi, :, 0, hh, :] for hh in range(HEADS)], axis=1)
            v_buf = jnp.concatenate([cs[g][bi, :, 1, hh, :] for hh in range(HEADS)], axis=1)
            s_buf = _dot_nt(qd.astype(BF16), k_buf.astype(BF16))
            s_new = jnp.sum(qd * k_new, axis=1, keepdims=True)
            m = jnp.maximum(jnp.max(s_buf, axis=1, keepdims=True), s_new)
            p_buf = jnp.exp(s_buf - m)
            p_new = jnp.exp(s_new - m)
            ls.append(jnp.sum(p_buf, axis=1, keepdims=True) + p_new)
            us.append(_dot(p_buf.astype(BF16), v_buf.astype(BF16)) + p_new * v_new)
            ms.append(m)
        m = jnp.maximum(jnp.maximum(ms[0], ms[1]), ms[2])
        num = jnp.zeros((SAMPLE_ROWS, ATT_OUT), F32)
        den = jnp.zeros((SAMPLE_ROWS, 1), F32)
        for g in range(N_GROUPS):
            w = jnp.exp(ms[g] - m)
            num = num + w * us[g]
            den = den + w * ls[g]
        out = num / den
        for hh in range(HEADS):
            o_ref[hh, b1, :] = out[hh:hh + 1, hh * HEAD_DIM:(hh + 1) * HEAD_DIM]


def _attn_sample(z, caches, layer, bt):
    db = z.shape[0]
    assert db % bt == 0
    nbt = db // bt
    views, specs = [], []
    for (win, dil), c in zip(DIL_GROUPS, caches):
        assert c.shape[2] == win and win // dil == BAND
        views.append(c.reshape(c.shape[0] * db, BAND, dil, 2, HEADS, HEAD_DIM))
        specs.append(pl.BlockSpec((bt, BAND, None, 2, HEADS, HEAD_DIM),
                                  lambda i: (layer * nbt + i, 0, 0, 0, 0, 0)))
    return pl.pallas_call(
        _attn_sample_kernel,
        grid=(nbt,),
        in_specs=[pl.BlockSpec((bt, IN_COLS), lambda i: (i, 0))] + specs,
        out_specs=pl.BlockSpec((HEADS, bt, HEAD_DIM), lambda i: (0, i, 0)),
        out_shape=jax.ShapeDtypeStruct((HEADS, db, HEAD_DIM), F32),
        compiler_params=_cparams(1, 40),
        name="attend_sample",
    )(z, *views)


def _group_norm_gate(o, gn, gate_logit):
    mu = jnp.mean(o, axis=-1, keepdims=True)
    d = o - mu
    var = jnp.mean(d * d, axis=-1, keepdims=True)
    return (gate_logit * _sigmoid(gate_logit)) * (d * lax.rsqrt(var + EPS) * gn)


def _ret_prompt_kernel(lg_ref, q_ref, k_ref, v_ref, gate_ref, gn_ref, o_ref, s_out_ref, s_scr):
    c = RET_CHUNK
    rows = q_ref.shape[1]
    t = lax.broadcasted_iota(jnp.int32, (c, 1), 0).astype(F32)
    rel = (lax.broadcasted_iota(jnp.int32, (c, c), 0) - lax.broadcasted_iota(jnp.int32, (c, c), 1)).astype(F32)

    @pl.when(pl.program_id(1) == 0)
    def _():
        s_scr[...] = jnp.zeros_like(s_scr)

    consts = []
    for h in range(RET_HEADS):
        lg = lg_ref[h]
        consts.append((
            jnp.where(rel >= 0, jnp.exp(lg * jnp.maximum(rel, 0.0)), 0.0),
            jnp.exp(lg * (t + 1.0)),
            jnp.exp(lg * (c - 1.0 - t)),
            jnp.exp(lg * jnp.full((1, RET_DV), float(c), F32)),
        ))

    def chunk(ci, carry):
        rs = pl.ds(pl.multiple_of(ci * c, c), c)
        for h in range(RET_HEADS):
            decay, inner, tail, chunk_decay = consts[h]
            q, k, vb = q_ref[h, rs, :], k_ref[h, rs, :], v_ref[h, rs, :]
            s = s_scr[h]
            a = _dot_nt(q.astype(BF16), k.astype(BF16)) * decay
            o = _dot(a.astype(BF16), vb) + _dot((q * inner).astype(BF16), s.astype(BF16))
            s_scr[h] = chunk_decay * s + _dot((k * tail).T.astype(BF16), vb)
            gn = gn_ref[:, h * RET_DV:(h + 1) * RET_DV]
            o_ref[h, rs, :] = _group_norm_gate(o, gn, gate_ref[h, rs, :]).astype(o_ref.dtype)
        return carry

    lax.fori_loop(0, rows // c, chunk, 0)

    @pl.when(pl.program_id(1) == pl.num_programs(1) - 1)
    def _():
        s_out_ref[...] = s_scr[...]


def _ret_prompt(log_g, rq, rk, rv, gsw, gn, batch, seq, rows):
    assert seq % rows == 0 and rows % RET_CHUNK == 0
    hm = lambda width: pl.BlockSpec((None, RET_HEADS, rows, width), lambda b, t: (b, 0, t, 0))
    return pl.pallas_call(
        _ret_prompt_kernel,
        grid=(batch, seq // rows),
        in_specs=[
            pl.BlockSpec(memory_space=pltpu.SMEM),
            hm(RET_DK), hm(RET_DK), hm(RET_DV), hm(RET_DV),
            _resident((1, RET_V), lambda b, t: (0, 0)),
        ],
        out_specs=[
            hm(RET_DV),
            pl.BlockSpec((None, RET_HEADS, RET_DK, RET_DV), lambda b, t: (b, 0, 0, 0)),
        ],
        out_shape=[
            jax.ShapeDtypeStruct((batch, RET_HEADS, seq, RET_DV), BF16),
            jax.ShapeDtypeStruct((batch, RET_HEADS, RET_DK, RET_DV), F32),
        ],
        scratch_shapes=[pltpu.VMEM((RET_HEADS, RET_DK, RET_DV), F32)],
        compiler_params=_cparams(2, 40),
        name="retain_prompt",
    )(log_g, rq, rk, rv, gsw, gn)


def _ret_sample_kernel(lg_ref, z_ref, cos_ref, sin_ref, gn_ref, s_ref, o_ref, s_out_ref, o_scr):
    bt = z_ref.shape[0]
    eye = (lax.broadcasted_iota(jnp.int32, (RET_DK, RET_DK), 0)
           == lax.broadcasted_iota(jnp.int32, (RET_DK, RET_DK), 1))
    cos, sin = cos_ref[...], sin_ref[...]

    def column(r):
        return jnp.sum(jnp.where(eye, jnp.broadcast_to(r, (RET_DK, RET_DK)), 0.0), axis=1, keepdims=True)

    def zcols(tile0, off, width):
        return z_ref[:, tile0 * TN + off:tile0 * TN + off + width]

    for h in range(RET_HEADS):
        gamma = jnp.exp(lg_ref[h] * jnp.ones((1, RET_DV), F32))
        q_h = _rope(zcols(COL_RQ, h * RET_DK, RET_DK), cos, sin)
        k_h = _rope(zcols(COL_RK, h * RET_DK, RET_DK), cos, sin) * (RET_DK ** -0.5)
        v_h = zcols(COL_RV, h * RET_DV, RET_DV)
        for bi in range(bt):
            b1 = pl.ds(bi, 1)
            qc = column(q_h[bi:bi + 1, :])
            kc = column(k_h[bi:bi + 1, :])
            vrow = v_h[bi:bi + 1, :]
            s_new = gamma * s_ref[bi, h] + kc * vrow
            s_out_ref[bi, h] = s_new
            o_scr[b1, h * RET_DV:(h + 1) * RET_DV] = jnp.sum(qc * s_new, axis=0, keepdims=True)
    for h in range(RET_HEADS):
        seg = slice(h * RET_DV, (h + 1) * RET_DV)
        gate = z_ref[:, COL_GSW * TN + h * RET_DV:COL_GSW * TN + (h + 1) * RET_DV]
        o_ref[h] = _group_norm_gate(o_scr[:, seg], gn_ref[:, seg], gate)


def _ret_sample(log_g, z, cos_t, sin_t, gn, state, layer, bt):
    db = z.shape[0]
    assert db % bt == 0
    nbt = db // bt
    st = state.reshape((state.shape[0] * db,) + state.shape[2:])
    st_block = (bt, RET_HEADS, RET_DK, RET_DV)
    return pl.pallas_call(
        _ret_sample_kernel,
        grid=(nbt,),
        in_specs=[
            pl.BlockSpec(memory_space=pltpu.SMEM),
            pl.BlockSpec((bt, IN_COLS), lambda i: (i, 0)),
            _resident((1, RET_DK), lambda i: (0, 0)),
            _resident((1, RET_DK), lambda i: (0, 0)),
            _resident((1, RET_V), lambda i: (0, 0)),
            pl.BlockSpec(st_block, lambda i: (layer * nbt + i, 0, 0, 0)),
        ],
        out_specs=[
            pl.BlockSpec((RET_HEADS, bt, RET_DV), lambda i: (0, i, 0)),
            pl.BlockSpec(st_block, lambda i: (i, 0, 0, 0)),
        ],
        out_shape=[
            jax.ShapeDtypeStruct((RET_HEADS, db, RET_DV), F32),
            jax.ShapeDtypeStruct((db, RET_HEADS, RET_DK, RET_DV), F32),
        ],
        scratch_shapes=[pltpu.VMEM((bt, RET_V), F32)],
        compiler_params=_cparams(1, 32),
        name="retain_sample",
    )(log_g, z, cos_t, sin_t, gn, st)


def _merge_kernel(att_ref, ret_ref, ga_ref, gr_ref, x_ref, wa_ref, wr_ref, wo_ref, gpost_ref, gpre_ref,
                  x1_ref, h2_ref):
    att = jnp.concatenate([att_ref[hh].astype(BF16) for hh in range(HEADS)], axis=1)
    ret = jnp.concatenate([ret_ref[hh].astype(BF16) for hh in range(RET_HEADS)], axis=1)
    m = _sigmoid(ga_ref[...]) * _dot(att, wa_ref[...]) + _sigmoid(gr_ref[...]) * _dot(ret, wr_ref[...])
    y = _dot(m.astype(BF16), wo_ref[...])
    x1 = x_ref[...] + _rms(y) * gpost_ref[...]
    x1_ref[...] = x1
    h2_ref[...] = (_rms(x1) * gpre_ref[...]).astype(BF16)


def _merge(att, ret, ga, gr, x2d, wa, wr, wo, gpost, gpre, seq, tm):
    m = x2d.shape[0]
    assert m % tm == 0 and seq % tm == 0
    tps = seq // tm
    rows = pl.BlockSpec((tm, D_MODEL), lambda i: (i, 0))
    hm = lambda a: pl.BlockSpec((None,) + a.shape[1:2] + (tm,) + a.shape[3:], lambda i: (i // tps, 0, i % tps, 0))
    full = lambda a: _resident(a.shape, lambda i: (0, 0))
    return pl.pallas_call(
        _merge_kernel,
        grid=(m // tm,),
        in_specs=[hm(att), hm(ret), rows, rows, rows,
                  full(wa), full(wr), full(wo), full(gpost), full(gpre)],
        out_specs=[rows, rows],
        out_shape=[jax.ShapeDtypeStruct((m, D_MODEL), F32), jax.ShapeDtypeStruct((m, D_MODEL), BF16)],
        compiler_params=_cparams(1, 40),
        name="merge",
    )(att, ret, ga, gr, x2d, wa, wr, wo, gpost, gpre)


FF_CHUNK = 1024


def _mlp_kernel(h_ref, x_ref, wu_ref, wd_ref, gpost_ref, y_ref):
    h = h_ref[...]
    acc = jnp.zeros(x_ref.shape, F32)
    for c in range(D_FF // FF_CHUNK):
        cs = slice(c * FF_CHUNK, (c + 1) * FF_CHUNK)
        u = jnp.maximum(_dot(h, wu_ref[:, cs]), 0.0)
        acc = acc + _dot((u * u).astype(BF16), wd_ref[cs, :])
    y_ref[...] = x_ref[...] + _rms(acc) * gpost_ref[...]


def _mlp(h2, x1, wu, wd, gpost, tm):
    m = x1.shape[0]
    assert m % tm == 0
    rows = pl.BlockSpec((tm, D_MODEL), lambda i: (i, 0))
    full = lambda a: _resident(a.shape, lambda i: (0, 0))
    return pl.pallas_call(
        _mlp_kernel,
        grid=(m // tm,),
        in_specs=[rows, rows, full(wu), full(wd), full(gpost)],
        out_specs=rows,
        out_shape=jax.ShapeDtypeStruct((m, D_MODEL), F32),
        compiler_params=_cparams(1, 40),
        name="mlp",
    )(h2, x1, wu, wd, gpost)


def _rope_tables(pos):
    half = RET_DK // 2
    inv = ROPE_BASE ** (-jnp.arange(half, dtype=F32) / half)
    ang = pos.astype(F32)[:, None] * inv[None, :]
    cos, sin = jnp.cos(ang), jnp.sin(ang)
    return jnp.concatenate([cos, cos], axis=-1), jnp.concatenate([-sin, sin], axis=-1)


IN_PROJ_TM = 256
TAIL_TM = 512
RET_ROWS = 1024
SAMPLE_BT = 8


def _layer_tail(att, ret, ga, gr, x2d, lw, seq, tm):
    x1, h2 = _merge(att, ret, ga, gr, x2d, lw["w_att_br"], lw["w_ret_br"], lw["w_out"],
                    lw["g_post_mix"], lw["g_pre_mlp"], seq, tm)
    return _mlp(h2, x1, lw["w_up"], lw["w_down"], lw["g_post_mlp"], tm)


def _layer_prompt(x, lw, log_g):
    batch, seq, _ = x.shape
    x2d = x.reshape(batch * seq, D_MODEL)
    cos_t, sin_t = _rope_tables(jnp.arange(seq, dtype=jnp.int32))
    outs = _in_proj_prompt(x2d, lw["g_pre_mix"], lw["w_in"], cos_t, sin_t, batch, seq, IN_PROJ_TM)
    qkv, kv_rows = outs[:9], outs[9:12]
    rq, rk, rv, gsw, ga, gr = outs[12:]
    att = _attn_prompt(qkv, batch, seq)
    ret, state = _ret_prompt(log_g, rq, rk, rv, gsw, lw["g_ret_norm"], batch, seq, RET_ROWS)
    y = _layer_tail(att, ret, ga, gr, x2d, lw, seq, TAIL_TM)
    return y.reshape(batch, seq, D_MODEL), kv_rows, state


def _layer_sample(x, caches, state, layer, lw, log_g):
    db, t, _ = x.shape
    assert t == 1
    x2d = x.reshape(db, D_MODEL)
    cos_t, sin_t = _rope_tables(PAST_LEN + jnp.arange(1, dtype=jnp.int32))
    z = _in_proj_sample(x2d, lw["g_pre_mix"], lw["w_in"])
    att = _attn_sample(z, caches, layer, SAMPLE_BT)
    ret, state_new = _ret_sample(log_g, z, cos_t, sin_t, lw["g_ret_norm"], state, layer, SAMPLE_BT)
    ga = z[:, COL_GA * TN:COL_GA * TN + D_MODEL]
    gr = z[:, COL_GR * TN:COL_GR * TN + D_MODEL]
    y = _layer_tail(att[None], ret[None], ga, gr, x2d, lw, db, db)
    rows = []
    for g in range(N_GROUPS):
        k_new = z[:, (COL_K + g) * TN:(COL_K + g + 1) * TN]
        v_new = z[:, (COL_V + g) * TN:(COL_V + g + 1) * TN]
        rows.append(jnp.stack([k_new, v_new], axis=1).reshape(db, 1, 2, HEADS, HEAD_DIM))
    return y.reshape(db, 1, D_MODEL), rows, state_new


def _stack(xs):
    return xs[0][None] if len(xs) == 1 else jnp.stack(xs, axis=0)


def kernel(x_prompt, x_sample, cache_kv_d1, cache_kv_d4, cache_kv_d16, state_ret, w_in, w_att_br, w_ret_br, w_out, w_up, w_down, g_ret_norm, g_pre_mix, g_post_mix, g_pre_mlp, g_post_mlp):
    depth = w_in.shape[0]
    log_g = jnp.log1p(-jnp.power(2.0, -5.0 - jnp.arange(RET_HEADS, dtype=F32)))
    caches = (cache_kv_d1, cache_kv_d4, cache_kv_d16)
    xp, xs = x_prompt, x_sample
    p_rows, s_rows = [[], [], []], [[], [], []]
    p_states, s_states = [], []
    for l in range(depth):
        lw = {
            "w_in": w_in[l].astype(BF16), "w_att_br": w_att_br[l].astype(BF16),
            "w_ret_br": w_ret_br[l].astype(BF16), "w_out": w_out[l].astype(BF16),
            "w_up": w_up[l].astype(BF16), "w_down": w_down[l].astype(BF16),
            "g_ret_norm": g_ret_norm[l].reshape(1, RET_V), "g_pre_mix": g_pre_mix[l].reshape(1, D_MODEL),
            "g_post_mix": g_post_mix[l].reshape(1, D_MODEL), "g_pre_mlp": g_pre_mlp[l].reshape(1, D_MODEL),
            "g_post_mlp": g_post_mlp[l].reshape(1, D_MODEL),
        }
        xp, rows_p, sp = _layer_prompt(xp, lw, log_g)
        xs, rows_s, ss = _layer_sample(xs, caches, state_ret, l, lw, log_g)
        for g in range(N_GROUPS):
            p_rows[g].append(rows_p[g])
            s_rows[g].append(rows_s[g])
        p_states.append(sp)
        s_states.append(ss)
    return (xp, xs, _stack(p_rows[0]), _stack(p_rows[1]), _stack(p_rows[2]), _stack(p_states),
            _stack(s_rows[0]), _stack(s_rows[1]), _stack(s_rows[2]), _stack(s_states))
```

```python
import functools

import jax
import jax.numpy as jnp
from jax import lax
from jax.experimental import pallas as pl
from jax.experimental.pallas import tpu as pltpu

F32 = jnp.float32
BF16 = jnp.bfloat16

D_MODEL = 1024
PAST_LEN = 8192

DIL_GROUPS = ((128, 1), (512, 4), (2048, 16))
N_GROUPS = 3
HEADS = 4
HEAD_DIM = 128
ATT_OUT = HEADS * HEAD_DIM
ATT_COLS = N_GROUPS * ATT_OUT
BAND = 128

RET_HEADS = 4
RET_DK = 128
RET_DV = 256
RET_QK = RET_HEADS * RET_DK
RET_V = RET_HEADS * RET_DV
RET_CHUNK = 128
ROPE_BASE = 10000.0

D_FF = 4 * D_MODEL
IN_COLS = 3 * ATT_COLS + 2 * RET_QK + 2 * RET_V + 2 * D_MODEL
EPS = 1e-6
NEG_INF = -1e30

TN = 512
N_COL_TILES = IN_COLS // TN
COL_Q, COL_K, COL_V, COL_RQ, COL_RK, COL_RV, COL_GSW, COL_GA, COL_GR = 0, 3, 6, 9, 10, 11, 13, 15, 17

V7X_VMEM_BYTES = 64 * 1024 * 1024
MIB = 1024 * 1024


def _cparams(n_axes, vmem_mib):
    assert vmem_mib * MIB < V7X_VMEM_BYTES
    return pltpu.CompilerParams(
        dimension_semantics=("arbitrary",) * n_axes,
        vmem_limit_bytes=vmem_mib * MIB,
    )


def _resident(shape, index_map):
    return pl.BlockSpec(shape, index_map, pipeline_mode=pl.Buffered(1))


def _rms(x):
    return x * lax.rsqrt(jnp.mean(x * x, axis=-1, keepdims=True) + EPS)


def _sigmoid(x):
    return 1.0 / (1.0 + jnp.exp(-x))


def _dot(a, b):
    return jnp.dot(a, b, preferred_element_type=F32)


def _dot_nt(a, b):
    return lax.dot_general(a, b, (((1,), (1,)), ((), ())), preferred_element_type=F32)


def _rope(x, cos, sin):
    return x * cos + pltpu.roll(x, RET_DK // 2, 1) * sin


def _in_proj_prompt_kernel(x_ref, g_ref, w_ref, cos_ref, sin_ref,
                           q0, q1, q2, k0, k1, k2, v0, v1, v2, kvo0, kvo1, kvo2,
                           rq_ref, rk_ref, rv_ref, gsw_ref, ga_ref, gr_ref, scr_q, scr_k, scr_v):
    tm = x_ref.shape[0]
    h = (_rms(x_ref[...]) * g_ref[...]).astype(BF16)

    def tile(j):
        return _dot(h, w_ref[:, j * TN:(j + 1) * TN])

    def head(acc, hh, width=HEAD_DIM):
        return acc[:, hh * width:(hh + 1) * width]

    def scatter_residues(acc, scr, dst, dil):
        if dil == 1:
            for hh in range(HEADS):
                dst[hh, 0] = head(acc, hh).astype(BF16)
            return
        for hh in range(HEADS):
            scr[hh] = head(acc, hh)
        for hh in range(HEADS):
            for c in range(dil):
                dst[hh, c] = scr[hh, pl.ds(c, tm // dil, stride=dil), :].astype(BF16)

    def kv_rows(acc, kvo, which):
        keep = kvo.shape[0] // (2 * HEADS)
        for hh in range(HEADS):
            kvo[pl.ds(which * HEADS + hh, keep, stride=2 * HEADS), :] = head(acc, hh)[tm - keep:, :]

    for g, (_, dil) in enumerate(DIL_GROUPS):
        scatter_residues(tile(COL_Q + g) * (HEAD_DIM ** -0.5), scr_q, (q0, q1, q2)[g], dil)
        acc = tile(COL_K + g)
        scatter_residues(acc, scr_k, (k0, k1, k2)[g], dil)
        kv_rows(acc, (kvo0, kvo1, kvo2)[g], 0)
        acc = tile(COL_V + g)
        scatter_residues(acc, scr_v, (v0, v1, v2)[g], dil)
        kv_rows(acc, (kvo0, kvo1, kvo2)[g], 1)

    cos, sin = cos_ref[...], sin_ref[...]
    acc = tile(COL_RQ)
    for hh in range(RET_HEADS):
        rq_ref[hh] = _rope(head(acc, hh), cos, sin)
    acc = tile(COL_RK)
    for hh in range(RET_HEADS):
        rk_ref[hh] = _rope(head(acc, hh), cos, sin) * (RET_DK ** -0.5)
    for e in range(2):
        acc = tile(COL_RV + e)
        for s in range(2):
            rv_ref[2 * e + s] = head(acc, s, RET_DV).astype(BF16)
        acc = tile(COL_GSW + e)
        for s in range(2):
            gsw_ref[2 * e + s] = head(acc, s, RET_DV)
        ga_ref[:, e * TN:(e + 1) * TN] = tile(COL_GA + e)
        gr_ref[:, e * TN:(e + 1) * TN] = tile(COL_GR + e)


def _in_proj_prompt(x2d, g, w_bf, cos_t, sin_t, batch, seq, tm):
    assert seq % tm == 0 and tm % (16 * DIL_GROUPS[-1][1]) == 0
    tps = seq // tm
    out_shape, out_specs = [], []
    for _ in range(3):
        for _, dil in DIL_GROUPS:
            out_shape.append(jax.ShapeDtypeStruct((batch, HEADS, dil, seq // dil, HEAD_DIM), BF16))
            out_specs.append(pl.BlockSpec((None, HEADS, dil, tm // dil, HEAD_DIM), lambda b, t: (b, 0, 0, t, 0)))
    for win, _ in DIL_GROUPS:
        keep = min(win, seq)
        rows = min(keep, tm)
        assert keep % rows == 0
        first = (seq - keep) // rows
        out_shape.append(jax.ShapeDtypeStruct((batch, keep * 2 * HEADS, HEAD_DIM), F32))
        out_specs.append(pl.BlockSpec(
            (None, rows * 2 * HEADS, HEAD_DIM),
            functools.partial(lambda b, t, first, per: (b, jnp.maximum((t + 1) * per - 1 - first, 0), 0),
                              first=first, per=tm // rows)))
    hm = lambda width, dt: (jax.ShapeDtypeStruct((batch, RET_HEADS, seq, width), dt),
                            pl.BlockSpec((None, RET_HEADS, tm, width), lambda b, t: (b, 0, t, 0)))
    nat = (jax.ShapeDtypeStruct((batch * seq, D_MODEL), F32),
           pl.BlockSpec((tm, D_MODEL), lambda b, t: (b * tps + t, 0)))
    for shp, spec in (hm(RET_DK, F32), hm(RET_DK, F32), hm(RET_DV, BF16), hm(RET_DV, F32), nat, nat):
        out_shape.append(shp)
        out_specs.append(spec)
    return pl.pallas_call(
        _in_proj_prompt_kernel,
        grid=(batch, tps),
        in_specs=[
            pl.BlockSpec((tm, D_MODEL), lambda b, t: (b * tps + t, 0)),
            _resident((1, D_MODEL), lambda b, t: (0, 0)),
            _resident((D_MODEL, IN_COLS), lambda b, t: (0, 0)),
            pl.BlockSpec((tm, RET_DK), lambda b, t: (t, 0)),
            pl.BlockSpec((tm, RET_DK), lambda b, t: (t, 0)),
        ],
        out_specs=out_specs,
        out_shape=out_shape,
        scratch_shapes=[pltpu.VMEM((HEADS, tm, HEAD_DIM), F32)] * 3,
        compiler_params=_cparams(2, 56),
        name="in_proj_prompt",
    )(x2d, g, w_bf, cos_t, sin_t)


def _in_proj_sample_kernel(x_ref, g_ref, w_ref, z_ref, h_ref):
    @pl.when(pl.program_id(0) == 0)
    def _():
        h_ref[...] = (_rms(x_ref[...]) * g_ref[...]).astype(BF16)

    z_ref[...] = _dot(h_ref[...], w_ref[...])


def _in_proj_sample(x2d, g, w_bf):
    m = x2d.shape[0]
    return pl.pallas_call(
        _in_proj_sample_kernel,
        grid=(N_COL_TILES,),
        in_specs=[
            _resident((m, D_MODEL), lambda j: (0, 0)),
            _resident((1, D_MODEL), lambda j: (0, 0)),
            pl.BlockSpec((D_MODEL, TN), lambda j: (0, j)),
        ],
        out_specs=pl.BlockSpec((m, TN), lambda j: (0, j)),
        out_shape=jax.ShapeDtypeStruct((m, IN_COLS), F32),
        scratch_shapes=[pltpu.VMEM((m, D_MODEL), BF16)],
        compiler_params=_cparams(1, 16),
        name="in_proj_sample",
    )(x2d, g, w_bf)


def _softmax_block(s, v):
    m = jnp.max(s, axis=1, keepdims=True)
    p = jnp.exp(s - m).astype(BF16)
    uv = _dot(p, jnp.concatenate([v, jnp.ones_like(v)], axis=1))
    return uv[:, :HEAD_DIM], jnp.broadcast_to(m, (s.shape[0], HEAD_DIM)), uv[:, HEAD_DIM:]


ATT_BLOCKS_PER_STEP = 8
ATT_ORDER = 4


def _attn_prompt_kernel(q0, q1, q2, k0, k1, k2, v0, v1, v2, o_ref,
                        u0, u1, u2, m0, m1, m2, l0, l1, l2, *, seq):
    qs, ks, vs = (q0, q1, q2), (k0, k1, k2), (v0, v1, v2)
    us, ms, ls = (u0, u1, u2), (m0, m1, m2), (l0, l1, l2)
    order = ATT_ORDER
    n_out = seq // order
    row1 = lax.broadcasted_iota(jnp.int32, (BAND, BAND), 0)
    col1 = lax.broadcasted_iota(jnp.int32, (BAND, BAND), 1)
    causal = col1 <= row1
    row2 = lax.broadcasted_iota(jnp.int32, (BAND, 2 * BAND), 0)
    col2 = lax.broadcasted_iota(jnp.int32, (BAND, 2 * BAND), 1)
    band = (col2 >= row2) & (col2 <= row2 + BAND)

    aligned = lambda r: r if isinstance(r, int) else pl.multiple_of(r, BAND)

    for g, (_, dil) in enumerate(DIL_GROUPS):
        n = seq // dil
        nb = n // BAND

        def run_blocks(blocks, g=g, n=n, dil=dil):
            scores, values = [], []
            for c, blk, has_prev in blocks:
                own = pl.ds(aligned(blk * BAND), BAND)
                keys = pl.ds(aligned((blk - 1) * BAND), 2 * BAND) if has_prev else own
                mask = band if has_prev else causal
                scores.append(jnp.where(mask, _dot_nt(qs[g][c, own, :], ks[g][c, keys, :]), NEG_INF))
                values.append(vs[g][c, keys, :])
            results = [_softmax_block(s, v) for s, v in zip(scores, values)]
            for (c, blk, _), (u, m, l) in zip(blocks, results):
                if dil <= order:
                    rs = pl.ds(aligned(c * n + blk * BAND), BAND)
                else:
                    step = dil // order
                    rs = pl.ds((c % order) * n_out + c // order + blk * (BAND * step), BAND, stride=step)
                us[g][rs, :] = u
                ms[g][rs, :] = m
                ls[g][rs, :] = l

        per = ATT_BLOCKS_PER_STEP
        assert (dil * nb) % per == 0 and (per % nb == 0 or nb % per == 0)
        if nb >= per:
            def body(i, carry, run_blocks=run_blocks, steps=nb // per):
                c, i0 = i // steps, (i % steps) * per
                run_blocks([(c, i0 + e, True) for e in range(per)])
                return carry
            for c in range(dil):
                run_blocks([(c, e, e > 0) for e in range(per)])
                lax.fori_loop(c * (nb // per) + 1, (c + 1) * (nb // per), body, 0)
        else:
            def body(i, carry, run_blocks=run_blocks, nb=nb, cps=per // nb):
                run_blocks([(i * cps + e // nb, e % nb, e % nb > 0) for e in range(per)])
                return carry
            lax.fori_loop(0, dil * nb // per, body, 0)

    chunks = n_out // BAND

    def combine_one(j):
        c, t = j // chunks, j % chunks
        rows = []
        for _, dil in DIL_GROUPS:
            r = min(dil, order)
            step = order // r
            start = (c % r) * (seq // r) + c // r + t * (BAND * step)
            rows.append(pl.ds(aligned(start), BAND) if step == 1 else pl.ds(start, BAND, stride=step))
        m_g = [ms[g][rows[g], :] for g in range(N_GROUPS)]
        m = jnp.maximum(jnp.maximum(m_g[0], m_g[1]), m_g[2])
        num = jnp.zeros((BAND, HEAD_DIM), F32)
        den = jnp.zeros((BAND, HEAD_DIM), F32)
        for g in range(N_GROUPS):
            w = jnp.exp(m_g[g] - m)
            num = num + w * us[g][rows[g], :]
            den = den + w * ls[g][rows[g], :]
        return num / den

    per = 4
    assert (order * chunks) % per == 0

    def combine(i, carry):
        outs = [combine_one(i * per + e) for e in range(per)]
        for e, o in enumerate(outs):
            o_ref[pl.ds(aligned((i * per + e) * BAND), BAND), :] = o
        return carry

    lax.fori_loop(0, order * chunks // per, combine, 0)


def _attn_prompt(qkv, batch, seq):
    assert seq % (BAND * DIL_GROUPS[-1][1]) == 0
    assert all(ATT_ORDER % d == 0 or d % ATT_ORDER == 0 for _, d in DIL_GROUPS)
    in_specs = [pl.BlockSpec((None, None) + a.shape[2:], lambda b, h: (b, h, 0, 0, 0)) for a in qkv]
    out = pl.pallas_call(
        functools.partial(_attn_prompt_kernel, seq=seq),
        grid=(batch, HEADS),
        in_specs=in_specs,
        out_specs=pl.BlockSpec((None, None, seq, HEAD_DIM), lambda b, h: (b, h, 0, 0)),
        out_shape=jax.ShapeDtypeStruct((batch, HEADS, seq, HEAD_DIM), F32),
        scratch_shapes=[pltpu.VMEM((seq, HEAD_DIM), F32)] * 9,
        compiler_params=_cparams(2, 32),
        name="attend_prompt",
    )(*qkv)
    return out.reshape(batch, HEADS, ATT_ORDER, seq // ATT_ORDER, HEAD_DIM)


def _attn_sample_kernel(z_ref, c0, c1, c2, o_ref):
    cs = (c0, c1, c2)
    bt = z_ref.shape[0]
    half = BAND // 2

    def heads(b1, tile0, g):
        base = (tile0 + g) * TN
        return jnp.concatenate(
            [z_ref[b1, base + hh * HEAD_DIM:base + (hh + 1) * HEAD_DIM] for hh in range(HEADS)], axis=0)

    def paired(ref, bi, lo):
        return jnp.concatenate([ref[bi, 0:half, lo:lo + HEADS, :], ref[bi, half:BAND, lo:lo + HEADS, :]], axis=1)

    twice = lambda a: jnp.concatenate([a, a], axis=0)
    fold = lambda a, op: op(a[0:HEADS], a[HEADS:2 * HEADS])

    for bi in range(bt):
        b1 = pl.ds(bi, 1)
        us, ms, ls = [], [], []
        for g in range(N_GROUPS):
            q = heads(b1, COL_Q, g) * (HEAD_DIM ** -0.5)
            k_new, v_new = heads(b1, COL_K, g), heads(b1, COL_V, g)
            k_buf, v_buf = paired(cs[g], bi, 0), paired(cs[g], bi, HEADS)
            s_buf = jnp.sum(k_buf * twice(q)[None], axis=-1, keepdims=True)
            s_new = jnp.sum(k_new * q, axis=-1, keepdims=True)
            m = jnp.maximum(fold(jnp.max(s_buf, axis=0), jnp.maximum), s_new)
            p_buf = jnp.exp(s_buf - twice(m)[None])
            p_new = jnp.exp(s_new - m)
            ls.append(fold(jnp.sum(p_buf, axis=0), jnp.add) + p_new)
            us.append(fold(jnp.sum(p_buf * v_buf, axis=0), jnp.add) + p_new * v_new)
            ms.append(m)
        m = jnp.maximum(jnp.maximum(ms[0], ms[1]), ms[2])
        num = jnp.zeros((HEADS, HEAD_DIM), F32)
        den = jnp.zeros((HEADS, 1), F32)
        for g in range(N_GROUPS):
            w = jnp.exp(ms[g] - m)
            num = num + w * us[g]
            den = den + w * ls[g]
        o_ref[bi] = num / den


def _attn_sample(z, caches, layer, bt):
    db = z.shape[0]
    assert db % bt == 0
    nbt = db // bt
    views, specs = [], []
    for (win, dil), c in zip(DIL_GROUPS, caches):
        assert c.shape[2] == win and win // dil == BAND
        views.append(c.reshape(c.shape[0] * db, BAND, dil, 2 * HEADS, HEAD_DIM))
        specs.append(pl.BlockSpec((bt, BAND, None, 2 * HEADS, HEAD_DIM),
                                  lambda i: (layer * nbt + i, 0, 0, 0, 0)))
    out = pl.pallas_call(
        _attn_sample_kernel,
        grid=(nbt,),
        in_specs=[pl.BlockSpec((bt, IN_COLS), lambda i: (i, 0))] + specs,
        out_specs=pl.BlockSpec((bt, HEADS, HEAD_DIM), lambda i: (i, 0, 0)),
        out_shape=jax.ShapeDtypeStruct((db, HEADS, HEAD_DIM), F32),
        compiler_params=_cparams(1, 40),
        name="attend_sample",
    )(z, *views)
    return jnp.transpose(out, (1, 0, 2))


def _group_norm_gate(o, gn, gate_logit):
    mu = jnp.mean(o, axis=-1, keepdims=True)
    d = o - mu
    var = jnp.mean(d * d, axis=-1, keepdims=True)
    return (gate_logit * _sigmoid(gate_logit)) * (d * lax.rsqrt(var + EPS) * gn)


def _ret_prompt_kernel(lg_ref, q_ref, k_ref, v_ref, gate_ref, gn_ref, o_ref, s_out_ref, s_scr):
    c = RET_CHUNK
    rows = q_ref.shape[1]
    t = lax.broadcasted_iota(jnp.int32, (c, 1), 0).astype(F32)
    rel = (lax.broadcasted_iota(jnp.int32, (c, c), 0) - lax.broadcasted_iota(jnp.int32, (c, c), 1)).astype(F32)

    @pl.when(pl.program_id(1) == 0)
    def _():
        s_scr[...] = jnp.zeros_like(s_scr)

    consts = []
    for h in range(RET_HEADS):
        lg = lg_ref[h]
        consts.append((
            jnp.where(rel >= 0, jnp.exp(lg * jnp.maximum(rel, 0.0)), 0.0),
            jnp.exp(lg * (t + 1.0)),
            jnp.exp(lg * (c - 1.0 - t)),
            jnp.exp(lg * jnp.full((1, RET_DV), float(c), F32)),
        ))

    def chunk(ci, carry):
        rs = pl.ds(pl.multiple_of(ci * c, c), c)
        for h in range(RET_HEADS):
            decay, inner, tail, chunk_decay = consts[h]
            q, k, vb = q_ref[h, rs, :], k_ref[h, rs, :], v_ref[h, rs, :]
            s = s_scr[h]
            a = _dot_nt(q.astype(BF16), k.astype(BF16)) * decay
            o = _dot(a.astype(BF16), vb) + _dot((q * inner).astype(BF16), s.astype(BF16))
            s_scr[h] = chunk_decay * s + _dot((k * tail).T.astype(BF16), vb)
            gn = gn_ref[:, h * RET_DV:(h + 1) * RET_DV]
            o_ref[h, rs, :] = _group_norm_gate(o, gn, gate_ref[h, rs, :]).astype(o_ref.dtype)
        return carry

    lax.fori_loop(0, rows // c, chunk, 0)

    @pl.when(pl.program_id(1) == pl.num_programs(1) - 1)
    def _():
        s_out_ref[...] = s_scr[...]


def _ret_prompt(log_g, rq, rk, rv, gsw, gn, batch, seq, rows):
    assert seq % rows == 0 and rows % RET_CHUNK == 0
    hm = lambda width: pl.BlockSpec((None, RET_HEADS, rows, width), lambda b, t: (b, 0, t, 0))
    return pl.pallas_call(
        _ret_prompt_kernel,
        grid=(batch, seq // rows),
        in_specs=[
            pl.BlockSpec(memory_space=pltpu.SMEM),
            hm(RET_DK), hm(RET_DK), hm(RET_DV), hm(RET_DV),
            _resident((1, RET_V), lambda b, t: (0, 0)),
        ],
        out_specs=[
            hm(RET_DV),
            pl.BlockSpec((None, RET_HEADS, RET_DK, RET_DV), lambda b, t: (b, 0, 0, 0)),
        ],
        out_shape=[
            jax.ShapeDtypeStruct((batch, RET_HEADS, seq, RET_DV), BF16),
            jax.ShapeDtypeStruct((batch, RET_HEADS, RET_DK, RET_DV), F32),
        ],
        scratch_shapes=[pltpu.VMEM((RET_HEADS, RET_DK, RET_DV), F32)],
        compiler_params=_cparams(2, 40),
        name="retain_prompt",
    )(log_g, rq, rk, rv, gsw, gn)


def _ret_sample_kernel(lg_ref, z_ref, cos_ref, sin_ref, gn_ref, s_ref, o_ref, s_out_ref, o_scr):
    bt = z_ref.shape[0]
    eye = (lax.broadcasted_iota(jnp.int32, (RET_DK, RET_DK), 0)
           == lax.broadcasted_iota(jnp.int32, (RET_DK, RET_DK), 1))
    cos, sin = cos_ref[...], sin_ref[...]

    def column(r):
        return jnp.sum(jnp.where(eye, jnp.broadcast_to(r, (RET_DK, RET_DK)), 0.0), axis=1, keepdims=True)

    def zcols(tile0, off, width):
        return z_ref[:, tile0 * TN + off:tile0 * TN + off + width]

    for h in range(RET_HEADS):
        gamma = jnp.exp(lg_ref[h] * jnp.ones((1, RET_DV), F32))
        q_h = _rope(zcols(COL_RQ, h * RET_DK, RET_DK), cos, sin)
        k_h = _rope(zcols(COL_RK, h * RET_DK, RET_DK), cos, sin) * (RET_DK ** -0.5)
        v_h = zcols(COL_RV, h * RET_DV, RET_DV)
        for bi in range(bt):
            b1 = pl.ds(bi, 1)
            qc = column(q_h[bi:bi + 1, :])
            kc = column(k_h[bi:bi + 1, :])
            vrow = v_h[bi:bi + 1, :]
            s_new = gamma * s_ref[bi, h] + kc * vrow
            s_out_ref[bi, h] = s_new
            o_scr[b1, h * RET_DV:(h + 1) * RET_DV] = jnp.sum(qc * s_new, axis=0, keepdims=True)
    for h in range(RET_HEADS):
        seg = slice(h * RET_DV, (h + 1) * RET_DV)
        gate = z_ref[:, COL_GSW * TN + h * RET_DV:COL_GSW * TN + (h + 1) * RET_DV]
        o_ref[h] = _group_norm_gate(o_scr[:, seg], gn_ref[:, seg], gate)


def _ret_sample(log_g, z, cos_t, sin_t, gn, state, layer, bt):
    db = z.shape[0]
    assert db % bt == 0
    nbt = db // bt
    st = state.reshape((state.shape[0] * db,) + state.shape[2:])
    st_block = (bt, RET_HEADS, RET_DK, RET_DV)
    return pl.pallas_call(
        _ret_sample_kernel,
        grid=(nbt,),
        in_specs=[
            pl.BlockSpec(memory_space=pltpu.SMEM),
            pl.BlockSpec((bt, IN_COLS), lambda i: (i, 0)),
            _resident((1, RET_DK), lambda i: (0, 0)),
            _resident((1, RET_DK), lambda i: (0, 0)),
            _resident((1, RET_V), lambda i: (0, 0)),
            pl.BlockSpec(st_block, lambda i: (layer * nbt + i, 0, 0, 0)),
        ],
        out_specs=[
            pl.BlockSpec((RET_HEADS, bt, RET_DV), lambda i: (0, i, 0)),
            pl.BlockSpec(st_block, lambda i: (i, 0, 0, 0)),
        ],
        out_shape=[
            jax.ShapeDtypeStruct((RET_HEADS, db, RET_DV), F32),
            jax.ShapeDtypeStruct((db, RET_HEADS, RET_DK, RET_DV), F32),
        ],
        scratch_shapes=[pltpu.VMEM((bt, RET_V), F32)],
        compiler_params=_cparams(1, 32),
        name="retain_sample",
    )(log_g, z, cos_t, sin_t, gn, st)


def _merge_kernel(att_ref, ret_ref, ga_ref, gr_ref, x_ref, wa_ref, wr_ref, wo_ref, gpost_ref, gpre_ref,
                  x1_ref, h2_ref):
    cpt, ni = att_ref.shape[1], att_ref.shape[2]

    def classes(ref, width):
        return jnp.concatenate([ref[:, c * width:(c + 1) * width] for c in range(cpt)], axis=0)

    att = jnp.concatenate(
        [jnp.concatenate([att_ref[hh, c].astype(BF16) for hh in range(HEADS)], axis=1) for c in range(cpt)], axis=0)
    ret = jnp.concatenate(
        [jnp.concatenate([ret_ref[hh, :, c * RET_DV:(c + 1) * RET_DV].astype(BF16) for hh in range(RET_HEADS)],
                         axis=1) for c in range(cpt)], axis=0)
    ga, gr, x = classes(ga_ref, D_MODEL), classes(gr_ref, D_MODEL), classes(x_ref, D_MODEL)
    m = _sigmoid(ga) * _dot(att, wa_ref[...]) + _sigmoid(gr) * _dot(ret, wr_ref[...])
    y = _dot(m.astype(BF16), wo_ref[...])
    x1 = x + _rms(y) * gpost_ref[...]
    h2 = (_rms(x1) * gpre_ref[...]).astype(BF16)
    for c in range(cpt):
        x1_ref[:, c * D_MODEL:(c + 1) * D_MODEL] = x1[c * ni:(c + 1) * ni]
        h2_ref[:, c * D_MODEL:(c + 1) * D_MODEL] = h2[c * ni:(c + 1) * ni]


def _merge(att, ret, ga, gr, x2d, wa, wr, wo, gpost, gpre, cpt):
    batch, _, r, ni, _ = att.shape
    assert r % cpt == 0 and x2d.shape[0] == batch * r * ni
    rows_view = lambda a: a.reshape(batch, ni, r * D_MODEL)
    rows = pl.BlockSpec((None, ni, cpt * D_MODEL), lambda b, j: (b, 0, j))
    full = lambda a: _resident(a.shape, lambda b, j: (0, 0))
    x1, h2 = pl.pallas_call(
        _merge_kernel,
        grid=(batch, r // cpt),
        in_specs=[pl.BlockSpec((None, HEADS, cpt, ni, HEAD_DIM), lambda b, j: (b, 0, j, 0, 0)),
                  pl.BlockSpec((None, RET_HEADS, ni, cpt * RET_DV), lambda b, j: (b, 0, 0, j)),
                  rows, rows, rows, full(wa), full(wr), full(wo), full(gpost), full(gpre)],
        out_specs=[rows, rows],
        out_shape=[jax.ShapeDtypeStruct((batch, ni, r * D_MODEL), F32),
                   jax.ShapeDtypeStruct((batch, ni, r * D_MODEL), BF16)],
        compiler_params=_cparams(2, 40),
        name="merge",
    )(att, ret.reshape(batch, RET_HEADS, ni, r * RET_DV), rows_view(ga), rows_view(gr), rows_view(x2d),
      wa, wr, wo, gpost, gpre)
    return x1.reshape(batch * r * ni, D_MODEL), h2.reshape(batch * r * ni, D_MODEL)


FF_CHUNK = 1024


def _mlp_kernel(h_ref, x_ref, wu_ref, wd_ref, gpost_ref, y_ref):
    h = h_ref[...]
    acc = jnp.zeros(x_ref.shape, F32)
    for c in range(D_FF // FF_CHUNK):
        cs = slice(c * FF_CHUNK, (c + 1) * FF_CHUNK)
        u = jnp.maximum(_dot(h, wu_ref[:, cs]), 0.0)
        acc = acc + _dot((u * u).astype(BF16), wd_ref[cs, :])
    y_ref[...] = x_ref[...] + _rms(acc) * gpost_ref[...]


def _mlp(h2, x1, wu, wd, gpost, tm):
    m = x1.shape[0]
    assert m % tm == 0
    rows = pl.BlockSpec((tm, D_MODEL), lambda i: (i, 0))
    full = lambda a: _resident(a.shape, lambda i: (0, 0))
    return pl.pallas_call(
        _mlp_kernel,
        grid=(m // tm,),
        in_specs=[rows, rows, full(wu), full(wd), full(gpost)],
        out_specs=rows,
        out_shape=jax.ShapeDtypeStruct((m, D_MODEL), F32),
        compiler_params=_cparams(1, 40),
        name="mlp",
    )(h2, x1, wu, wd, gpost)


def _rope_tables(pos):
    half = RET_DK // 2
    inv = ROPE_BASE ** (-jnp.arange(half, dtype=F32) / half)
    ang = pos.astype(F32)[:, None] * inv[None, :]
    cos, sin = jnp.cos(ang), jnp.sin(ang)
    return jnp.concatenate([cos, cos], axis=-1), jnp.concatenate([-sin, sin], axis=-1)


IN_PROJ_TM = 256
MERGE_CLASSES = 1
TAIL_TM = 512
RET_ROWS = 1024
SAMPLE_BT = 8


def _layer_tail(att, ret, ga, gr, x2d, lw, cpt, tm):
    x1, h2 = _merge(att, ret, ga, gr, x2d, lw["w_att_br"], lw["w_ret_br"], lw["w_out"],
                    lw["g_post_mix"], lw["g_pre_mlp"], cpt)
    return _mlp(h2, x1, lw["w_up"], lw["w_down"], lw["g_post_mlp"], tm)


def _layer_prompt(x, lw, log_g):
    batch, seq, _ = x.shape
    x2d = x.reshape(batch * seq, D_MODEL)
    cos_t, sin_t = _rope_tables(jnp.arange(seq, dtype=jnp.int32))
    outs = _in_proj_prompt(x2d, lw["g_pre_mix"], lw["w_in"], cos_t, sin_t, batch, seq, IN_PROJ_TM)
    qkv = outs[:9]
    kv_rows = [kv.reshape(batch, -1, 2, HEADS, HEAD_DIM) for kv in outs[9:12]]
    rq, rk, rv, gsw, ga, gr = outs[12:]
    att = _attn_prompt(qkv, batch, seq)
    ret, state = _ret_prompt(log_g, rq, rk, rv, gsw, lw["g_ret_norm"], batch, seq, RET_ROWS)
    y = _layer_tail(att, ret, ga, gr, x2d, lw, MERGE_CLASSES, TAIL_TM)
    return y.reshape(batch, seq, D_MODEL), kv_rows, state


def _layer_sample(x, caches, state, layer, lw, log_g):
    db, t, _ = x.shape
    assert t == 1
    x2d = x.reshape(db, D_MODEL)
    cos_t, sin_t = _rope_tables(PAST_LEN + jnp.arange(1, dtype=jnp.int32))
    z = _in_proj_sample(x2d, lw["g_pre_mix"], lw["w_in"])
    att = _attn_sample(z, caches, layer, SAMPLE_BT)
    ret, state_new = _ret_sample(log_g, z, cos_t, sin_t, lw["g_ret_norm"], state, layer, SAMPLE_BT)
    ga = z[:, COL_GA * TN:COL_GA * TN + D_MODEL]
    gr = z[:, COL_GR * TN:COL_GR * TN + D_MODEL]
    y = _layer_tail(att.reshape(1, HEADS, 1, db, HEAD_DIM), ret[None], ga, gr, x2d, lw, 1, db)
    rows = []
    for g in range(N_GROUPS):
        k_new = z[:, (COL_K + g) * TN:(COL_K + g + 1) * TN]
        v_new = z[:, (COL_V + g) * TN:(COL_V + g + 1) * TN]
        rows.append(jnp.stack([k_new, v_new], axis=1).reshape(db, 1, 2, HEADS, HEAD_DIM))
    return y.reshape(db, 1, D_MODEL), rows, state_new


def _stack(xs):
    return xs[0][None] if len(xs) == 1 else jnp.stack(xs, axis=0)


def kernel(x_prompt, x_sample, cache_kv_d1, cache_kv_d4, cache_kv_d16, state_ret, w_in, w_att_br, w_ret_br, w_out, w_up, w_down, g_ret_norm, g_pre_mix, g_post_mix, g_pre_mlp, g_post_mlp):
    depth = w_in.shape[0]
    log_g = jnp.log1p(-jnp.power(2.0, -5.0 - jnp.arange(RET_HEADS, dtype=F32)))
    caches = (cache_kv_d1, cache_kv_d4, cache_kv_d16)
    xp, xs = x_prompt, x_sample
    p_rows, s_rows = [[], [], []], [[], [], []]
    p_states, s_states = [], []
    for l in range(depth):
        lw = {
            "w_in": w_in[l].astype(BF16), "w_att_br": w_att_br[l].astype(BF16),
            "w_ret_br": w_ret_br[l].astype(BF16), "w_out": w_out[l].astype(BF16),
            "w_up": w_up[l].astype(BF16), "w_down": w_down[l].astype(BF16),
            "g_ret_norm": g_ret_norm[l].reshape(1, RET_V), "g_pre_mix": g_pre_mix[l].reshape(1, D_MODEL),
            "g_post_mix": g_post_mix[l].reshape(1, D_MODEL), "g_pre_mlp": g_pre_mlp[l].reshape(1, D_MODEL),
            "g_post_mlp": g_post_mlp[l].reshape(1, D_MODEL),
        }
        xp, rows_p, sp = _layer_prompt(xp, lw, log_g)
        xs, rows_s, ss = _layer_sample(xs, caches, state_ret, l, lw, log_g)
        for g in range(N_GROUPS):
            p_rows[g].append(rows_p[g])
            s_rows[g].append(rows_s[g])
        p_states.append(sp)
        s_states.append(ss)
    return (xp, xs, _stack(p_rows[0]), _stack(p_rows[1]), _stack(p_rows[2]), _stack(p_states),
            _stack(s_rows[0]), _stack(s_rows[1]), _stack(s_rows[2]), _stack(s_states))
```

```python
import functools

import jax
import jax.numpy as jnp
from jax import lax
from jax.experimental import pallas as pl
from jax.experimental.pallas import tpu as pltpu

F32 = jnp.float32
BF16 = jnp.bfloat16

D_MODEL = 1024
PAST_LEN = 8192

DIL_GROUPS = ((128, 1), (512, 4), (2048, 16))
N_GROUPS = 3
HEADS = 4
HEAD_DIM = 128
ATT_OUT = HEADS * HEAD_DIM
ATT_COLS = N_GROUPS * ATT_OUT
BAND = 128

RET_HEADS = 4
RET_DK = 128
RET_DV = 256
RET_QK = RET_HEADS * RET_DK
RET_V = RET_HEADS * RET_DV
RET_CHUNK = 128
ROPE_BASE = 10000.0

D_FF = 4 * D_MODEL
IN_COLS = 3 * ATT_COLS + 2 * RET_QK + 2 * RET_V + 2 * D_MODEL
EPS = 1e-6
NEG_INF = -1e30

TN = 512
N_COL_TILES = IN_COLS // TN
COL_Q, COL_K, COL_V, COL_RQ, COL_RK, COL_RV, COL_GSW, COL_GA, COL_GR = 0, 3, 6, 9, 10, 11, 13, 15, 17

V7X_VMEM_BYTES = 64 * 1024 * 1024
MIB = 1024 * 1024


def _cparams(n_axes, vmem_mib):
    assert vmem_mib * MIB < V7X_VMEM_BYTES
    return pltpu.CompilerParams(
        dimension_semantics=("arbitrary",) * n_axes,
        vmem_limit_bytes=vmem_mib * MIB,
    )


def _resident(shape, index_map):
    return pl.BlockSpec(shape, index_map, pipeline_mode=pl.Buffered(1))


def _rms(x):
    return x * lax.rsqrt(jnp.mean(x * x, axis=-1, keepdims=True) + EPS)


def _sigmoid(x):
    return 1.0 / (1.0 + jnp.exp(-x))


def _dot(a, b):
    return jnp.dot(a, b, preferred_element_type=F32)


def _dot_nt(a, b):
    return lax.dot_general(a, b, (((1,), (1,)), ((), ())), preferred_element_type=F32)


def _rope(x, cos, sin):
    return x * cos + pltpu.roll(x, RET_DK // 2, 1) * sin


def _in_proj_prompt_kernel(x_ref, g_ref, w_ref, cos_ref, sin_ref,
                           q0, q1, q2, k0, k1, k2, v0, v1, v2, kvo0, kvo1, kvo2,
                           rq_ref, rk_ref, rv_ref, gsw_ref, ga_ref, gr_ref, scr_q, scr_k, scr_v):
    tm = x_ref.shape[0]
    h = (_rms(x_ref[...]) * g_ref[...]).astype(BF16)

    def tile(j):
        return _dot(h, w_ref[:, j * TN:(j + 1) * TN])

    def head(acc, hh, width=HEAD_DIM):
        return acc[:, hh * width:(hh + 1) * width]

    def scatter_residues(acc, scr, dst, dil):
        if dil == 1:
            for hh in range(HEADS):
                dst[hh, 0] = head(acc, hh).astype(BF16)
            return
        for hh in range(HEADS):
            scr[hh] = head(acc, hh)
        for hh in range(HEADS):
            for c in range(dil):
                dst[hh, c] = scr[hh, pl.ds(c, tm // dil, stride=dil), :].astype(BF16)

    def kv_rows(acc, kvo, which):
        keep = kvo.shape[0] // (2 * HEADS)
        for hh in range(HEADS):
            kvo[pl.ds(which * HEADS + hh, keep, stride=2 * HEADS), :] = head(acc, hh)[tm - keep:, :]

    for g, (_, dil) in enumerate(DIL_GROUPS):
        scatter_residues(tile(COL_Q + g) * (HEAD_DIM ** -0.5), scr_q, (q0, q1, q2)[g], dil)
        acc = tile(COL_K + g)
        scatter_residues(acc, scr_k, (k0, k1, k2)[g], dil)
        kv_rows(acc, (kvo0, kvo1, kvo2)[g], 0)
        acc = tile(COL_V + g)
        scatter_residues(acc, scr_v, (v0, v1, v2)[g], dil)
        kv_rows(acc, (kvo0, kvo1, kvo2)[g], 1)

    cos, sin = cos_ref[...], sin_ref[...]
    acc = tile(COL_RQ)
    for hh in range(RET_HEADS):
        rq_ref[hh] = _rope(head(acc, hh), cos, sin)
    acc = tile(COL_RK)
    for hh in range(RET_HEADS):
        rk_ref[hh] = _rope(head(acc, hh), cos, sin) * (RET_DK ** -0.5)
    for e in range(2):
        acc = tile(COL_RV + e)
        for s in range(2):
            rv_ref[2 * e + s] = head(acc, s, RET_DV).astype(BF16)
        acc = tile(COL_GSW + e)
        for s in range(2):
            gsw_ref[2 * e + s] = head(acc, s, RET_DV)
        ga_ref[:, e * TN:(e + 1) * TN] = tile(COL_GA + e)
        gr_ref[:, e * TN:(e + 1) * TN] = tile(COL_GR + e)


def _in_proj_prompt(x2d, g, w_bf, cos_t, sin_t, batch, seq, tm):
    assert seq % tm == 0 and tm % (16 * DIL_GROUPS[-1][1]) == 0
    tps = seq // tm
    out_shape, out_specs = [], []
    for _ in range(3):
        for _, dil in DIL_GROUPS:
            out_shape.append(jax.ShapeDtypeStruct((batch, HEADS, dil, seq // dil, HEAD_DIM), BF16))
            out_specs.append(pl.BlockSpec((None, HEADS, dil, tm // dil, HEAD_DIM), lambda b, t: (b, 0, 0, t, 0)))
    for win, _ in DIL_GROUPS:
        keep = min(win, seq)
        rows = min(keep, tm)
        assert keep % rows == 0
        first = (seq - keep) // rows
        out_shape.append(jax.ShapeDtypeStruct((batch, keep * 2 * HEADS, HEAD_DIM), F32))
        out_specs.append(pl.BlockSpec(
            (None, rows * 2 * HEADS, HEAD_DIM),
            functools.partial(lambda b, t, first, per: (b, jnp.maximum((t + 1) * per - 1 - first, 0), 0),
                              first=first, per=tm // rows)))
    hm = lambda width, dt: (jax.ShapeDtypeStruct((batch, RET_HEADS, seq, width), dt),
                            pl.BlockSpec((None, RET_HEADS, tm, width), lambda b, t: (b, 0, t, 0)))
    nat = (jax.ShapeDtypeStruct((batch * seq, D_MODEL), F32),
           pl.BlockSpec((tm, D_MODEL), lambda b, t: (b * tps + t, 0)))
    for shp, spec in (hm(RET_DK, F32), hm(RET_DK, F32), hm(RET_DV, BF16), hm(RET_DV, F32), nat, nat):
        out_shape.append(shp)
        out_specs.append(spec)
    return pl.pallas_call(
        _in_proj_prompt_kernel,
        grid=(batch, tps),
        in_specs=[
            pl.BlockSpec((tm, D_MODEL), lambda b, t: (b * tps + t, 0)),
            _resident((1, D_MODEL), lambda b, t: (0, 0)),
            _resident((D_MODEL, IN_COLS), lambda b, t: (0, 0)),
            pl.BlockSpec((tm, RET_DK), lambda b, t: (t, 0)),
            pl.BlockSpec((tm, RET_DK), lambda b, t: (t, 0)),
        ],
        out_specs=out_specs,
        out_shape=out_shape,
        scratch_shapes=[pltpu.VMEM((HEADS, tm, HEAD_DIM), F32)] * 3,
        compiler_params=_cparams(2, 56),
        name="in_proj_prompt",
    )(x2d, g, w_bf, cos_t, sin_t)


def _in_proj_sample_kernel(x_ref, g_ref, w_ref, z_ref, h_ref):
    @pl.when(pl.program_id(0) == 0)
    def _():
        h_ref[...] = (_rms(x_ref[...]) * g_ref[...]).astype(BF16)

    z_ref[...] = _dot(h_ref[...], w_ref[...])


def _in_proj_sample(x2d, g, w_bf):
    m = x2d.shape[0]
    return pl.pallas_call(
        _in_proj_sample_kernel,
        grid=(N_COL_TILES,),
        in_specs=[
            _resident((m, D_MODEL), lambda j: (0, 0)),
            _resident((1, D_MODEL), lambda j: (0, 0)),
            pl.BlockSpec((D_MODEL, TN), lambda j: (0, j)),
        ],
        out_specs=pl.BlockSpec((m, TN), lambda j: (0, j)),
        out_shape=jax.ShapeDtypeStruct((m, IN_COLS), F32),
        scratch_shapes=[pltpu.VMEM((m, D_MODEL), BF16)],
        compiler_params=_cparams(1, 16),
        name="in_proj_sample",
    )(x2d, g, w_bf)


def _softmax_block(s, v):
    m = jnp.max(s, axis=1, keepdims=True)
    p = jnp.exp(s - m).astype(BF16)
    uv = _dot(p, jnp.concatenate([v, jnp.ones_like(v)], axis=1))
    return uv[:, :HEAD_DIM], jnp.broadcast_to(m, (s.shape[0], HEAD_DIM)), uv[:, HEAD_DIM:]


ATT_BLOCKS_PER_STEP = 8
ATT_ORDER = 4


def _attn_prompt_kernel(q0, q1, q2, k0, k1, k2, v0, v1, v2, o_ref,
                        u0, u1, u2, m0, m1, m2, l0, l1, l2, *, seq):
    qs, ks, vs = (q0, q1, q2), (k0, k1, k2), (v0, v1, v2)
    us, ms, ls = (u0, u1, u2), (m0, m1, m2), (l0, l1, l2)
    order = ATT_ORDER
    n_out = seq // order
    row1 = lax.broadcasted_iota(jnp.int32, (BAND, BAND), 0)
    col1 = lax.broadcasted_iota(jnp.int32, (BAND, BAND), 1)
    causal = col1 <= row1
    row2 = lax.broadcasted_iota(jnp.int32, (BAND, 2 * BAND), 0)
    col2 = lax.broadcasted_iota(jnp.int32, (BAND, 2 * BAND), 1)
    band = (col2 >= row2) & (col2 <= row2 + BAND)

    aligned = lambda r: r if isinstance(r, int) else pl.multiple_of(r, BAND)

    for g, (_, dil) in enumerate(DIL_GROUPS):
        n = seq // dil
        nb = n // BAND

        def run_blocks(blocks, g=g, n=n, dil=dil):
            scores, values = [], []
            for c, blk, has_prev in blocks:
                own = pl.ds(aligned(blk * BAND), BAND)
                keys = pl.ds(aligned((blk - 1) * BAND), 2 * BAND) if has_prev else own
                mask = band if has_prev else causal
                scores.append(jnp.where(mask, _dot_nt(qs[g][c, own, :], ks[g][c, keys, :]), NEG_INF))
                values.append(vs[g][c, keys, :])
            results = [_softmax_block(s, v) for s, v in zip(scores, values)]
            for (c, blk, _), (u, m, l) in zip(blocks, results):
                if dil <= order:
                    rs = pl.ds(aligned(c * n + blk * BAND), BAND)
                else:
                    step = dil // order
                    rs = pl.ds((c % order) * n_out + c // order + blk * (BAND * step), BAND, stride=step)
                us[g][rs, :] = u
                ms[g][rs, :] = m
                ls[g][rs, :] = l

        per = ATT_BLOCKS_PER_STEP
        assert (dil * nb) % per == 0 and (per % nb == 0 or nb % per == 0)
        if nb >= per:
            def body(i, carry, run_blocks=run_blocks, steps=nb // per):
                c, i0 = i // steps, (i % steps) * per
                run_blocks([(c, i0 + e, True) for e in range(per)])
                return carry
            for c in range(dil):
                run_blocks([(c, e, e > 0) for e in range(per)])
                lax.fori_loop(c * (nb // per) + 1, (c + 1) * (nb // per), body, 0)
        else:
            def body(i, carry, run_blocks=run_blocks, nb=nb, cps=per // nb):
                run_blocks([(i * cps + e // nb, e % nb, e % nb > 0) for e in range(per)])
                return carry
            lax.fori_loop(0, dil * nb // per, body, 0)

    chunks = n_out // BAND

    def combine_one(j):
        c, t = j // chunks, j % chunks
        rows = []
        for _, dil in DIL_GROUPS:
            r = min(dil, order)
            step = order // r
            start = (c % r) * (seq // r) + c // r + t * (BAND * step)
            rows.append(pl.ds(aligned(start), BAND) if step == 1 else pl.ds(start, BAND, stride=step))
        m_g = [ms[g][rows[g], :] for g in range(N_GROUPS)]
        m = jnp.maximum(jnp.maximum(m_g[0], m_g[1]), m_g[2])
        num = jnp.zeros((BAND, HEAD_DIM), F32)
        den = jnp.zeros((BAND, HEAD_DIM), F32)
        for g in range(N_GROUPS):
            w = jnp.exp(m_g[g] - m)
            num = num + w * us[g][rows[g], :]
            den = den + w * ls[g][rows[g], :]
        return num / den

    per = 4
    assert (order * chunks) % per == 0

    def combine(i, carry):
        outs = [combine_one(i * per + e) for e in range(per)]
        for e, o in enumerate(outs):
            o_ref[pl.ds(aligned((i * per + e) * BAND), BAND), :] = o
        return carry

    lax.fori_loop(0, order * chunks // per, combine, 0)


def _attn_prompt(qkv, batch, seq):
    assert seq % (BAND * DIL_GROUPS[-1][1]) == 0
    assert all(ATT_ORDER % d == 0 or d % ATT_ORDER == 0 for _, d in DIL_GROUPS)
    in_specs = [pl.BlockSpec((None, None) + a.shape[2:], lambda b, h: (b, h, 0, 0, 0)) for a in qkv]
    out = pl.pallas_call(
        functools.partial(_attn_prompt_kernel, seq=seq),
        grid=(batch, HEADS),
        in_specs=in_specs,
        out_specs=pl.BlockSpec((None, None, seq, HEAD_DIM), lambda b, h: (b, h, 0, 0)),
        out_shape=jax.ShapeDtypeStruct((batch, HEADS, seq, HEAD_DIM), F32),
        scratch_shapes=[pltpu.VMEM((seq, HEAD_DIM), F32)] * 9,
        compiler_params=_cparams(2, 32),
        name="attend_prompt",
    )(*qkv)
    return out.reshape(batch, HEADS, ATT_ORDER, seq // ATT_ORDER, HEAD_DIM)


def _attn_sample_kernel(z_ref, c0, c1, c2, o_ref):
    cs = (c0, c1, c2)
    bt = z_ref.shape[0]
    half = BAND // 2

    def heads(b1, tile0, g):
        base = (tile0 + g) * TN
        return jnp.concatenate(
            [z_ref[b1, base + hh * HEAD_DIM:base + (hh + 1) * HEAD_DIM] for hh in range(HEADS)], axis=0)

    def paired(ref, bi, lo):
        return jnp.concatenate([ref[bi, 0:half, lo:lo + HEADS, :], ref[bi, half:BAND, lo:lo + HEADS, :]], axis=1)

    twice = lambda a: jnp.concatenate([a, a], axis=0)
    fold = lambda a, op: op(a[0:HEADS], a[HEADS:2 * HEADS])

    for bi in range(bt):
        b1 = pl.ds(bi, 1)
        us, ms, ls = [], [], []
        for g in range(N_GROUPS):
            q = heads(b1, COL_Q, g) * (HEAD_DIM ** -0.5)
            k_new, v_new = heads(b1, COL_K, g), heads(b1, COL_V, g)
            k_buf, v_buf = paired(cs[g], bi, 0), paired(cs[g], bi, HEADS)
            s_buf = jnp.sum(k_buf * twice(q)[None], axis=-1, keepdims=True)
            s_new = jnp.sum(k_new * q, axis=-1, keepdims=True)
            m = jnp.maximum(fold(jnp.max(s_buf, axis=0), jnp.maximum), s_new)
            p_buf = jnp.exp(s_buf - twice(m)[None])
            p_new = jnp.exp(s_new - m)
            ls.append(fold(jnp.sum(p_buf, axis=0), jnp.add) + p_new)
            us.append(fold(jnp.sum(p_buf * v_buf, axis=0), jnp.add) + p_new * v_new)
            ms.append(m)
        m = jnp.maximum(jnp.maximum(ms[0], ms[1]), ms[2])
        num = jnp.zeros((HEADS, HEAD_DIM), F32)
        den = jnp.zeros((HEADS, 1), F32)
        for g in range(N_GROUPS):
            w = jnp.exp(ms[g] - m)
            num = num + w * us[g]
            den = den + w * ls[g]
        o_ref[bi] = num / den


def _attn_sample(z, caches, layer, bt):
    db = z.shape[0]
    assert db % bt == 0
    nbt = db // bt
    views, specs = [], []
    for (win, dil), c in zip(DIL_GROUPS, caches):
        assert c.shape[2] == win and win // dil == BAND
        views.append(c.reshape(c.shape[0] * db, BAND, dil, 2 * HEADS, HEAD_DIM))
        specs.append(pl.BlockSpec((bt, BAND, None, 2 * HEADS, HEAD_DIM),
                                  lambda i: (layer * nbt + i, 0, 0, 0, 0)))
    out = pl.pallas_call(
        _attn_sample_kernel,
        grid=(nbt,),
        in_specs=[pl.BlockSpec((bt, IN_COLS), lambda i: (i, 0))] + specs,
        out_specs=pl.BlockSpec((bt, HEADS, HEAD_DIM), lambda i: (i, 0, 0)),
        out_shape=jax.ShapeDtypeStruct((db, HEADS, HEAD_DIM), F32),
        compiler_params=_cparams(1, 40),
        name="attend_sample",
    )(z, *views)
    return jnp.transpose(out, (1, 0, 2))


def _group_norm_gate(o, gn, gate_logit):
    mu = jnp.mean(o, axis=-1, keepdims=True)
    d = o - mu
    var = jnp.mean(d * d, axis=-1, keepdims=True)
    return (gate_logit * _sigmoid(gate_logit)) * (d * lax.rsqrt(var + EPS) * gn)


def _ret_prompt_kernel(lg_ref, q_ref, k_ref, v_ref, gate_ref, gn_ref, o_ref, s_out_ref, s_scr):
    c = RET_CHUNK
    rows = q_ref.shape[1]
    t = lax.broadcasted_iota(jnp.int32, (c, 1), 0).astype(F32)
    rel = (lax.broadcasted_iota(jnp.int32, (c, c), 0) - lax.broadcasted_iota(jnp.int32, (c, c), 1)).astype(F32)

    @pl.when(pl.program_id(1) == 0)
    def _():
        s_scr[...] = jnp.zeros_like(s_scr)

    consts = []
    for h in range(RET_HEADS):
        lg = lg_ref[h]
        consts.append((
            jnp.where(rel >= 0, jnp.exp(lg * jnp.maximum(rel, 0.0)), 0.0),
            jnp.exp(lg * (t + 1.0)),
            jnp.exp(lg * (c - 1.0 - t)),
            jnp.exp(lg * jnp.full((1, RET_DV), float(c), F32)),
        ))

    def chunk(ci, carry):
        rs = pl.ds(pl.multiple_of(ci * c, c), c)
        for h in range(RET_HEADS):
            decay, inner, tail, chunk_decay = consts[h]
            q, k, vb = q_ref[h, rs, :], k_ref[h, rs, :], v_ref[h, rs, :]
            s = s_scr[h]
            a = _dot_nt(q.astype(BF16), k.astype(BF16)) * decay
            o = _dot(a.astype(BF16), vb) + _dot((q * inner).astype(BF16), s.astype(BF16))
            s_scr[h] = chunk_decay * s + _dot((k * tail).T.astype(BF16), vb)
            gn = gn_ref[:, h * RET_DV:(h + 1) * RET_DV]
            o_ref[h, rs, :] = _group_norm_gate(o, gn, gate_ref[h, rs, :]).astype(o_ref.dtype)
        return carry

    lax.fori_loop(0, rows // c, chunk, 0)

    @pl.when(pl.program_id(1) == pl.num_programs(1) - 1)
    def _():
        s_out_ref[...] = s_scr[...]


def _ret_prompt(log_g, rq, rk, rv, gsw, gn, batch, seq, rows):
    assert seq % rows == 0 and rows % RET_CHUNK == 0
    hm = lambda width: pl.BlockSpec((None, RET_HEADS, rows, width), lambda b, t: (b, 0, t, 0))
    return pl.pallas_call(
        _ret_prompt_kernel,
        grid=(batch, seq // rows),
        in_specs=[
            pl.BlockSpec(memory_space=pltpu.SMEM),
            hm(RET_DK), hm(RET_DK), hm(RET_DV), hm(RET_DV),
            _resident((1, RET_V), lambda b, t: (0, 0)),
        ],
        out_specs=[
            hm(RET_DV),
            pl.BlockSpec((None, RET_HEADS, RET_DK, RET_DV), lambda b, t: (b, 0, 0, 0)),
        ],
        out_shape=[
            jax.ShapeDtypeStruct((batch, RET_HEADS, seq, RET_DV), BF16),
            jax.ShapeDtypeStruct((batch, RET_HEADS, RET_DK, RET_DV), F32),
        ],
        scratch_shapes=[pltpu.VMEM((RET_HEADS, RET_DK, RET_DV), F32)],
        compiler_params=_cparams(2, 40),
        name="retain_prompt",
    )(log_g, rq, rk, rv, gsw, gn)


def _ret_sample_kernel(lg_ref, z_ref, cos_ref, sin_ref, gn_ref, s_ref, o_ref, s_out_ref, o_scr):
    bt = z_ref.shape[0]
    eye = (lax.broadcasted_iota(jnp.int32, (RET_DK, RET_DK), 0)
           == lax.broadcasted_iota(jnp.int32, (RET_DK, RET_DK), 1))
    cos, sin = cos_ref[...], sin_ref[...]

    def column(r):
        return jnp.sum(jnp.where(eye, jnp.broadcast_to(r, (RET_DK, RET_DK)), 0.0), axis=1, keepdims=True)

    def zcols(tile0, off, width):
        return z_ref[:, tile0 * TN + off:tile0 * TN + off + width]

    for h in range(RET_HEADS):
        gamma = jnp.exp(lg_ref[h] * jnp.ones((1, RET_DV), F32))
        q_h = _rope(zcols(COL_RQ, h * RET_DK, RET_DK), cos, sin)
        k_h = _rope(zcols(COL_RK, h * RET_DK, RET_DK), cos, sin) * (RET_DK ** -0.5)
        v_h = zcols(COL_RV, h * RET_DV, RET_DV)
        for bi in range(bt):
            b1 = pl.ds(bi, 1)
            qc = column(q_h[bi:bi + 1, :])
            kc = column(k_h[bi:bi + 1, :])
            vrow = v_h[bi:bi + 1, :]
            s_new = gamma * s_ref[bi, h] + kc * vrow
            s_out_ref[bi, h] = s_new
            o_scr[b1, h * RET_DV:(h + 1) * RET_DV] = jnp.sum(qc * s_new, axis=0, keepdims=True)
    for h in range(RET_HEADS):
        seg = slice(h * RET_DV, (h + 1) * RET_DV)
        gate = z_ref[:, COL_GSW * TN + h * RET_DV:COL_GSW * TN + (h + 1) * RET_DV]
        o_ref[h] = _group_norm_gate(o_scr[:, seg], gn_ref[:, seg], gate)


def _ret_sample(log_g, z, cos_t, sin_t, gn, state, layer, bt):
    db = z.shape[0]
    assert db % bt == 0
    nbt = db // bt
    st = state.reshape((state.shape[0] * db,) + state.shape[2:])
    st_block = (bt, RET_HEADS, RET_DK, RET_DV)
    return pl.pallas_call(
        _ret_sample_kernel,
        grid=(nbt,),
        in_specs=[
            pl.BlockSpec(memory_space=pltpu.SMEM),
            pl.BlockSpec((bt, IN_COLS), lambda i: (i, 0)),
            _resident((1, RET_DK), lambda i: (0, 0)),
            _resident((1, RET_DK), lambda i: (0, 0)),
            _resident((1, RET_V), lambda i: (0, 0)),
            pl.BlockSpec(st_block, lambda i: (layer * nbt + i, 0, 0, 0)),
        ],
        out_specs=[
            pl.BlockSpec((RET_HEADS, bt, RET_DV), lambda i: (0, i, 0)),
            pl.BlockSpec(st_block, lambda i: (i, 0, 0, 0)),
        ],
        out_shape=[
            jax.ShapeDtypeStruct((RET_HEADS, db, RET_DV), F32),
            jax.ShapeDtypeStruct((db, RET_HEADS, RET_DK, RET_DV), F32),
        ],
        scratch_shapes=[pltpu.VMEM((bt, RET_V), F32)],
        compiler_params=_cparams(1, 32),
        name="retain_sample",
    )(log_g, z, cos_t, sin_t, gn, st)


def _merge_kernel(att_ref, ret_ref, ga_ref, gr_ref, x_ref, wa_ref, wr_ref, wo_ref, gpost_ref, gpre_ref,
                  x1_ref, h2_ref, att_scr):
    r = att_ref.shape[1]
    for hh in range(HEADS):
        for c in range(r):
            rs = pl.ds(c, att_ref.shape[2], stride=r) if r > 1 else slice(None)
            att_scr[hh, rs, :] = att_ref[hh, c]
    att = jnp.concatenate([att_scr[hh].astype(BF16) for hh in range(HEADS)], axis=1)
    ret = jnp.concatenate([ret_ref[hh].astype(BF16) for hh in range(RET_HEADS)], axis=1)
    m = _sigmoid(ga_ref[...]) * _dot(att, wa_ref[...]) + _sigmoid(gr_ref[...]) * _dot(ret, wr_ref[...])
    y = _dot(m.astype(BF16), wo_ref[...])
    x1 = x_ref[...] + _rms(y) * gpost_ref[...]
    x1_ref[...] = x1
    h2_ref[...] = (_rms(x1) * gpre_ref[...]).astype(BF16)


def _merge(att, ret, ga, gr, x2d, wa, wr, wo, gpost, gpre, tm):
    batch, _, r, ni, _ = att.shape
    seq = r * ni
    assert seq % tm == 0 and tm % r == 0 and x2d.shape[0] == batch * seq
    tps = seq // tm
    rows = pl.BlockSpec((tm, D_MODEL), lambda b, t: (b * tps + t, 0))
    full = lambda a: _resident(a.shape, lambda b, t: (0, 0))
    return pl.pallas_call(
        _merge_kernel,
        grid=(batch, tps),
        in_specs=[pl.BlockSpec((None, HEADS, r, tm // r, HEAD_DIM), lambda b, t: (b, 0, 0, t, 0)),
                  pl.BlockSpec((None, RET_HEADS, tm, RET_DV), lambda b, t: (b, 0, t, 0)),
                  rows, rows, rows, full(wa), full(wr), full(wo), full(gpost), full(gpre)],
        out_specs=[rows, rows],
        out_shape=[jax.ShapeDtypeStruct((batch * seq, D_MODEL), F32),
                   jax.ShapeDtypeStruct((batch * seq, D_MODEL), BF16)],
        scratch_shapes=[pltpu.VMEM((HEADS, tm, HEAD_DIM), F32)],
        compiler_params=_cparams(2, 40),
        name="merge",
    )(att, ret, ga, gr, x2d, wa, wr, wo, gpost, gpre)


FF_CHUNK = 1024


def _mlp_kernel(h_ref, x_ref, wu_ref, wd_ref, gpost_ref, y_ref):
    h = h_ref[...]
    acc = jnp.zeros(x_ref.shape, F32)
    for c in range(D_FF // FF_CHUNK):
        cs = slice(c * FF_CHUNK, (c + 1) * FF_CHUNK)
        u = jnp.maximum(_dot(h, wu_ref[:, cs]), 0.0)
        acc = acc + _dot((u * u).astype(BF16), wd_ref[cs, :])
    y_ref[...] = x_ref[...] + _rms(acc) * gpost_ref[...]


def _mlp(h2, x1, wu, wd, gpost, tm):
    m = x1.shape[0]
    assert m % tm == 0
    rows = pl.BlockSpec((tm, D_MODEL), lambda i: (i, 0))
    full = lambda a: _resident(a.shape, lambda i: (0, 0))
    return pl.pallas_call(
        _mlp_kernel,
        grid=(m // tm,),
        in_specs=[rows, rows, full(wu), full(wd), full(gpost)],
        out_specs=rows,
        out_shape=jax.ShapeDtypeStruct((m, D_MODEL), F32),
        compiler_params=_cparams(1, 40),
        name="mlp",
    )(h2, x1, wu, wd, gpost)


def _rope_tables(pos):
    half = RET_DK // 2
    inv = ROPE_BASE ** (-jnp.arange(half, dtype=F32) / half)
    ang = pos.astype(F32)[:, None] * inv[None, :]
    cos, sin = jnp.cos(ang), jnp.sin(ang)
    return jnp.concatenate([cos, cos], axis=-1), jnp.concatenate([-sin, sin], axis=-1)


IN_PROJ_TM = 256
TAIL_TM = 512
RET_ROWS = 1024
SAMPLE_BT = 8


def _layer_tail(att, ret, ga, gr, x2d, lw, tm):
    x1, h2 = _merge(att, ret, ga, gr, x2d, lw["w_att_br"], lw["w_ret_br"], lw["w_out"],
                    lw["g_post_mix"], lw["g_pre_mlp"], tm)
    return _mlp(h2, x1, lw["w_up"], lw["w_down"], lw["g_post_mlp"], tm)


def _layer_prompt(x, lw, log_g):
    batch, seq, _ = x.shape
    x2d = x.reshape(batch * seq, D_MODEL)
    cos_t, sin_t = _rope_tables(jnp.arange(seq, dtype=jnp.int32))
    outs = _in_proj_prompt(x2d, lw["g_pre_mix"], lw["w_in"], cos_t, sin_t, batch, seq, IN_PROJ_TM)
    qkv = outs[:9]
    kv_rows = [kv.reshape(batch, -1, 2, HEADS, HEAD_DIM) for kv in outs[9:12]]
    rq, rk, rv, gsw, ga, gr = outs[12:]
    att = _attn_prompt(qkv, batch, seq)
    ret, state = _ret_prompt(log_g, rq, rk, rv, gsw, lw["g_ret_norm"], batch, seq, RET_ROWS)
    y = _layer_tail(att, ret, ga, gr, x2d, lw, TAIL_TM)
    return y.reshape(batch, seq, D_MODEL), kv_rows, state


def _layer_sample(x, caches, state, layer, lw, log_g):
    db, t, _ = x.shape
    assert t == 1
    x2d = x.reshape(db, D_MODEL)
    cos_t, sin_t = _rope_tables(PAST_LEN + jnp.arange(1, dtype=jnp.int32))
    z = _in_proj_sample(x2d, lw["g_pre_mix"], lw["w_in"])
    att = _attn_sample(z, caches, layer, SAMPLE_BT)
    ret, state_new = _ret_sample(log_g, z, cos_t, sin_t, lw["g_ret_norm"], state, layer, SAMPLE_BT)
    ga = z[:, COL_GA * TN:COL_GA * TN + D_MODEL]
    gr = z[:, COL_GR * TN:COL_GR * TN + D_MODEL]
    y = _layer_tail(att.reshape(1, HEADS, 1, db, HEAD_DIM), ret[None], ga, gr, x2d, lw, db)
    rows = []
    for g in range(N_GROUPS):
        k_new = z[:, (COL_K + g) * TN:(COL_K + g + 1) * TN]
        v_new = z[:, (COL_V + g) * TN:(COL_V + g + 1) * TN]
        rows.append(jnp.stack([k_new, v_new], axis=1).reshape(db, 1, 2, HEADS, HEAD_DIM))
    return y.reshape(db, 1, D_MODEL), rows, state_new


def _stack(xs):
    return xs[0][None] if len(xs) == 1 else jnp.stack(xs, axis=0)


def kernel(x_prompt, x_sample, cache_kv_d1, cache_kv_d4, cache_kv_d16, state_ret, w_in, w_att_br, w_ret_br, w_out, w_up, w_down, g_ret_norm, g_pre_mix, g_post_mix, g_pre_mlp, g_post_mlp):
    depth = w_in.shape[0]
    log_g = jnp.log1p(-jnp.power(2.0, -5.0 - jnp.arange(RET_HEADS, dtype=F32)))
    caches = (cache_kv_d1, cache_kv_d4, cache_kv_d16)
    xp, xs = x_prompt, x_sample
    p_rows, s_rows = [[], [], []], [[], [], []]
    p_states, s_states = [], []
    for l in range(depth):
        lw = {
            "w_in": w_in[l].astype(BF16), "w_att_br": w_att_br[l].astype(BF16),
            "w_ret_br": w_ret_br[l].astype(BF16), "w_out": w_out[l].astype(BF16),
            "w_up": w_up[l].astype(BF16), "w_down": w_down[l].astype(BF16),
            "g_ret_norm": g_ret_norm[l].reshape(1, RET_V), "g_pre_mix": g_pre_mix[l].reshape(1, D_MODEL),
            "g_post_mix": g_post_mix[l].reshape(1, D_MODEL), "g_pre_mlp": g_pre_mlp[l].reshape(1, D_MODEL),
            "g_post_mlp": g_post_mlp[l].reshape(1, D_MODEL),
        }
        xp, rows_p, sp = _layer_prompt(xp, lw, log_g)
        xs, rows_s, ss = _layer_sample(xs, caches, state_ret, l, lw, log_g)
        for g in range(N_GROUPS):
            p_rows[g].append(rows_p[g])
            s_rows[g].append(rows_s[g])
        p_states.append(sp)
        s_states.append(ss)
    return (xp, xs, _stack(p_rows[0]), _stack(p_rows[1]), _stack(p_rows[2]), _stack(p_states),
            _stack(s_rows[0]), _stack(s_rows[1]), _stack(s_rows[2]), _stack(s_states))
```

```python
import functools

import jax
import jax.numpy as jnp
from jax import lax
from jax.experimental import pallas as pl
from jax.experimental.pallas import tpu as pltpu

F32 = jnp.float32
BF16 = jnp.bfloat16

D_MODEL = 1024
PAST_LEN = 8192

DIL_GROUPS = ((128, 1), (512, 4), (2048, 16))
N_GROUPS = 3
HEADS = 4
HEAD_DIM = 128
ATT_OUT = HEADS * HEAD_DIM
ATT_COLS = N_GROUPS * ATT_OUT
BAND = 128

RET_HEADS = 4
RET_DK = 128
RET_DV = 256
RET_QK = RET_HEADS * RET_DK
RET_V = RET_HEADS * RET_DV
RET_CHUNK = 128
ROPE_BASE = 10000.0

D_FF = 4 * D_MODEL
IN_COLS = 3 * ATT_COLS + 2 * RET_QK + 2 * RET_V + 2 * D_MODEL
EPS = 1e-6
NEG_INF = -1e30

TN = 512
N_COL_TILES = IN_COLS // TN
COL_Q, COL_K, COL_V, COL_RQ, COL_RK, COL_RV, COL_GSW, COL_GA, COL_GR = 0, 3, 6, 9, 10, 11, 13, 15, 17

V7X_VMEM_BYTES = 64 * 1024 * 1024
MIB = 1024 * 1024


def _cparams(n_axes, vmem_mib):
    assert vmem_mib * MIB < V7X_VMEM_BYTES
    return pltpu.CompilerParams(
        dimension_semantics=("arbitrary",) * n_axes,
        vmem_limit_bytes=vmem_mib * MIB,
    )


def _resident(shape, index_map):
    return pl.BlockSpec(shape, index_map, pipeline_mode=pl.Buffered(1))


def _rms(x):
    return x * lax.rsqrt(jnp.mean(x * x, axis=-1, keepdims=True) + EPS)


def _sigmoid(x):
    return 1.0 / (1.0 + jnp.exp(-x))


def _silu(x):
    return x * _sigmoid(x)


def _dot(a, b):
    return jnp.dot(a, b, preferred_element_type=F32)


def _dot_nt(a, b):
    return lax.dot_general(a, b, (((1,), (1,)), ((), ())), preferred_element_type=F32)


def _rope(x, cos, sin):
    return x * cos + pltpu.roll(x, RET_DK // 2, 1) * sin


def _in_proj_prompt_kernel(x_ref, g_ref, w_ref, cos_ref, sin_ref,
                           q0, q1, q2, k0, k1, k2, v0, v1, v2, kvo0, kvo1, kvo2,
                           rq_ref, rk_ref, rv_ref, gsw_ref, ga_ref, gr_ref, scr_q, scr_k, scr_v):
    tm = x_ref.shape[0]
    h = (_rms(x_ref[...]) * g_ref[...]).astype(BF16)

    def tile(j):
        return _dot(h, w_ref[:, j * TN:(j + 1) * TN])

    def head(acc, hh, width=HEAD_DIM):
        return acc[:, hh * width:(hh + 1) * width]

    def scatter_residues(acc, scr, dst, dil):
        if dil == 1:
            for hh in range(HEADS):
                dst[hh, 0] = head(acc, hh).astype(BF16)
            return
        for hh in range(HEADS):
            scr[hh] = head(acc, hh)
        for hh in range(HEADS):
            for c in range(dil):
                dst[hh, c] = scr[hh, pl.ds(c, tm // dil, stride=dil), :].astype(BF16)

    def kv_rows(acc, kvo, which):
        keep = kvo.shape[0] // (2 * HEADS)
        for hh in range(HEADS):
            kvo[pl.ds(which * HEADS + hh, keep, stride=2 * HEADS), :] = head(acc, hh)[tm - keep:, :]

    def att_q(g):
        scatter_residues(tile(COL_Q + g) * (HEAD_DIM ** -0.5), scr_q, (q0, q1, q2)[g], DIL_GROUPS[g][1])

    def att_k(g):
        acc = tile(COL_K + g)
        scatter_residues(acc, scr_k, (k0, k1, k2)[g], DIL_GROUPS[g][1])
        kv_rows(acc, (kvo0, kvo1, kvo2)[g], 0)

    def att_v(g):
        acc = tile(COL_V + g)
        scatter_residues(acc, scr_v, (v0, v1, v2)[g], DIL_GROUPS[g][1])
        kv_rows(acc, (kvo0, kvo1, kvo2)[g], 1)

    def ret_qk(col, dst, scale):
        acc = tile(col)
        cos, sin = cos_ref[...], sin_ref[...]
        for hh in range(RET_HEADS):
            r = _rope(head(acc, hh), cos, sin)
            dst[hh] = r if scale is None else r * scale

    def ret_wide(col, dst, e, act=None):
        acc = tile(col + e)
        for s in range(2):
            part = head(acc, s, RET_DV)
            dst[2 * e + s] = (part if act is None else act(part)).astype(dst.dtype)

    def gate(col, dst, e):
        dst[:, e * TN:(e + 1) * TN] = tile(col + e)

    for g in range(N_GROUPS):
        att_q(g)
        att_k(g)
        att_v(g)
    ret_qk(COL_RQ, rq_ref, None)
    ret_qk(COL_RK, rk_ref, RET_DK ** -0.5)
    for e in range(2):
        ret_wide(COL_RV, rv_ref, e)
        ret_wide(COL_GSW, gsw_ref, e, _silu)
        gate(COL_GA, ga_ref, e)
        gate(COL_GR, gr_ref, e)


def _in_proj_prompt(x2d, g, w_bf, cos_t, sin_t, batch, seq, tm):
    assert seq % tm == 0 and tm % (16 * DIL_GROUPS[-1][1]) == 0
    tps = seq // tm
    out_shape, out_specs = [], []
    for _ in range(3):
        for _, dil in DIL_GROUPS:
            out_shape.append(jax.ShapeDtypeStruct((batch, HEADS, dil, seq // dil, HEAD_DIM), BF16))
            out_specs.append(pl.BlockSpec((None, HEADS, dil, tm // dil, HEAD_DIM), lambda b, t: (b, 0, 0, t, 0)))
    for win, _ in DIL_GROUPS:
        keep = min(win, seq)
        rows = min(keep, tm)
        assert keep % rows == 0
        first = (seq - keep) // rows
        out_shape.append(jax.ShapeDtypeStruct((batch, keep * 2 * HEADS, HEAD_DIM), F32))
        out_specs.append(pl.BlockSpec(
            (None, rows * 2 * HEADS, HEAD_DIM),
            functools.partial(lambda b, t, first, per: (b, jnp.maximum((t + 1) * per - 1 - first, 0), 0),
                              first=first, per=tm // rows)))
    hm = lambda width, dt: (jax.ShapeDtypeStruct((batch, RET_HEADS, seq, width), dt),
                            pl.BlockSpec((None, RET_HEADS, tm, width), lambda b, t: (b, 0, t, 0)))
    nat = (jax.ShapeDtypeStruct((batch * seq, D_MODEL), F32),
           pl.BlockSpec((tm, D_MODEL), lambda b, t: (b * tps + t, 0)))
    for shp, spec in (hm(RET_DK, F32), hm(RET_DK, F32), hm(RET_DV, BF16), hm(RET_DV, F32), nat, nat):
        out_shape.append(shp)
        out_specs.append(spec)
    return pl.pallas_call(
        _in_proj_prompt_kernel,
        grid=(batch, tps),
        in_specs=[
            pl.BlockSpec((tm, D_MODEL), lambda b, t: (b * tps + t, 0)),
            _resident((1, D_MODEL), lambda b, t: (0, 0)),
            _resident((D_MODEL, IN_COLS), lambda b, t: (0, 0)),
            pl.BlockSpec((tm, RET_DK), lambda b, t: (t, 0)),
            pl.BlockSpec((tm, RET_DK), lambda b, t: (t, 0)),
        ],
        out_specs=out_specs,
        out_shape=out_shape,
        scratch_shapes=[pltpu.VMEM((HEADS, tm, HEAD_DIM), F32)] * 3,
        compiler_params=_cparams(2, 56),
        name="in_proj_prompt",
    )(x2d, g, w_bf, cos_t, sin_t)


def _in_proj_sample_kernel(x_ref, g_ref, w_ref, z_ref, h_ref):
    @pl.when(pl.program_id(0) == 0)
    def _():
        h_ref[...] = (_rms(x_ref[...]) * g_ref[...]).astype(BF16)

    z_ref[...] = _dot(h_ref[...], w_ref[...])


def _in_proj_sample(x2d, g, w_bf):
    m = x2d.shape[0]
    return pl.pallas_call(
        _in_proj_sample_kernel,
        grid=(N_COL_TILES,),
        in_specs=[
            _resident((m, D_MODEL), lambda j: (0, 0)),
            _resident((1, D_MODEL), lambda j: (0, 0)),
            pl.BlockSpec((D_MODEL, TN), lambda j: (0, j)),
        ],
        out_specs=pl.BlockSpec((m, TN), lambda j: (0, j)),
        out_shape=jax.ShapeDtypeStruct((m, IN_COLS), F32),
        scratch_shapes=[pltpu.VMEM((m, D_MODEL), BF16)],
        compiler_params=_cparams(1, 16),
        name="in_proj_sample",
    )(x2d, g, w_bf)


def _softmax_block(s, v):
    m = jnp.max(s, axis=1, keepdims=True)
    p = jnp.exp(s - m).astype(BF16)
    uv = _dot(p, jnp.concatenate([v, jnp.ones_like(v)], axis=1))
    return uv[:, :HEAD_DIM], jnp.broadcast_to(m, (s.shape[0], HEAD_DIM)), uv[:, HEAD_DIM:]


ATT_BLOCKS_PER_STEP = 8
ATT_ORDER = 4


def _attn_prompt_kernel(q0, q1, q2, k0, k1, k2, v0, v1, v2, o_ref,
                        u0, u1, u2, m0, m1, m2, l0, l1, l2, *, seq):
    qs, ks, vs = (q0, q1, q2), (k0, k1, k2), (v0, v1, v2)
    us, ms, ls = (u0, u1, u2), (m0, m1, m2), (l0, l1, l2)
    order = ATT_ORDER
    n_out = seq // order
    row1 = lax.broadcasted_iota(jnp.int32, (BAND, BAND), 0)
    col1 = lax.broadcasted_iota(jnp.int32, (BAND, BAND), 1)
    causal = col1 <= row1
    row2 = lax.broadcasted_iota(jnp.int32, (BAND, 2 * BAND), 0)
    col2 = lax.broadcasted_iota(jnp.int32, (BAND, 2 * BAND), 1)
    band = (col2 >= row2) & (col2 <= row2 + BAND)

    aligned = lambda r: r if isinstance(r, int) else pl.multiple_of(r, BAND)

    for g, (_, dil) in enumerate(DIL_GROUPS):
        n = seq // dil
        nb = n // BAND

        def run_blocks(blocks, g=g, n=n, dil=dil):
            scores, values = [], []
            for c, blk, has_prev in blocks:
                own = pl.ds(aligned(blk * BAND), BAND)
                keys = pl.ds(aligned((blk - 1) * BAND), 2 * BAND) if has_prev else own
                mask = band if has_prev else causal
                scores.append(jnp.where(mask, _dot_nt(qs[g][c, own, :], ks[g][c, keys, :]), NEG_INF))
                values.append(vs[g][c, keys, :])
            results = [_softmax_block(s, v) for s, v in zip(scores, values)]
            for (c, blk, _), (u, m, l) in zip(blocks, results):
                if dil <= order:
                    rs = pl.ds(aligned(c * n + blk * BAND), BAND)
                else:
                    step = dil // order
                    rs = pl.ds((c % order) * n_out + c // order + blk * (BAND * step), BAND, stride=step)
                us[g][rs, :] = u
                ms[g][rs, :] = m
                ls[g][rs, :] = l

        per = ATT_BLOCKS_PER_STEP
        assert (dil * nb) % per == 0 and (per % nb == 0 or nb % per == 0)
        if nb >= per:
            def body(i, carry, run_blocks=run_blocks, steps=nb // per):
                c, i0 = i // steps, (i % steps) * per
                run_blocks([(c, i0 + e, True) for e in range(per)])
                return carry
            for c in range(dil):
                run_blocks([(c, e, e > 0) for e in range(per)])
                lax.fori_loop(c * (nb // per) + 1, (c + 1) * (nb // per), body, 0)
        else:
            def body(i, carry, run_blocks=run_blocks, nb=nb, cps=per // nb):
                run_blocks([(i * cps + e // nb, e % nb, e % nb > 0) for e in range(per)])
                return carry
            lax.fori_loop(0, dil * nb // per, body, 0)

    chunks = n_out // BAND

    def combine_one(j):
        c, t = j // chunks, j % chunks
        rows = []
        for _, dil in DIL_GROUPS:
            r = min(dil, order)
            step = order // r
            start = (c % r) * (seq // r) + c // r + t * (BAND * step)
            rows.append(pl.ds(aligned(start), BAND) if step == 1 else pl.ds(start, BAND, stride=step))
        m_g = [ms[g][rows[g], :] for g in range(N_GROUPS)]
        m = jnp.maximum(jnp.maximum(m_g[0], m_g[1]), m_g[2])
        num = jnp.zeros((BAND, HEAD_DIM), F32)
        den = jnp.zeros((BAND, HEAD_DIM), F32)
        for g in range(N_GROUPS):
            w = jnp.exp(m_g[g] - m)
            num = num + w * us[g][rows[g], :]
            den = den + w * ls[g][rows[g], :]
        return num / den

    per = 4
    assert (order * chunks) % per == 0

    def combine(i, carry):
        outs = [combine_one(i * per + e) for e in range(per)]
        for e, o in enumerate(outs):
            o_ref[pl.ds(aligned((i * per + e) * BAND), BAND), :] = o
        return carry

    lax.fori_loop(0, order * chunks // per, combine, 0)


def _attn_prompt(qkv, batch, seq):
    assert seq % (BAND * DIL_GROUPS[-1][1]) == 0
    assert all(ATT_ORDER % d == 0 or d % ATT_ORDER == 0 for _, d in DIL_GROUPS)
    in_specs = [pl.BlockSpec((None, None) + a.shape[2:], lambda b, h: (b, h, 0, 0, 0)) for a in qkv]
    out = pl.pallas_call(
        functools.partial(_attn_prompt_kernel, seq=seq),
        grid=(batch, HEADS),
        in_specs=in_specs,
        out_specs=pl.BlockSpec((None, None, seq, HEAD_DIM), lambda b, h: (b, h, 0, 0)),
        out_shape=jax.ShapeDtypeStruct((batch, HEADS, seq, HEAD_DIM), F32),
        scratch_shapes=[pltpu.VMEM((seq, HEAD_DIM), F32)] * 9,
        compiler_params=_cparams(2, 32),
        name="attend_prompt",
    )(*qkv)
    return out.reshape(batch, HEADS, ATT_ORDER, seq // ATT_ORDER, HEAD_DIM)


def _attn_sample_kernel(z_ref, c0, c1, c2, o_ref):
    cs = (c0, c1, c2)
    bt = z_ref.shape[0]
    half = BAND // 2

    def heads(b1, tile0, g):
        base = (tile0 + g) * TN
        return jnp.concatenate(
            [z_ref[b1, base + hh * HEAD_DIM:base + (hh + 1) * HEAD_DIM] for hh in range(HEADS)], axis=0)

    def paired(ref, bi, lo):
        return jnp.concatenate([ref[bi, 0:half, lo:lo + HEADS, :], ref[bi, half:BAND, lo:lo + HEADS, :]], axis=1)

    twice = lambda a: jnp.concatenate([a, a], axis=0)
    fold = lambda a, op: op(a[0:HEADS], a[HEADS:2 * HEADS])

    for bi in range(bt):
        b1 = pl.ds(bi, 1)
        us, ms, ls = [], [], []
        for g in range(N_GROUPS):
            q = heads(b1, COL_Q, g) * (HEAD_DIM ** -0.5)
            k_new, v_new = heads(b1, COL_K, g), heads(b1, COL_V, g)
            k_buf, v_buf = paired(cs[g], bi, 0), paired(cs[g], bi, HEADS)
            s_buf = jnp.sum(k_buf * twice(q)[None], axis=-1, keepdims=True)
            s_new = jnp.sum(k_new * q, axis=-1, keepdims=True)
            m = jnp.maximum(fold(jnp.max(s_buf, axis=0), jnp.maximum), s_new)
            p_buf = jnp.exp(s_buf - twice(m)[None])
            p_new = jnp.exp(s_new - m)
            ls.append(fold(jnp.sum(p_buf, axis=0), jnp.add) + p_new)
            us.append(fold(jnp.sum(p_buf * v_buf, axis=0), jnp.add) + p_new * v_new)
            ms.append(m)
        m = jnp.maximum(jnp.maximum(ms[0], ms[1]), ms[2])
        num = jnp.zeros((HEADS, HEAD_DIM), F32)
        den = jnp.zeros((HEADS, 1), F32)
        for g in range(N_GROUPS):
            w = jnp.exp(ms[g] - m)
            num = num + w * us[g]
            den = den + w * ls[g]
        o_ref[bi] = num / den


def _attn_sample(z, caches, layer, bt):
    db = z.shape[0]
    assert db % bt == 0
    nbt = db // bt
    views, specs = [], []
    for (win, dil), c in zip(DIL_GROUPS, caches):
        assert c.shape[2] == win and win // dil == BAND
        views.append(c.reshape(c.shape[0] * db, BAND, dil, 2 * HEADS, HEAD_DIM))
        specs.append(pl.BlockSpec((bt, BAND, None, 2 * HEADS, HEAD_DIM),
                                  lambda i: (layer * nbt + i, 0, 0, 0, 0)))
    out = pl.pallas_call(
        _attn_sample_kernel,
        grid=(nbt,),
        in_specs=[pl.BlockSpec((bt, IN_COLS), lambda i: (i, 0))] + specs,
        out_specs=pl.BlockSpec((bt, HEADS, HEAD_DIM), lambda i: (i, 0, 0)),
        out_shape=jax.ShapeDtypeStruct((db, HEADS, HEAD_DIM), F32),
        compiler_params=_cparams(1, 40),
        name="attend_sample",
    )(z, *views)
    return jnp.transpose(out, (1, 0, 2))


def _group_norm_gate(o, gn, gate):
    mu = jnp.mean(o, axis=-1, keepdims=True)
    d = o - mu
    var = jnp.mean(d * d, axis=-1, keepdims=True)
    return gate * (d * lax.rsqrt(var + EPS) * gn)


RET_CHUNKS_PER_STEP = 2


def _ret_prompt_kernel(lg_ref, q_ref, k_ref, v_ref, gate_ref, gn_ref, o_ref, s_out_ref, s_scr):
    c = RET_CHUNK
    rows = q_ref.shape[1]
    t = lax.broadcasted_iota(jnp.int32, (c, 1), 0).astype(F32)
    rel = (lax.broadcasted_iota(jnp.int32, (c, c), 0) - lax.broadcasted_iota(jnp.int32, (c, c), 1)).astype(F32)

    @pl.when(pl.program_id(1) == 0)
    def _():
        s_scr[...] = jnp.zeros_like(s_scr)

    consts = []
    for h in range(RET_HEADS):
        lg = lg_ref[h]
        consts.append((
            jnp.where(rel >= 0, jnp.exp(lg * jnp.maximum(rel, 0.0)), 0.0),
            jnp.exp(lg * (t + 1.0)),
            jnp.exp(lg * (c - 1.0 - t)),
            jnp.exp(lg * jnp.full((1, RET_DV), float(c), F32)),
        ))

    per = RET_CHUNKS_PER_STEP
    assert (rows // c) % per == 0

    def step(i, carry):
        rss = [pl.ds(pl.multiple_of((i * per + e) * c, c), c) for e in range(per)]
        local = {}
        for h in range(RET_HEADS):
            decay, inner, tail, _ = consts[h]
            for e, rs in enumerate(rss):
                q, k, vb = q_ref[h, rs, :], k_ref[h, rs, :], v_ref[h, rs, :]
                a = _dot_nt(q.astype(BF16), k.astype(BF16)) * decay
                local[h, e] = (_dot(a.astype(BF16), vb),
                               (q * inner).astype(BF16),
                               _dot((k * tail).T.astype(BF16), vb))
        outs = {}
        for h in range(RET_HEADS):
            s = s_scr[h]
            for e in range(per):
                intra, q_in, kv = local[h, e]
                outs[h, e] = intra + _dot(q_in, s.astype(BF16))
                s = consts[h][3] * s + kv
            s_scr[h] = s
        for h in range(RET_HEADS):
            gn = gn_ref[:, h * RET_DV:(h + 1) * RET_DV]
            for e, rs in enumerate(rss):
                o_ref[h, rs, :] = _group_norm_gate(outs[h, e], gn, gate_ref[h, rs, :]).astype(o_ref.dtype)
        return carry

    lax.fori_loop(0, rows // (c * per), step, 0)

    @pl.when(pl.program_id(1) == pl.num_programs(1) - 1)
    def _():
        s_out_ref[...] = s_scr[...]


def _ret_prompt(log_g, rq, rk, rv, gsw, gn, batch, seq, rows):
    assert seq % rows == 0 and rows % RET_CHUNK == 0
    hm = lambda width: pl.BlockSpec((None, RET_HEADS, rows, width), lambda b, t: (b, 0, t, 0))
    return pl.pallas_call(
        _ret_prompt_kernel,
        grid=(batch, seq // rows),
        in_specs=[
            pl.BlockSpec(memory_space=pltpu.SMEM),
            hm(RET_DK), hm(RET_DK), hm(RET_DV), hm(RET_DV),
            _resident((1, RET_V), lambda b, t: (0, 0)),
        ],
        out_specs=[
            hm(RET_DV),
            pl.BlockSpec((None, RET_HEADS, RET_DK, RET_DV), lambda b, t: (b, 0, 0, 0)),
        ],
        out_shape=[
            jax.ShapeDtypeStruct((batch, RET_HEADS, seq, RET_DV), BF16),
            jax.ShapeDtypeStruct((batch, RET_HEADS, RET_DK, RET_DV), F32),
        ],
        scratch_shapes=[pltpu.VMEM((RET_HEADS, RET_DK, RET_DV), F32)],
        compiler_params=_cparams(2, 40),
        name="retain_prompt",
    )(log_g, rq, rk, rv, gsw, gn)


def _ret_sample_kernel(lg_ref, z_ref, cos_ref, sin_ref, gn_ref, s_ref, o_ref, s_out_ref, o_scr):
    bt = z_ref.shape[0]
    eye = (lax.broadcasted_iota(jnp.int32, (RET_DK, RET_DK), 0)
           == lax.broadcasted_iota(jnp.int32, (RET_DK, RET_DK), 1))
    cos, sin = cos_ref[...], sin_ref[...]

    def column(r):
        return jnp.sum(jnp.where(eye, jnp.broadcast_to(r, (RET_DK, RET_DK)), 0.0), axis=1, keepdims=True)

    def zcols(tile0, off, width):
        return z_ref[:, tile0 * TN + off:tile0 * TN + off + width]

    for h in range(RET_HEADS):
        gamma = jnp.exp(lg_ref[h] * jnp.ones((1, RET_DV), F32))
        q_h = _rope(zcols(COL_RQ, h * RET_DK, RET_DK), cos, sin)
        k_h = _rope(zcols(COL_RK, h * RET_DK, RET_DK), cos, sin) * (RET_DK ** -0.5)
        v_h = zcols(COL_RV, h * RET_DV, RET_DV)
        for bi in range(bt):
            b1 = pl.ds(bi, 1)
            qc = column(q_h[bi:bi + 1, :])
            kc = column(k_h[bi:bi + 1, :])
            vrow = v_h[bi:bi + 1, :]
            s_new = gamma * s_ref[bi, h] + kc * vrow
            s_out_ref[bi, h] = s_new
            o_scr[b1, h * RET_DV:(h + 1) * RET_DV] = jnp.sum(qc * s_new, axis=0, keepdims=True)
    for h in range(RET_HEADS):
        seg = slice(h * RET_DV, (h + 1) * RET_DV)
        gate = z_ref[:, COL_GSW * TN + h * RET_DV:COL_GSW * TN + (h + 1) * RET_DV]
        o_ref[h] = _group_norm_gate(o_scr[:, seg], gn_ref[:, seg], _silu(gate))


def _ret_sample(log_g, z, cos_t, sin_t, gn, state, layer, bt):
    db = z.shape[0]
    assert db % bt == 0
    nbt = db // bt
    st = state.reshape((state.shape[0] * db,) + state.shape[2:])
    st_block = (bt, RET_HEADS, RET_DK, RET_DV)
    return pl.pallas_call(
        _ret_sample_kernel,
        grid=(nbt,),
        in_specs=[
            pl.BlockSpec(memory_space=pltpu.SMEM),
            pl.BlockSpec((bt, IN_COLS), lambda i: (i, 0)),
            _resident((1, RET_DK), lambda i: (0, 0)),
            _resident((1, RET_DK), lambda i: (0, 0)),
            _resident((1, RET_V), lambda i: (0, 0)),
            pl.BlockSpec(st_block, lambda i: (layer * nbt + i, 0, 0, 0)),
        ],
        out_specs=[
            pl.BlockSpec((RET_HEADS, bt, RET_DV), lambda i: (0, i, 0)),
            pl.BlockSpec(st_block, lambda i: (i, 0, 0, 0)),
        ],
        out_shape=[
            jax.ShapeDtypeStruct((RET_HEADS, db, RET_DV), F32),
            jax.ShapeDtypeStruct((db, RET_HEADS, RET_DK, RET_DV), F32),
        ],
        scratch_shapes=[pltpu.VMEM((bt, RET_V), F32)],
        compiler_params=_cparams(1, 32),
        name="retain_sample",
    )(log_g, z, cos_t, sin_t, gn, st)


def _merge_kernel(att_ref, ret_ref, ga_ref, gr_ref, x_ref, wa_ref, wr_ref, wo_ref, gpost_ref, gpre_ref,
                  x1_ref, h2_ref, att_scr):
    r = att_ref.shape[1]
    for hh in range(HEADS):
        for c in range(r):
            rs = pl.ds(c, att_ref.shape[2], stride=r) if r > 1 else slice(None)
            att_scr[hh, rs, :] = att_ref[hh, c]
    att = jnp.concatenate([att_scr[hh].astype(BF16) for hh in range(HEADS)], axis=1)
    ret = jnp.concatenate([ret_ref[hh].astype(BF16) for hh in range(RET_HEADS)], axis=1)
    m = _sigmoid(ga_ref[...]) * _dot(att, wa_ref[...]) + _sigmoid(gr_ref[...]) * _dot(ret, wr_ref[...])
    y = _dot(m.astype(BF16), wo_ref[...])
    x1 = x_ref[...] + _rms(y) * gpost_ref[...]
    x1_ref[...] = x1
    h2_ref[...] = (_rms(x1) * gpre_ref[...]).astype(BF16)


def _merge(att, ret, ga, gr, x2d, wa, wr, wo, gpost, gpre, tm):
    batch, _, r, ni, _ = att.shape
    seq = r * ni
    assert seq % tm == 0 and tm % r == 0 and x2d.shape[0] == batch * seq
    tps = seq // tm
    rows = pl.BlockSpec((tm, D_MODEL), lambda b, t: (b * tps + t, 0))
    full = lambda a: _resident(a.shape, lambda b, t: (0, 0))
    return pl.pallas_call(
        _merge_kernel,
        grid=(batch, tps),
        in_specs=[pl.BlockSpec((None, HEADS, r, tm // r, HEAD_DIM), lambda b, t: (b, 0, 0, t, 0)),
                  pl.BlockSpec((None, RET_HEADS, tm, RET_DV), lambda b, t: (b, 0, t, 0)),
                  rows, rows, rows, full(wa), full(wr), full(wo), full(gpost), full(gpre)],
        out_specs=[rows, rows],
        out_shape=[jax.ShapeDtypeStruct((batch * seq, D_MODEL), F32),
                   jax.ShapeDtypeStruct((batch * seq, D_MODEL), BF16)],
        scratch_shapes=[pltpu.VMEM((HEADS, tm, HEAD_DIM), F32)],
        compiler_params=_cparams(2, 40),
        name="merge",
    )(att, ret, ga, gr, x2d, wa, wr, wo, gpost, gpre)


FF_CHUNK = 1024


def _mlp_kernel(h_ref, x_ref, wu_ref, wd_ref, gpost_ref, y_ref):
    h = h_ref[...]
    acc = jnp.zeros(x_ref.shape, F32)
    for c in range(D_FF // FF_CHUNK):
        cs = slice(c * FF_CHUNK, (c + 1) * FF_CHUNK)
        u = jnp.maximum(_dot(h, wu_ref[:, cs]), 0.0)
        acc = acc + _dot((u * u).astype(BF16), wd_ref[cs, :])
    y_ref[...] = x_ref[...] + _rms(acc) * gpost_ref[...]


def _mlp(h2, x1, wu, wd, gpost, tm):
    m = x1.shape[0]
    assert m % tm == 0
    rows = pl.BlockSpec((tm, D_MODEL), lambda i: (i, 0))
    full = lambda a: _resident(a.shape, lambda i: (0, 0))
    return pl.pallas_call(
        _mlp_kernel,
        grid=(m // tm,),
        in_specs=[rows, rows, full(wu), full(wd), full(gpost)],
        out_specs=rows,
        out_shape=jax.ShapeDtypeStruct((m, D_MODEL), F32),
        compiler_params=_cparams(1, 40),
        name="mlp",
    )(h2, x1, wu, wd, gpost)


def _rope_tables(pos):
    half = RET_DK // 2
    inv = ROPE_BASE ** (-jnp.arange(half, dtype=F32) / half)
    ang = pos.astype(F32)[:, None] * inv[None, :]
    cos, sin = jnp.cos(ang), jnp.sin(ang)
    return jnp.concatenate([cos, cos], axis=-1), jnp.concatenate([-sin, sin], axis=-1)


IN_PROJ_TM = 256
TAIL_TM = 512
RET_ROWS = 1024
SAMPLE_BT = 8


def _layer_tail(att, ret, ga, gr, x2d, lw, tm):
    x1, h2 = _merge(att, ret, ga, gr, x2d, lw["w_att_br"], lw["w_ret_br"], lw["w_out"],
                    lw["g_post_mix"], lw["g_pre_mlp"], tm)
    return _mlp(h2, x1, lw["w_up"], lw["w_down"], lw["g_post_mlp"], tm)


def _layer_prompt(x, lw, log_g):
    batch, seq, _ = x.shape
    x2d = x.reshape(batch * seq, D_MODEL)
    cos_t, sin_t = _rope_tables(jnp.arange(seq, dtype=jnp.int32))
    outs = _in_proj_prompt(x2d, lw["g_pre_mix"], lw["w_in"], cos_t, sin_t, batch, seq, IN_PROJ_TM)
    qkv = outs[:9]
    kv_rows = [kv.reshape(batch, -1, 2, HEADS, HEAD_DIM) for kv in outs[9:12]]
    rq, rk, rv, gsw, ga, gr = outs[12:]
    att = _attn_prompt(qkv, batch, seq)
    ret, state = _ret_prompt(log_g, rq, rk, rv, gsw, lw["g_ret_norm"], batch, seq, RET_ROWS)
    y = _layer_tail(att, ret, ga, gr, x2d, lw, TAIL_TM)
    return y.reshape(batch, seq, D_MODEL), kv_rows, state


def _layer_sample(x, caches, state, layer, lw, log_g):
    db, t, _ = x.shape
    assert t == 1
    x2d = x.reshape(db, D_MODEL)
    cos_t, sin_t = _rope_tables(PAST_LEN + jnp.arange(1, dtype=jnp.int32))
    z = _in_proj_sample(x2d, lw["g_pre_mix"], lw["w_in"])
    att = _attn_sample(z, caches, layer, SAMPLE_BT)
    ret, state_new = _ret_sample(log_g, z, cos_t, sin_t, lw["g_ret_norm"], state, layer, SAMPLE_BT)
    ga = z[:, COL_GA * TN:COL_GA * TN + D_MODEL]
    gr = z[:, COL_GR * TN:COL_GR * TN + D_MODEL]
    y = _layer_tail(att.reshape(1, HEADS, 1, db, HEAD_DIM), ret[None], ga, gr, x2d, lw, db)
    rows = []
    for g in range(N_GROUPS):
        k_new = z[:, (COL_K + g) * TN:(COL_K + g + 1) * TN]
        v_new = z[:, (COL_V + g) * TN:(COL_V + g + 1) * TN]
        rows.append(jnp.stack([k_new, v_new], axis=1).reshape(db, 1, 2, HEADS, HEAD_DIM))
    return y.reshape(db, 1, D_MODEL), rows, state_new


def _stack(xs):
    return xs[0][None] if len(xs) == 1 else jnp.stack(xs, axis=0)


def kernel(x_prompt, x_sample, cache_kv_d1, cache_kv_d4, cache_kv_d16, state_ret, w_in, w_att_br, w_ret_br, w_out, w_up, w_down, g_ret_norm, g_pre_mix, g_post_mix, g_pre_mlp, g_post_mlp):
    depth = w_in.shape[0]
    log_g = jnp.log1p(-jnp.power(2.0, -5.0 - jnp.arange(RET_HEADS, dtype=F32)))
    caches = (cache_kv_d1, cache_kv_d4, cache_kv_d16)
    xp, xs = x_prompt, x_sample
    p_rows, s_rows = [[], [], []], [[], [], []]
    p_states, s_states = [], []
    for l in range(depth):
        lw = {
            "w_in": w_in[l].astype(BF16), "w_att_br": w_att_br[l].astype(BF16),
            "w_ret_br": w_ret_br[l].astype(BF16), "w_out": w_out[l].astype(BF16),
            "w_up": w_up[l].astype(BF16), "w_down": w_down[l].astype(BF16),
            "g_ret_norm": g_ret_norm[l].reshape(1, RET_V), "g_pre_mix": g_pre_mix[l].reshape(1, D_MODEL),
            "g_post_mix": g_post_mix[l].reshape(1, D_MODEL), "g_pre_mlp": g_pre_mlp[l].reshape(1, D_MODEL),
            "g_post_mlp": g_post_mlp[l].reshape(1, D_MODEL),
        }
        xp, rows_p, sp = _layer_prompt(xp, lw, log_g)
        xs, rows_s, ss = _layer_sample(xs, caches, state_ret, l, lw, log_g)
        for g in range(N_GROUPS):
            p_rows[g].append(rows_p[g])
            s_rows[g].append(rows_s[g])
        p_states.append(sp)
        s_states.append(ss)
    return (xp, xs, _stack(p_rows[0]), _stack(p_rows[1]), _stack(p_rows[2]), _stack(p_states),
            _stack(s_rows[0]), _stack(s_rows[1]), _stack(s_rows[2]), _stack(s_states))
```

```python
import functools

import jax
import jax.numpy as jnp
from jax import lax
from jax.experimental import pallas as pl
from jax.experimental.pallas import tpu as pltpu

F32 = jnp.float32
BF16 = jnp.bfloat16

D_MODEL = 1024
PAST_LEN = 8192

DIL_GROUPS = ((128, 1), (512, 4), (2048, 16))
N_GROUPS = 3
HEADS = 4
HEAD_DIM = 128
ATT_OUT = HEADS * HEAD_DIM
ATT_COLS = N_GROUPS * ATT_OUT
BAND = 128

RET_HEADS = 4
RET_DK = 128
RET_DV = 256
RET_QK = RET_HEADS * RET_DK
RET_V = RET_HEADS * RET_DV
RET_CHUNK = 128
ROPE_BASE = 10000.0

D_FF = 4 * D_MODEL
IN_COLS = 3 * ATT_COLS + 2 * RET_QK + 2 * RET_V + 2 * D_MODEL
EPS = 1e-6
NEG_INF = -1e30

TN = 512
N_COL_TILES = IN_COLS // TN
COL_Q, COL_K, COL_V, COL_RQ, COL_RK, COL_RV, COL_GSW, COL_GA, COL_GR = 0, 3, 6, 9, 10, 11, 13, 15, 17

V7X_VMEM_BYTES = 64 * 1024 * 1024
MIB = 1024 * 1024


def _cparams(n_axes, vmem_mib):
    assert vmem_mib * MIB < V7X_VMEM_BYTES
    return pltpu.CompilerParams(
        dimension_semantics=("arbitrary",) * n_axes,
        vmem_limit_bytes=vmem_mib * MIB,
    )


def _resident(shape, index_map):
    return pl.BlockSpec(shape, index_map, pipeline_mode=pl.Buffered(1))


def _rms(x):
    return x * lax.rsqrt(jnp.mean(x * x, axis=-1, keepdims=True) + EPS)


def _sigmoid(x):
    return 1.0 / (1.0 + jnp.exp(-x))


def _silu(x):
    return x * _sigmoid(x)


def _dot(a, b):
    return jnp.dot(a, b, preferred_element_type=F32)


def _dot_nt(a, b):
    return lax.dot_general(a, b, (((1,), (1,)), ((), ())), preferred_element_type=F32)


def _rope(x, cos, sin):
    return x * cos + pltpu.roll(x, RET_DK // 2, 1) * sin


def _in_proj_prompt_kernel(x_ref, g_ref, w_ref, cos_ref, sin_ref,
                           q0, q1, q2, k0, k1, k2, v0, v1, v2, kvo0, kvo1, kvo2,
                           rq_ref, rk_ref, rv_ref, gsw_ref, ga_ref, gr_ref, scr_q, scr_k, scr_v):
    tm = x_ref.shape[0]
    h = (_rms(x_ref[...]) * g_ref[...]).astype(BF16)

    def tile(j):
        return _dot(h, w_ref[:, j * TN:(j + 1) * TN])

    def head(acc, hh, width=HEAD_DIM):
        return acc[:, hh * width:(hh + 1) * width]

    def scatter_residues(acc, scr, dst, dil):
        if dil == 1:
            for hh in range(HEADS):
                dst[hh, 0] = head(acc, hh).astype(BF16)
            return
        for hh in range(HEADS):
            scr[hh] = head(acc, hh)
        for hh in range(HEADS):
            for c in range(dil):
                dst[hh, c] = scr[hh, pl.ds(c, tm // dil, stride=dil), :].astype(BF16)

    def kv_rows(acc, kvo, which):
        keep = kvo.shape[0] // (2 * HEADS)
        for hh in range(HEADS):
            kvo[pl.ds(which * HEADS + hh, keep, stride=2 * HEADS), :] = head(acc, hh)[tm - keep:, :]

    def att_q(g):
        scatter_residues(tile(COL_Q + g) * (HEAD_DIM ** -0.5), scr_q, (q0, q1, q2)[g], DIL_GROUPS[g][1])

    def att_k(g):
        acc = tile(COL_K + g)
        scatter_residues(acc, scr_k, (k0, k1, k2)[g], DIL_GROUPS[g][1])
        kv_rows(acc, (kvo0, kvo1, kvo2)[g], 0)

    def att_v(g):
        acc = tile(COL_V + g)
        scatter_residues(acc, scr_v, (v0, v1, v2)[g], DIL_GROUPS[g][1])
        kv_rows(acc, (kvo0, kvo1, kvo2)[g], 1)

    def ret_qk(col, dst, scale):
        acc = tile(col)
        cos, sin = cos_ref[...], sin_ref[...]
        for hh in range(RET_HEADS):
            r = _rope(head(acc, hh), cos, sin)
            dst[hh] = r if scale is None else r * scale

    def ret_wide(col, dst, e, act=None):
        acc = tile(col + e)
        for s in range(2):
            part = head(acc, s, RET_DV)
            dst[2 * e + s] = (part if act is None else act(part)).astype(dst.dtype)

    def gate(col, dst, e):
        dst[:, e * TN:(e + 1) * TN] = tile(col + e)

    for g in range(N_GROUPS):
        att_q(g)
        att_k(g)
        att_v(g)
    ret_qk(COL_RQ, rq_ref, None)
    ret_qk(COL_RK, rk_ref, RET_DK ** -0.5)
    for e in range(2):
        ret_wide(COL_RV, rv_ref, e)
        ret_wide(COL_GSW, gsw_ref, e, _silu)
        gate(COL_GA, ga_ref, e)
        gate(COL_GR, gr_ref, e)


def _in_proj_prompt(x2d, g, w_bf, cos_t, sin_t, batch, seq, tm):
    assert seq % tm == 0 and tm % (16 * DIL_GROUPS[-1][1]) == 0
    tps = seq // tm
    out_shape, out_specs = [], []
    for _ in range(3):
        for _, dil in DIL_GROUPS:
            out_shape.append(jax.ShapeDtypeStruct((batch, HEADS, dil, seq // dil, HEAD_DIM), BF16))
            out_specs.append(pl.BlockSpec((None, HEADS, dil, tm // dil, HEAD_DIM), lambda b, t: (b, 0, 0, t, 0)))
    for win, _ in DIL_GROUPS:
        keep = min(win, seq)
        rows = min(keep, tm)
        assert keep % rows == 0
        first = (seq - keep) // rows
        out_shape.append(jax.ShapeDtypeStruct((batch, keep * 2 * HEADS, HEAD_DIM), F32))
        out_specs.append(pl.BlockSpec(
            (None, rows * 2 * HEADS, HEAD_DIM),
            functools.partial(lambda b, t, first, per: (b, jnp.maximum((t + 1) * per - 1 - first, 0), 0),
                              first=first, per=tm // rows)))
    hm = lambda width, dt: (jax.ShapeDtypeStruct((batch, RET_HEADS, seq, width), dt),
                            pl.BlockSpec((None, RET_HEADS, tm, width), lambda b, t: (b, 0, t, 0)))
    nat = (jax.ShapeDtypeStruct((batch * seq, D_MODEL), F32),
           pl.BlockSpec((tm, D_MODEL), lambda b, t: (b * tps + t, 0)))
    for shp, spec in (hm(RET_DK, F32), hm(RET_DK, F32), hm(RET_DV, BF16), hm(RET_DV, F32), nat, nat):
        out_shape.append(shp)
        out_specs.append(spec)
    return pl.pallas_call(
        _in_proj_prompt_kernel,
        grid=(batch, tps),
        in_specs=[
            pl.BlockSpec((tm, D_MODEL), lambda b, t: (b * tps + t, 0)),
            _resident((1, D_MODEL), lambda b, t: (0, 0)),
            _resident((D_MODEL, IN_COLS), lambda b, t: (0, 0)),
            pl.BlockSpec((tm, RET_DK), lambda b, t: (t, 0)),
            pl.BlockSpec((tm, RET_DK), lambda b, t: (t, 0)),
        ],
        out_specs=out_specs,
        out_shape=out_shape,
        scratch_shapes=[pltpu.VMEM((HEADS, tm, HEAD_DIM), F32)] * 3,
        compiler_params=_cparams(2, 56),
        name="in_proj_prompt",
    )(x2d, g, w_bf, cos_t, sin_t)


def _in_proj_sample_kernel(x_ref, g_ref, w_ref, z_ref, h_ref):
    @pl.when(pl.program_id(0) == 0)
    def _():
        h_ref[...] = (_rms(x_ref[...]) * g_ref[...]).astype(BF16)

    z_ref[...] = _dot(h_ref[...], w_ref[...])


def _in_proj_sample(x2d, g, w_bf):
    m = x2d.shape[0]
    return pl.pallas_call(
        _in_proj_sample_kernel,
        grid=(N_COL_TILES,),
        in_specs=[
            _resident((m, D_MODEL), lambda j: (0, 0)),
            _resident((1, D_MODEL), lambda j: (0, 0)),
            pl.BlockSpec((D_MODEL, TN), lambda j: (0, j)),
        ],
        out_specs=pl.BlockSpec((m, TN), lambda j: (0, j)),
        out_shape=jax.ShapeDtypeStruct((m, IN_COLS), F32),
        scratch_shapes=[pltpu.VMEM((m, D_MODEL), BF16)],
        compiler_params=_cparams(1, 16),
        name="in_proj_sample",
    )(x2d, g, w_bf)


def _softmax_block(s, v):
    m = jnp.max(s, axis=1, keepdims=True)
    p = jnp.exp(s - m).astype(BF16)
    uv = _dot(p, jnp.concatenate([v, jnp.ones_like(v)], axis=1))
    return uv[:, :HEAD_DIM], jnp.broadcast_to(m, (s.shape[0], HEAD_DIM)), uv[:, HEAD_DIM:]


ATT_BLOCKS_PER_STEP = 8
ATT_ORDER = 4


def _attn_prompt_kernel(q0, q1, q2, k0, k1, k2, v0, v1, v2, o_ref,
                        u0, u1, u2, m0, m1, m2, l0, l1, l2, *, seq):
    qs, ks, vs = (q0, q1, q2), (k0, k1, k2), (v0, v1, v2)
    us, ms, ls = (u0, u1, u2), (m0, m1, m2), (l0, l1, l2)
    order = ATT_ORDER
    n_out = seq // order
    row1 = lax.broadcasted_iota(jnp.int32, (BAND, BAND), 0)
    col1 = lax.broadcasted_iota(jnp.int32, (BAND, BAND), 1)
    causal = col1 <= row1
    row2 = lax.broadcasted_iota(jnp.int32, (BAND, 2 * BAND), 0)
    col2 = lax.broadcasted_iota(jnp.int32, (BAND, 2 * BAND), 1)
    band = (col2 >= row2) & (col2 <= row2 + BAND)

    aligned = lambda r: r if isinstance(r, int) else pl.multiple_of(r, BAND)

    for g, (_, dil) in enumerate(DIL_GROUPS):
        n = seq // dil
        nb = n // BAND

        def run_blocks(blocks, g=g, n=n, dil=dil):
            scores, values = [], []
            for c, blk, has_prev in blocks:
                own = pl.ds(aligned(blk * BAND), BAND)
                keys = pl.ds(aligned((blk - 1) * BAND), 2 * BAND) if has_prev else own
                mask = band if has_prev else causal
                scores.append(jnp.where(mask, _dot_nt(qs[g][c, own, :], ks[g][c, keys, :]), NEG_INF))
                values.append(vs[g][c, keys, :])
            results = [_softmax_block(s, v) for s, v in zip(scores, values)]
            for (c, blk, _), (u, m, l) in zip(blocks, results):
                if dil <= order:
                    rs = pl.ds(aligned(c * n + blk * BAND), BAND)
                else:
                    step = dil // order
                    rs = pl.ds((c % order) * n_out + c // order + blk * (BAND * step), BAND, stride=step)
                us[g][rs, :] = u
                ms[g][rs, :] = m
                ls[g][rs, :] = l

        per = ATT_BLOCKS_PER_STEP
        assert (dil * nb) % per == 0 and (per % nb == 0 or nb % per == 0)
        if nb >= per:
            def body(i, carry, run_blocks=run_blocks, steps=nb // per):
                c, i0 = i // steps, (i % steps) * per
                run_blocks([(c, i0 + e, True) for e in range(per)])
                return carry
            for c in range(dil):
                run_blocks([(c, e, e > 0) for e in range(per)])
                lax.fori_loop(c * (nb // per) + 1, (c + 1) * (nb // per), body, 0)
        else:
            def body(i, carry, run_blocks=run_blocks, nb=nb, cps=per // nb):
                run_blocks([(i * cps + e // nb, e % nb, e % nb > 0) for e in range(per)])
                return carry
            lax.fori_loop(0, dil * nb // per, body, 0)

    chunks = n_out // BAND

    def combine_one(j):
        c, t = j // chunks, j % chunks
        rows = []
        for _, dil in DIL_GROUPS:
            r = min(dil, order)
            step = order // r
            start = (c % r) * (seq // r) + c // r + t * (BAND * step)
            rows.append(pl.ds(aligned(start), BAND) if step == 1 else pl.ds(start, BAND, stride=step))
        m_g = [ms[g][rows[g], :] for g in range(N_GROUPS)]
        m = jnp.maximum(jnp.maximum(m_g[0], m_g[1]), m_g[2])
        num = jnp.zeros((BAND, HEAD_DIM), F32)
        den = jnp.zeros((BAND, HEAD_DIM), F32)
        for g in range(N_GROUPS):
            w = jnp.exp(m_g[g] - m)
            num = num + w * us[g][rows[g], :]
            den = den + w * ls[g][rows[g], :]
        return num / den

    per = 4
    assert (order * chunks) % per == 0

    def combine(i, carry):
        outs = [combine_one(i * per + e) for e in range(per)]
        for e, o in enumerate(outs):
            o_ref[pl.ds(aligned((i * per + e) * BAND), BAND), :] = o
        return carry

    lax.fori_loop(0, order * chunks // per, combine, 0)


def _attn_prompt(qkv, batch, seq):
    assert seq % (BAND * DIL_GROUPS[-1][1]) == 0
    assert all(ATT_ORDER % d == 0 or d % ATT_ORDER == 0 for _, d in DIL_GROUPS)
    in_specs = [pl.BlockSpec((None, None) + a.shape[2:], lambda b, h: (b, h, 0, 0, 0)) for a in qkv]
    out = pl.pallas_call(
        functools.partial(_attn_prompt_kernel, seq=seq),
        grid=(batch, HEADS),
        in_specs=in_specs,
        out_specs=pl.BlockSpec((None, None, seq, HEAD_DIM), lambda b, h: (b, h, 0, 0)),
        out_shape=jax.ShapeDtypeStruct((batch, HEADS, seq, HEAD_DIM), F32),
        scratch_shapes=[pltpu.VMEM((seq, HEAD_DIM), F32)] * 9,
        compiler_params=_cparams(2, 32),
        name="attend_prompt",
    )(*qkv)
    return out.reshape(batch, HEADS, ATT_ORDER, seq // ATT_ORDER, HEAD_DIM)


def _group_norm_gate(o, gn, gate):
    mu = jnp.mean(o, axis=-1, keepdims=True)
    d = o - mu
    var = jnp.mean(d * d, axis=-1, keepdims=True)
    return gate * (d * lax.rsqrt(var + EPS) * gn)


RET_CHUNKS_PER_STEP = 2


def _ret_prompt_kernel(lg_ref, q_ref, k_ref, v_ref, gate_ref, gn_ref, o_ref, s_out_ref, s_scr):
    c = RET_CHUNK
    rows = q_ref.shape[1]
    t = lax.broadcasted_iota(jnp.int32, (c, 1), 0).astype(F32)
    rel = (lax.broadcasted_iota(jnp.int32, (c, c), 0) - lax.broadcasted_iota(jnp.int32, (c, c), 1)).astype(F32)

    @pl.when(pl.program_id(1) == 0)
    def _():
        s_scr[...] = jnp.zeros_like(s_scr)

    consts = []
    for h in range(RET_HEADS):
        lg = lg_ref[h]
        consts.append((
            jnp.where(rel >= 0, jnp.exp(lg * jnp.maximum(rel, 0.0)), 0.0),
            jnp.exp(lg * (t + 1.0)),
            jnp.exp(lg * (c - 1.0 - t)),
            jnp.exp(lg * jnp.full((1, RET_DV), float(c), F32)),
        ))

    per = RET_CHUNKS_PER_STEP
    assert (rows // c) % per == 0

    def step(i, carry):
        rss = [pl.ds(pl.multiple_of((i * per + e) * c, c), c) for e in range(per)]
        local = {}
        for h in range(RET_HEADS):
            decay, inner, tail, _ = consts[h]
            for e, rs in enumerate(rss):
                q, k, vb = q_ref[h, rs, :], k_ref[h, rs, :], v_ref[h, rs, :]
                a = _dot_nt(q.astype(BF16), k.astype(BF16)) * decay
                local[h, e] = (_dot(a.astype(BF16), vb),
                               (q * inner).astype(BF16),
                               _dot((k * tail).T.astype(BF16), vb))
        outs = {}
        for h in range(RET_HEADS):
            s = s_scr[h]
            for e in range(per):
                intra, q_in, kv = local[h, e]
                outs[h, e] = intra + _dot(q_in, s.astype(BF16))
                s = consts[h][3] * s + kv
            s_scr[h] = s
        for h in range(RET_HEADS):
            gn = gn_ref[:, h * RET_DV:(h + 1) * RET_DV]
            for e, rs in enumerate(rss):
                o_ref[h, rs, :] = _group_norm_gate(outs[h, e], gn, gate_ref[h, rs, :]).astype(o_ref.dtype)
        return carry

    lax.fori_loop(0, rows // (c * per), step, 0)

    @pl.when(pl.program_id(1) == pl.num_programs(1) - 1)
    def _():
        s_out_ref[...] = s_scr[...]


def _ret_prompt(log_g, rq, rk, rv, gsw, gn, batch, seq, rows):
    assert seq % rows == 0 and rows % RET_CHUNK == 0
    hm = lambda width: pl.BlockSpec((None, RET_HEADS, rows, width), lambda b, t: (b, 0, t, 0))
    return pl.pallas_call(
        _ret_prompt_kernel,
        grid=(batch, seq // rows),
        in_specs=[
            pl.BlockSpec(memory_space=pltpu.SMEM),
            hm(RET_DK), hm(RET_DK), hm(RET_DV), hm(RET_DV),
            _resident((1, RET_V), lambda b, t: (0, 0)),
        ],
        out_specs=[
            hm(RET_DV),
            pl.BlockSpec((None, RET_HEADS, RET_DK, RET_DV), lambda b, t: (b, 0, 0, 0)),
        ],
        out_shape=[
            jax.ShapeDtypeStruct((batch, RET_HEADS, seq, RET_DV), BF16),
            jax.ShapeDtypeStruct((batch, RET_HEADS, RET_DK, RET_DV), F32),
        ],
        scratch_shapes=[pltpu.VMEM((RET_HEADS, RET_DK, RET_DV), F32)],
        compiler_params=_cparams(2, 40),
        name="retain_prompt",
    )(log_g, rq, rk, rv, gsw, gn)


def _sample_attend_row(z_ref, caches, r):
    half = BAND // 2

    def heads(tile0, g):
        base = (tile0 + g) * TN
        return jnp.concatenate(
            [z_ref[r, :, base + hh * HEAD_DIM:base + (hh + 1) * HEAD_DIM] for hh in range(HEADS)], axis=0)

    def paired(ref, lo):
        return jnp.concatenate([ref[r, 0:half, lo:lo + HEADS, :], ref[r, half:BAND, lo:lo + HEADS, :]], axis=1)

    twice = lambda a: jnp.concatenate([a, a], axis=0)
    fold = lambda a, op: op(a[0:HEADS], a[HEADS:2 * HEADS])

    us, ms, ls = [], [], []
    for g in range(N_GROUPS):
        q = heads(COL_Q, g) * (HEAD_DIM ** -0.5)
        k_new, v_new = heads(COL_K, g), heads(COL_V, g)
        k_buf, v_buf = paired(caches[g], 0), paired(caches[g], HEADS)
        s_buf = jnp.sum(k_buf * twice(q)[None], axis=-1, keepdims=True)
        s_new = jnp.sum(k_new * q, axis=-1, keepdims=True)
        m = jnp.maximum(fold(jnp.max(s_buf, axis=0), jnp.maximum), s_new)
        p_buf = jnp.exp(s_buf - twice(m)[None])
        p_new = jnp.exp(s_new - m)
        ls.append(fold(jnp.sum(p_buf, axis=0), jnp.add) + p_new)
        us.append(fold(jnp.sum(p_buf * v_buf, axis=0), jnp.add) + p_new * v_new)
        ms.append(m)
    m = jnp.maximum(jnp.maximum(ms[0], ms[1]), ms[2])
    num = jnp.zeros((HEADS, HEAD_DIM), F32)
    den = jnp.zeros((HEADS, 1), F32)
    for g in range(N_GROUPS):
        w = jnp.exp(ms[g] - m)
        num = num + w * us[g]
        den = den + w * ls[g]
    return num / den


def _sample_retain_head(lg_ref, z_ref, cos, sin, gn_ref, s_ref, ret_ref, s_out_ref, h):
    rows = z_ref.shape[0]
    eye = (lax.broadcasted_iota(jnp.int32, (RET_DK, RET_DK), 0)
           == lax.broadcasted_iota(jnp.int32, (RET_DK, RET_DK), 1))

    def column(v):
        return jnp.sum(jnp.where(eye, jnp.broadcast_to(v, (RET_DK, RET_DK)), 0.0), axis=1, keepdims=True)

    def zcols(tile0, width):
        return z_ref[:, 0, tile0 * TN + h * width:tile0 * TN + (h + 1) * width]

    gamma = jnp.exp(lg_ref[h] * jnp.ones((1, RET_DV), F32))
    q_h = _rope(zcols(COL_RQ, RET_DK), cos, sin)
    k_h = _rope(zcols(COL_RK, RET_DK), cos, sin) * (RET_DK ** -0.5)
    v_h = zcols(COL_RV, RET_DV)
    outs = []
    for r in range(rows):
        s_new = gamma * s_ref[r, h] + column(k_h[r:r + 1, :]) * v_h[r:r + 1, :]
        s_out_ref[r, h] = s_new
        outs.append(jnp.sum(column(q_h[r:r + 1, :]) * s_new, axis=0, keepdims=True))
    seg = slice(h * RET_DV, (h + 1) * RET_DV)
    ret_ref[:, 0, seg] = _group_norm_gate(jnp.concatenate(outs, axis=0), gn_ref[:, seg], _silu(zcols(COL_GSW, RET_DV)))


def _merge_kernel(att_ref, ret_ref, ga_ref, gr_ref, x_ref, wa_ref, wr_ref, wo_ref, gpost_ref, gpre_ref,
                  x1_ref, h2_ref, att_scr):
    r = att_ref.shape[1]
    for hh in range(HEADS):
        for c in range(r):
            rs = pl.ds(c, att_ref.shape[2], stride=r) if r > 1 else slice(None)
            att_scr[hh, rs, :] = att_ref[hh, c]
    att = jnp.concatenate([att_scr[hh].astype(BF16) for hh in range(HEADS)], axis=1)
    ret = jnp.concatenate([ret_ref[hh].astype(BF16) for hh in range(RET_HEADS)], axis=1)
    m = _sigmoid(ga_ref[...]) * _dot(att, wa_ref[...]) + _sigmoid(gr_ref[...]) * _dot(ret, wr_ref[...])
    y = _dot(m.astype(BF16), wo_ref[...])
    x1 = x_ref[...] + _rms(y) * gpost_ref[...]
    x1_ref[...] = x1
    h2_ref[...] = (_rms(x1) * gpre_ref[...]).astype(BF16)


def _merge(att, ret, ga, gr, x2d, wa, wr, wo, gpost, gpre, tm):
    batch, _, r, ni, _ = att.shape
    seq = r * ni
    assert seq % tm == 0 and tm % r == 0 and x2d.shape[0] == batch * seq
    tps = seq // tm
    rows = pl.BlockSpec((tm, D_MODEL), lambda b, t: (b * tps + t, 0))
    full = lambda a: _resident(a.shape, lambda b, t: (0, 0))
    return pl.pallas_call(
        _merge_kernel,
        grid=(batch, tps),
        in_specs=[pl.BlockSpec((None, HEADS, r, tm // r, HEAD_DIM), lambda b, t: (b, 0, 0, t, 0)),
                  pl.BlockSpec((None, RET_HEADS, tm, RET_DV), lambda b, t: (b, 0, t, 0)),
                  rows, rows, rows, full(wa), full(wr), full(wo), full(gpost), full(gpre)],
        out_specs=[rows, rows],
        out_shape=[jax.ShapeDtypeStruct((batch * seq, D_MODEL), F32),
                   jax.ShapeDtypeStruct((batch * seq, D_MODEL), BF16)],
        scratch_shapes=[pltpu.VMEM((HEADS, tm, HEAD_DIM), F32)],
        compiler_params=_cparams(2, 40),
        name="merge",
    )(att, ret, ga, gr, x2d, wa, wr, wo, gpost, gpre)


FF_CHUNK = 1024


def _mlp_kernel(*refs, side_rows):
    h_ref, x_ref, wu_ref, wd_ref, gpost_ref = refs[:5]
    side = []
    if side_rows:
        lg_ref, z_ref, cos_ref, sin_ref, gn_ref, c0, c1, c2, s_ref, y_ref, att_ref, ret_ref, s_out_ref = refs[5:]

        def attend(r):
            att_ref[r] = _sample_attend_row(z_ref, (c0, c1, c2), r)

        def retain(h):
            _sample_retain_head(lg_ref, z_ref, cos_ref[...], sin_ref[...], gn_ref, s_ref, ret_ref, s_out_ref, h)

        side = [functools.partial(attend, r) for r in range(side_rows)]
        side += [functools.partial(retain, h) for h in range(RET_HEADS)]
    else:
        y_ref, = refs[5:]

    n_chunks = D_FF // FF_CHUNK
    h = h_ref[...]
    acc = jnp.zeros(x_ref.shape, F32)
    for c in range(n_chunks):
        cs = slice(c * FF_CHUNK, (c + 1) * FF_CHUNK)
        u = jnp.maximum(_dot(h, wu_ref[:, cs]), 0.0)
        acc = acc + _dot((u * u).astype(BF16), wd_ref[cs, :])
        for work in side[c::n_chunks]:
            work()
    y_ref[...] = x_ref[...] + _rms(acc) * gpost_ref[...]


def _mlp(h2, x1, wu, wd, gpost, tm, side=None):
    m = x1.shape[0]
    assert m % tm == 0
    steps = m // tm
    rows = pl.BlockSpec((tm, D_MODEL), lambda i: (i, 0))
    full = lambda a: _resident(a.shape, lambda i: (0,) * a.ndim)
    args = [h2, x1, wu, wd, gpost]
    in_specs = [rows, rows, full(wu), full(wd), full(gpost)]
    out_shape = [jax.ShapeDtypeStruct((m, D_MODEL), F32)]
    out_specs = [rows]
    side_rows = 0
    if side is not None:
        log_g, z, cos_t, sin_t, gn, caches, state, layer = side
        db = z.shape[0]
        assert db % steps == 0
        side_rows = db // steps
        st_block = (side_rows, RET_HEADS, RET_DK, RET_DV)
        args += [log_g, z.reshape(db, 1, IN_COLS), cos_t, sin_t, gn]
        in_specs += [pl.BlockSpec(memory_space=pltpu.SMEM),
                     pl.BlockSpec((side_rows, 1, IN_COLS), lambda i: (i, 0, 0)),
                     full(cos_t), full(sin_t), full(gn)]
        for (win, dil), c in zip(DIL_GROUPS, caches):
            assert c.shape[1] == db and c.shape[2] == win and win // dil == BAND
            args.append(c.reshape(c.shape[0] * db, BAND, dil, 2 * HEADS, HEAD_DIM))
            in_specs.append(pl.BlockSpec((side_rows, BAND, None, 2 * HEADS, HEAD_DIM),
                                         lambda i: (layer * steps + i, 0, 0, 0, 0)))
        args.append(state.reshape((state.shape[0] * db,) + state.shape[2:]))
        in_specs.append(pl.BlockSpec(st_block, lambda i: (layer * steps + i, 0, 0, 0)))
        out_shape += [jax.ShapeDtypeStruct((db, HEADS, HEAD_DIM), F32),
                      jax.ShapeDtypeStruct((db, 1, RET_V), F32),
                      jax.ShapeDtypeStruct((db, RET_HEADS, RET_DK, RET_DV), F32)]
        out_specs += [pl.BlockSpec((side_rows, HEADS, HEAD_DIM), lambda i: (i, 0, 0)),
                      pl.BlockSpec((side_rows, 1, RET_V), lambda i: (i, 0, 0)),
                      pl.BlockSpec(st_block, lambda i: (i, 0, 0, 0))]
    outs = pl.pallas_call(
        functools.partial(_mlp_kernel, side_rows=side_rows),
        grid=(steps,),
        in_specs=in_specs,
        out_specs=out_specs,
        out_shape=out_shape,
        compiler_params=_cparams(1, 48),
        name="mlp",
    )(*args)
    if side is None:
        return outs[0]
    y, att, ret, state_new = outs
    return y, att, ret.reshape(db, RET_V), state_new


def _rope_tables(pos):
    half = RET_DK // 2
    inv = ROPE_BASE ** (-jnp.arange(half, dtype=F32) / half)
    ang = pos.astype(F32)[:, None] * inv[None, :]
    cos, sin = jnp.cos(ang), jnp.sin(ang)
    return jnp.concatenate([cos, cos], axis=-1), jnp.concatenate([-sin, sin], axis=-1)


IN_PROJ_TM = 256
MERGE_TM = 512
MLP_TM = 256
RET_ROWS = 1024


def _merge_lw(att, ret, ga, gr, x2d, lw, tm):
    return _merge(att, ret, ga, gr, x2d, lw["w_att_br"], lw["w_ret_br"], lw["w_out"],
                  lw["g_post_mix"], lw["g_pre_mlp"], tm)


def _layer(xp, xs, caches, state, layer, lw, log_g):
    batch, seq, _ = xp.shape
    db, t, _ = xs.shape
    assert t == 1
    xp2d, xs2d = xp.reshape(batch * seq, D_MODEL), xs.reshape(db, D_MODEL)

    cos_p, sin_p = _rope_tables(jnp.arange(seq, dtype=jnp.int32))
    outs = _in_proj_prompt(xp2d, lw["g_pre_mix"], lw["w_in"], cos_p, sin_p, batch, seq, IN_PROJ_TM)
    qkv = outs[:9]
    rows_p = [kv.reshape(batch, -1, 2, HEADS, HEAD_DIM) for kv in outs[9:12]]
    rq, rk, rv, gsw, ga, gr = outs[12:]
    att = _attn_prompt(qkv, batch, seq)
    ret, state_p = _ret_prompt(log_g, rq, rk, rv, gsw, lw["g_ret_norm"], batch, seq, RET_ROWS)
    x1, h2 = _merge_lw(att, ret, ga, gr, xp2d, lw, MERGE_TM)

    cos_s, sin_s = _rope_tables(PAST_LEN + jnp.arange(1, dtype=jnp.int32))
    z = _in_proj_sample(xs2d, lw["g_pre_mix"], lw["w_in"])
    yp, att_s, ret_s, state_s = _mlp(h2, x1, lw["w_up"], lw["w_down"], lw["g_post_mlp"], MLP_TM,
                                     side=(log_g, z, cos_s, sin_s, lw["g_ret_norm"], caches, state, layer))

    att_s = jnp.transpose(att_s, (1, 0, 2)).reshape(1, HEADS, 1, db, HEAD_DIM)
    ret_s = jnp.transpose(ret_s.reshape(db, RET_HEADS, RET_DV), (1, 0, 2))[None]
    ga_s = z[:, COL_GA * TN:COL_GA * TN + D_MODEL]
    gr_s = z[:, COL_GR * TN:COL_GR * TN + D_MODEL]
    x1_s, h2_s = _merge_lw(att_s, ret_s, ga_s, gr_s, xs2d, lw, db)
    ys = _mlp(h2_s, x1_s, lw["w_up"], lw["w_down"], lw["g_post_mlp"], db)
    rows_s = []
    for g in range(N_GROUPS):
        k_new = z[:, (COL_K + g) * TN:(COL_K + g + 1) * TN]
        v_new = z[:, (COL_V + g) * TN:(COL_V + g + 1) * TN]
        rows_s.append(jnp.stack([k_new, v_new], axis=1).reshape(db, 1, 2, HEADS, HEAD_DIM))
    return (yp.reshape(batch, seq, D_MODEL), rows_p, state_p), (ys.reshape(db, 1, D_MODEL), rows_s, state_s)


def _stack(xs):
    return xs[0][None] if len(xs) == 1 else jnp.stack(xs, axis=0)


def kernel(x_prompt, x_sample, cache_kv_d1, cache_kv_d4, cache_kv_d16, state_ret, w_in, w_att_br, w_ret_br, w_out, w_up, w_down, g_ret_norm, g_pre_mix, g_post_mix, g_pre_mlp, g_post_mlp):
    depth = w_in.shape[0]
    log_g = jnp.log1p(-jnp.power(2.0, -5.0 - jnp.arange(RET_HEADS, dtype=F32)))
    caches = (cache_kv_d1, cache_kv_d4, cache_kv_d16)
    xp, xs = x_prompt, x_sample
    p_rows, s_rows = [[], [], []], [[], [], []]
    p_states, s_states = [], []
    for l in range(depth):
        lw = {
            "w_in": w_in[l].astype(BF16), "w_att_br": w_att_br[l].astype(BF16),
            "w_ret_br": w_ret_br[l].astype(BF16), "w_out": w_out[l].astype(BF16),
            "w_up": w_up[l].astype(BF16), "w_down": w_down[l].astype(BF16),
            "g_ret_norm": g_ret_norm[l].reshape(1, RET_V), "g_pre_mix": g_pre_mix[l].reshape(1, D_MODEL),
            "g_post_mix": g_post_mix[l].reshape(1, D_MODEL), "g_pre_mlp": g_pre_mlp[l].reshape(1, D_MODEL),
            "g_post_mlp": g_post_mlp[l].reshape(1, D_MODEL),
        }
        (xp, rows_p, sp), (xs, rows_s, ss) = _layer(xp, xs, caches, state_ret, l, lw, log_g)
        for g in range(N_GROUPS):
            p_rows[g].append(rows_p[g])
            s_rows[g].append(rows_s[g])
        p_states.append(sp)
        s_states.append(ss)
    return (xp, xs, _stack(p_rows[0]), _stack(p_rows[1]), _stack(p_rows[2]), _stack(p_states),
            _stack(s_rows[0]), _stack(s_rows[1]), _stack(s_rows[2]), _stack(s_states))
```

```python
import functools

import jax
import jax.numpy as jnp
from jax import lax
from jax.experimental import pallas as pl
from jax.experimental.pallas import tpu as pltpu

F32 = jnp.float32
BF16 = jnp.bfloat16

D_MODEL = 1024
PAST_LEN = 8192

DIL_GROUPS = ((128, 1), (512, 4), (2048, 16))
N_GROUPS = 3
HEADS = 4
HEAD_DIM = 128
ATT_OUT = HEADS * HEAD_DIM
ATT_COLS = N_GROUPS * ATT_OUT
BAND = 128

RET_HEADS = 4
RET_DK = 128
RET_DV = 256
RET_QK = RET_HEADS * RET_DK
RET_V = RET_HEADS * RET_DV
RET_CHUNK = 128
ROPE_BASE = 10000.0

D_FF = 4 * D_MODEL
IN_COLS = 3 * ATT_COLS + 2 * RET_QK + 2 * RET_V + 2 * D_MODEL
EPS = 1e-6
NEG_INF = -1e30

TN = 512
N_COL_TILES = IN_COLS // TN
COL_Q, COL_K, COL_V, COL_RQ, COL_RK, COL_RV, COL_GSW, COL_GA, COL_GR = 0, 3, 6, 9, 10, 11, 13, 15, 17

V7X_VMEM_BYTES = 64 * 1024 * 1024
MIB = 1024 * 1024


def _cparams(n_axes, vmem_mib):
    assert vmem_mib * MIB < V7X_VMEM_BYTES
    return pltpu.CompilerParams(
        dimension_semantics=("arbitrary",) * n_axes,
        vmem_limit_bytes=vmem_mib * MIB,
    )


def _resident(shape, index_map):
    return pl.BlockSpec(shape, index_map, pipeline_mode=pl.Buffered(1))


def _rms(x):
    return x * lax.rsqrt(jnp.mean(x * x, axis=-1, keepdims=True) + EPS)


def _sigmoid(x):
    return 1.0 / (1.0 + jnp.exp(-x))


def _silu(x):
    return x * _sigmoid(x)


def _dot(a, b):
    return jnp.dot(a, b, preferred_element_type=F32)


def _dot_nt(a, b):
    return lax.dot_general(a, b, (((1,), (1,)), ((), ())), preferred_element_type=F32)


def _rope(x, cos, sin):
    return x * cos + pltpu.roll(x, RET_DK // 2, 1) * sin


def _in_proj_prompt_kernel(x_ref, g_ref, w_ref, cos_ref, sin_ref,
                           q0, q1, q2, k0, k1, k2, v0, v1, v2, kvo0, kvo1, kvo2,
                           rq_ref, rk_ref, rv_ref, gsw_ref, ga_ref, gr_ref, scr_q, scr_k, scr_v):
    tm = x_ref.shape[0]
    h = (_rms(x_ref[...]) * g_ref[...]).astype(BF16)

    def tile(j):
        return _dot(h, w_ref[:, j * TN:(j + 1) * TN])

    def head(acc, hh, width=HEAD_DIM):
        return acc[:, hh * width:(hh + 1) * width]

    def scatter_residues(acc, scr, dst, dil):
        if dil == 1:
            for hh in range(HEADS):
                dst[hh, 0] = head(acc, hh).astype(BF16)
            return
        for hh in range(HEADS):
            scr[hh] = head(acc, hh)
        for hh in range(HEADS):
            for c in range(dil):
                dst[hh, c] = scr[hh, pl.ds(c, tm // dil, stride=dil), :].astype(BF16)

    def kv_rows(acc, kvo, which):
        keep = kvo.shape[0] // (2 * HEADS)
        for hh in range(HEADS):
            kvo[pl.ds(which * HEADS + hh, keep, stride=2 * HEADS), :] = head(acc, hh)[tm - keep:, :]

    def att_q(g):
        scatter_residues(tile(COL_Q + g) * (HEAD_DIM ** -0.5), scr_q, (q0, q1, q2)[g], DIL_GROUPS[g][1])

    def att_k(g):
        acc = tile(COL_K + g)
        scatter_residues(acc, scr_k, (k0, k1, k2)[g], DIL_GROUPS[g][1])
        kv_rows(acc, (kvo0, kvo1, kvo2)[g], 0)

    def att_v(g):
        acc = tile(COL_V + g)
        scatter_residues(acc, scr_v, (v0, v1, v2)[g], DIL_GROUPS[g][1])
        kv_rows(acc, (kvo0, kvo1, kvo2)[g], 1)

    def ret_qk(col, dst, scale):
        acc = tile(col)
        cos, sin = cos_ref[...], sin_ref[...]
        for hh in range(RET_HEADS):
            r = _rope(head(acc, hh), cos, sin)
            dst[hh] = r if scale is None else r * scale

    def ret_wide(col, dst, e, act=None):
        acc = tile(col + e)
        for s in range(2):
            part = head(acc, s, RET_DV)
            dst[2 * e + s] = (part if act is None else act(part)).astype(dst.dtype)

    def gate(col, dst, e):
        dst[:, e * TN:(e + 1) * TN] = tile(col + e)

    for g in range(N_GROUPS):
        att_q(g)
        att_k(g)
        att_v(g)
    ret_qk(COL_RQ, rq_ref, None)
    ret_qk(COL_RK, rk_ref, RET_DK ** -0.5)
    for e in range(2):
        ret_wide(COL_RV, rv_ref, e)
        ret_wide(COL_GSW, gsw_ref, e, _silu)
        gate(COL_GA, ga_ref, e)
        gate(COL_GR, gr_ref, e)


def _in_proj_prompt(x2d, g, w_bf, cos_t, sin_t, batch, seq, tm):
    assert seq % tm == 0 and tm % (16 * DIL_GROUPS[-1][1]) == 0
    tps = seq // tm
    out_shape, out_specs = [], []
    for _ in range(3):
        for _, dil in DIL_GROUPS:
            out_shape.append(jax.ShapeDtypeStruct((batch, HEADS, dil, seq // dil, HEAD_DIM), BF16))
            out_specs.append(pl.BlockSpec((None, HEADS, dil, tm // dil, HEAD_DIM), lambda b, t: (b, 0, 0, t, 0)))
    for win, _ in DIL_GROUPS:
        keep = min(win, seq)
        rows = min(keep, tm)
        assert keep % rows == 0
        first = (seq - keep) // rows
        out_shape.append(jax.ShapeDtypeStruct((batch, keep * 2 * HEADS, HEAD_DIM), F32))
        out_specs.append(pl.BlockSpec(
            (None, rows * 2 * HEADS, HEAD_DIM),
            functools.partial(lambda b, t, first, per: (b, jnp.maximum((t + 1) * per - 1 - first, 0), 0),
                              first=first, per=tm // rows)))
    hm = lambda width, dt: (jax.ShapeDtypeStruct((batch, RET_HEADS, seq, width), dt),
                            pl.BlockSpec((None, RET_HEADS, tm, width), lambda b, t: (b, 0, t, 0)))
    nat = (jax.ShapeDtypeStruct((batch * seq, D_MODEL), F32),
           pl.BlockSpec((tm, D_MODEL), lambda b, t: (b * tps + t, 0)))
    for shp, spec in (hm(RET_DK, F32), hm(RET_DK, F32), hm(RET_DV, BF16), hm(RET_DV, F32), nat, nat):
        out_shape.append(shp)
        out_specs.append(spec)
    return pl.pallas_call(
        _in_proj_prompt_kernel,
        grid=(batch, tps),
        in_specs=[
            pl.BlockSpec((tm, D_MODEL), lambda b, t: (b * tps + t, 0)),
            _resident((1, D_MODEL), lambda b, t: (0, 0)),
            _resident((D_MODEL, IN_COLS), lambda b, t: (0, 0)),
            pl.BlockSpec((tm, RET_DK), lambda b, t: (t, 0)),
            pl.BlockSpec((tm, RET_DK), lambda b, t: (t, 0)),
        ],
        out_specs=out_specs,
        out_shape=out_shape,
        scratch_shapes=[pltpu.VMEM((HEADS, tm, HEAD_DIM), F32)] * 3,
        compiler_params=_cparams(2, 56),
        name="in_proj_prompt",
    )(x2d, g, w_bf, cos_t, sin_t)


def _in_proj_sample_kernel(x_ref, g_ref, w_ref, z_ref, h_ref):
    @pl.when(pl.program_id(0) == 0)
    def _():
        h_ref[...] = (_rms(x_ref[...]) * g_ref[...]).astype(BF16)

    z_ref[...] = _dot(h_ref[...], w_ref[...])


def _in_proj_sample(x2d, g, w_bf):
    m = x2d.shape[0]
    return pl.pallas_call(
        _in_proj_sample_kernel,
        grid=(N_COL_TILES,),
        in_specs=[
            _resident((m, D_MODEL), lambda j: (0, 0)),
            _resident((1, D_MODEL), lambda j: (0, 0)),
            pl.BlockSpec((D_MODEL, TN), lambda j: (0, j)),
        ],
        out_specs=pl.BlockSpec((m, TN), lambda j: (0, j)),
        out_shape=jax.ShapeDtypeStruct((m, IN_COLS), F32),
        scratch_shapes=[pltpu.VMEM((m, D_MODEL), BF16)],
        compiler_params=_cparams(1, 16),
        name="in_proj_sample",
    )(x2d, g, w_bf)


def _softmax_block(s, v):
    m = jnp.max(s, axis=1, keepdims=True)
    p = jnp.exp(s - m).astype(BF16)
    uv = _dot(p, jnp.concatenate([v, jnp.ones_like(v)], axis=1))
    return uv[:, :HEAD_DIM], jnp.broadcast_to(m, (s.shape[0], HEAD_DIM)), uv[:, HEAD_DIM:]


ATT_BLOCKS_PER_STEP = 8
ATT_ORDER = 4


def _attn_prompt_kernel(q0, q1, q2, k0, k1, k2, v0, v1, v2, o_ref,
                        u0, u1, u2, m0, m1, m2, l0, l1, l2, *, seq):
    qs, ks, vs = (q0, q1, q2), (k0, k1, k2), (v0, v1, v2)
    us, ms, ls = (u0, u1, u2), (m0, m1, m2), (l0, l1, l2)
    order = ATT_ORDER
    n_out = seq // order
    row1 = lax.broadcasted_iota(jnp.int32, (BAND, BAND), 0)
    col1 = lax.broadcasted_iota(jnp.int32, (BAND, BAND), 1)
    causal = col1 <= row1
    row2 = lax.broadcasted_iota(jnp.int32, (BAND, 2 * BAND), 0)
    col2 = lax.broadcasted_iota(jnp.int32, (BAND, 2 * BAND), 1)
    band = (col2 >= row2) & (col2 <= row2 + BAND)

    aligned = lambda r: r if isinstance(r, int) else pl.multiple_of(r, BAND)

    for g, (_, dil) in enumerate(DIL_GROUPS):
        n = seq // dil
        nb = n // BAND

        def run_blocks(blocks, g=g, n=n, dil=dil):
            scores, values = [], []
            for c, blk, has_prev in blocks:
                own = pl.ds(aligned(blk * BAND), BAND)
                keys = pl.ds(aligned((blk - 1) * BAND), 2 * BAND) if has_prev else own
                mask = band if has_prev else causal
                scores.append(jnp.where(mask, _dot_nt(qs[g][c, own, :], ks[g][c, keys, :]), NEG_INF))
                values.append(vs[g][c, keys, :])
            results = [_softmax_block(s, v) for s, v in zip(scores, values)]
            for (c, blk, _), (u, m, l) in zip(blocks, results):
                if dil <= order:
                    rs = pl.ds(aligned(c * n + blk * BAND), BAND)
                else:
                    step = dil // order
                    rs = pl.ds((c % order) * n_out + c // order + blk * (BAND * step), BAND, stride=step)
                us[g][rs, :] = u
                ms[g][rs, :] = m
                ls[g][rs, :] = l

        per = ATT_BLOCKS_PER_STEP
        assert (dil * nb) % per == 0 and (per % nb == 0 or nb % per == 0)
        if nb >= per:
            def body(i, carry, run_blocks=run_blocks, steps=nb // per):
                c, i0 = i // steps, (i % steps) * per
                run_blocks([(c, i0 + e, True) for e in range(per)])
                return carry
            for c in range(dil):
                run_blocks([(c, e, e > 0) for e in range(per)])
                lax.fori_loop(c * (nb // per) + 1, (c + 1) * (nb // per), body, 0)
        else:
            def body(i, carry, run_blocks=run_blocks, nb=nb, cps=per // nb):
                run_blocks([(i * cps + e // nb, e % nb, e % nb > 0) for e in range(per)])
                return carry
            lax.fori_loop(0, dil * nb // per, body, 0)

    chunks = n_out // BAND

    def combine_one(j):
        c, t = j // chunks, j % chunks
        rows = []
        for _, dil in DIL_GROUPS:
            r = min(dil, order)
            step = order // r
            start = (c % r) * (seq // r) + c // r + t * (BAND * step)
            rows.append(pl.ds(aligned(start), BAND) if step == 1 else pl.ds(start, BAND, stride=step))
        m_g = [ms[g][rows[g], :] for g in range(N_GROUPS)]
        m = jnp.maximum(jnp.maximum(m_g[0], m_g[1]), m_g[2])
        num = jnp.zeros((BAND, HEAD_DIM), F32)
        den = jnp.zeros((BAND, HEAD_DIM), F32)
        for g in range(N_GROUPS):
            w = jnp.exp(m_g[g] - m)
            num = num + w * us[g][rows[g], :]
            den = den + w * ls[g][rows[g], :]
        return num / den

    per = 4
    assert (order * chunks) % per == 0

    def combine(i, carry):
        outs = [combine_one(i * per + e) for e in range(per)]
        for e, o in enumerate(outs):
            o_ref[pl.ds(aligned((i * per + e) * BAND), BAND), :] = o
        return carry

    lax.fori_loop(0, order * chunks // per, combine, 0)


def _attn_prompt(qkv, batch, seq):
    assert seq % (BAND * DIL_GROUPS[-1][1]) == 0
    assert all(ATT_ORDER % d == 0 or d % ATT_ORDER == 0 for _, d in DIL_GROUPS)
    in_specs = [pl.BlockSpec((None, None) + a.shape[2:], lambda b, h: (b, h, 0, 0, 0)) for a in qkv]
    out = pl.pallas_call(
        functools.partial(_attn_prompt_kernel, seq=seq),
        grid=(batch, HEADS),
        in_specs=in_specs,
        out_specs=pl.BlockSpec((None, None, seq, HEAD_DIM), lambda b, h: (b, h, 0, 0)),
        out_shape=jax.ShapeDtypeStruct((batch, HEADS, seq, HEAD_DIM), F32),
        scratch_shapes=[pltpu.VMEM((seq, HEAD_DIM), F32)] * 9,
        compiler_params=_cparams(2, 32),
        name="attend_prompt",
    )(*qkv)
    return out.reshape(batch, HEADS, ATT_ORDER, seq // ATT_ORDER, HEAD_DIM)


def _group_norm_gate(o, gn, gate):
    mu = jnp.mean(o, axis=-1, keepdims=True)
    d = o - mu
    var = jnp.mean(d * d, axis=-1, keepdims=True)
    return gate * (d * lax.rsqrt(var + EPS) * gn)


RET_CHUNKS_PER_STEP = 2


def _ret_prompt_kernel(lg_ref, q_ref, k_ref, v_ref, gate_ref, gn_ref, o_ref, s_out_ref, s_scr):
    c = RET_CHUNK
    rows = q_ref.shape[1]
    t = lax.broadcasted_iota(jnp.int32, (c, 1), 0).astype(F32)
    rel = (lax.broadcasted_iota(jnp.int32, (c, c), 0) - lax.broadcasted_iota(jnp.int32, (c, c), 1)).astype(F32)

    @pl.when(pl.program_id(1) == 0)
    def _():
        s_scr[...] = jnp.zeros_like(s_scr)

    consts = []
    for h in range(RET_HEADS):
        lg = lg_ref[h]
        consts.append((
            jnp.where(rel >= 0, jnp.exp(lg * jnp.maximum(rel, 0.0)), 0.0),
            jnp.exp(lg * (t + 1.0)),
            jnp.exp(lg * (c - 1.0 - t)),
            jnp.exp(lg * jnp.full((1, RET_DV), float(c), F32)),
        ))

    per = RET_CHUNKS_PER_STEP
    assert (rows // c) % per == 0

    def step(i, carry):
        rss = [pl.ds(pl.multiple_of((i * per + e) * c, c), c) for e in range(per)]
        local = {}
        for h in range(RET_HEADS):
            decay, inner, tail, _ = consts[h]
            for e, rs in enumerate(rss):
                q, k, vb = q_ref[h, rs, :], k_ref[h, rs, :], v_ref[h, rs, :]
                a = _dot_nt(q.astype(BF16), k.astype(BF16)) * decay
                local[h, e] = (_dot(a.astype(BF16), vb),
                               (q * inner).astype(BF16),
                               _dot((k * tail).T.astype(BF16), vb))
        outs = {}
        for h in range(RET_HEADS):
            s = s_scr[h]
            for e in range(per):
                intra, q_in, kv = local[h, e]
                outs[h, e] = intra + _dot(q_in, s.astype(BF16))
                s = consts[h][3] * s + kv
            s_scr[h] = s
        for h in range(RET_HEADS):
            gn = gn_ref[:, h * RET_DV:(h + 1) * RET_DV]
            for e, rs in enumerate(rss):
                o_ref[h, rs, :] = _group_norm_gate(outs[h, e], gn, gate_ref[h, rs, :]).astype(o_ref.dtype)
        return carry

    lax.fori_loop(0, rows // (c * per), step, 0)

    @pl.when(pl.program_id(1) == pl.num_programs(1) - 1)
    def _():
        s_out_ref[...] = s_scr[...]


def _ret_prompt(log_g, rq, rk, rv, gsw, gn, batch, seq, rows):
    assert seq % rows == 0 and rows % RET_CHUNK == 0
    hm = lambda width: pl.BlockSpec((None, RET_HEADS, rows, width), lambda b, t: (b, 0, t, 0))
    return pl.pallas_call(
        _ret_prompt_kernel,
        grid=(batch, seq // rows),
        in_specs=[
            pl.BlockSpec(memory_space=pltpu.SMEM),
            hm(RET_DK), hm(RET_DK), hm(RET_DV), hm(RET_DV),
            _resident((1, RET_V), lambda b, t: (0, 0)),
        ],
        out_specs=[
            hm(RET_DV),
            pl.BlockSpec((None, RET_HEADS, RET_DK, RET_DV), lambda b, t: (b, 0, 0, 0)),
        ],
        out_shape=[
            jax.ShapeDtypeStruct((batch, RET_HEADS, seq, RET_DV), BF16),
            jax.ShapeDtypeStruct((batch, RET_HEADS, RET_DK, RET_DV), F32),
        ],
        scratch_shapes=[pltpu.VMEM((RET_HEADS, RET_DK, RET_DV), F32)],
        compiler_params=_cparams(2, 40),
        name="retain_prompt",
    )(log_g, rq, rk, rv, gsw, gn)


def _sample_attend_row(z_ref, caches, r):
    half = BAND // 2

    def heads(tile0, g):
        base = (tile0 + g) * TN
        return jnp.concatenate(
            [z_ref[r, :, base + hh * HEAD_DIM:base + (hh + 1) * HEAD_DIM] for hh in range(HEADS)], axis=0)

    def paired(ref, lo):
        return jnp.concatenate([ref[r, 0:half, lo:lo + HEADS, :], ref[r, half:BAND, lo:lo + HEADS, :]], axis=1)

    twice = lambda a: jnp.concatenate([a, a], axis=0)
    fold = lambda a, op: op(a[0:HEADS], a[HEADS:2 * HEADS])

    us, ms, ls = [], [], []
    for g in range(N_GROUPS):
        q = heads(COL_Q, g) * (HEAD_DIM ** -0.5)
        k_new, v_new = heads(COL_K, g), heads(COL_V, g)
        k_buf, v_buf = paired(caches[g], 0), paired(caches[g], HEADS)
        s_buf = jnp.sum(k_buf * twice(q)[None], axis=-1, keepdims=True)
        s_new = jnp.sum(k_new * q, axis=-1, keepdims=True)
        m = jnp.maximum(fold(jnp.max(s_buf, axis=0), jnp.maximum), s_new)
        p_buf = jnp.exp(s_buf - twice(m)[None])
        p_new = jnp.exp(s_new - m)
        ls.append(fold(jnp.sum(p_buf, axis=0), jnp.add) + p_new)
        us.append(fold(jnp.sum(p_buf * v_buf, axis=0), jnp.add) + p_new * v_new)
        ms.append(m)
    m = jnp.maximum(jnp.maximum(ms[0], ms[1]), ms[2])
    num = jnp.zeros((HEADS, HEAD_DIM), F32)
    den = jnp.zeros((HEADS, 1), F32)
    for g in range(N_GROUPS):
        w = jnp.exp(ms[g] - m)
        num = num + w * us[g]
        den = den + w * ls[g]
    return num / den


def _sample_retain_head(lg_ref, z_ref, cos, sin, gn_ref, s_ref, ret_ref, s_out_ref, h):
    rows = z_ref.shape[0]
    eye = (lax.broadcasted_iota(jnp.int32, (RET_DK, RET_DK), 0)
           == lax.broadcasted_iota(jnp.int32, (RET_DK, RET_DK), 1))

    def column(v):
        return jnp.sum(jnp.where(eye, jnp.broadcast_to(v, (RET_DK, RET_DK)), 0.0), axis=1, keepdims=True)

    def zcols(tile0, width):
        return z_ref[:, 0, tile0 * TN + h * width:tile0 * TN + (h + 1) * width]

    gamma = jnp.exp(lg_ref[h] * jnp.ones((1, RET_DV), F32))
    q_h = _rope(zcols(COL_RQ, RET_DK), cos, sin)
    k_h = _rope(zcols(COL_RK, RET_DK), cos, sin) * (RET_DK ** -0.5)
    v_h = zcols(COL_RV, RET_DV)
    outs = []
    for r in range(rows):
        s_new = gamma * s_ref[r, h] + column(k_h[r:r + 1, :]) * v_h[r:r + 1, :]
        s_out_ref[r, h] = s_new
        outs.append(jnp.sum(column(q_h[r:r + 1, :]) * s_new, axis=0, keepdims=True))
    seg = slice(h * RET_DV, (h + 1) * RET_DV)
    ret_ref[:, 0, seg] = _group_norm_gate(jnp.concatenate(outs, axis=0), gn_ref[:, seg], _silu(zcols(COL_GSW, RET_DV)))


FF_CHUNK = 1024


def _tail_kernel(*refs, side_rows):
    (att_ref, ret_ref, ga_ref, gr_ref, x_ref, wa_ref, wr_ref, wo_ref, gpost_mix_ref, gpre_mlp_ref,
     wu_ref, wd_ref, gpost_mlp_ref) = refs[:13]
    att_scr = refs[-1]
    side = []
    if side_rows:
        lg_ref, z_ref, cos_ref, sin_ref, gn_ref, c0, c1, c2, s_ref, y_ref, att_s_ref, ret_s_ref, s_out_ref = refs[13:-1]

        def attend(r):
            att_s_ref[r] = _sample_attend_row(z_ref, (c0, c1, c2), r)

        def retain(h):
            _sample_retain_head(lg_ref, z_ref, cos_ref[...], sin_ref[...], gn_ref, s_ref, ret_s_ref, s_out_ref, h)

        side = [functools.partial(attend, r) for r in range(side_rows)]
        side += [functools.partial(retain, h) for h in range(RET_HEADS)]
    else:
        y_ref, = refs[13:-1]

    r = att_ref.shape[1]
    for hh in range(HEADS):
        for c in range(r):
            rs = pl.ds(c, att_ref.shape[2], stride=r) if r > 1 else slice(None)
            att_scr[hh, rs, :] = att_ref[hh, c]
    att = jnp.concatenate([att_scr[hh].astype(BF16) for hh in range(HEADS)], axis=1)
    ret = jnp.concatenate([ret_ref[hh].astype(BF16) for hh in range(RET_HEADS)], axis=1)
    m = _sigmoid(ga_ref[...]) * _dot(att, wa_ref[...]) + _sigmoid(gr_ref[...]) * _dot(ret, wr_ref[...])
    x1 = x_ref[...] + _rms(_dot(m.astype(BF16), wo_ref[...])) * gpost_mix_ref[...]
    h = (_rms(x1) * gpre_mlp_ref[...]).astype(BF16)

    n_chunks = D_FF // FF_CHUNK
    acc = jnp.zeros(x_ref.shape, F32)
    for c in range(n_chunks):
        cs = slice(c * FF_CHUNK, (c + 1) * FF_CHUNK)
        u = jnp.maximum(_dot(h, wu_ref[:, cs]), 0.0)
        acc = acc + _dot((u * u).astype(BF16), wd_ref[cs, :])
        for work in side[c::n_chunks]:
            work()
    y_ref[...] = x1 + _rms(acc) * gpost_mlp_ref[...]


def _tail(att, ret, ga, gr, x2d, lw, tm, side=None):
    batch, _, r, ni, _ = att.shape
    seq = r * ni
    m = batch * seq
    assert seq % tm == 0 and tm % r == 0 and x2d.shape[0] == m
    tps = seq // tm
    steps = batch * tps
    rows = pl.BlockSpec((tm, D_MODEL), lambda i: (i, 0))
    full = lambda a: _resident(a.shape, lambda i: (0,) * a.ndim)
    weights = [lw[k] for k in ("w_att_br", "w_ret_br", "w_out", "g_post_mix", "g_pre_mlp",
                               "w_up", "w_down", "g_post_mlp")]
    args = [att, ret, ga, gr, x2d] + weights
    in_specs = [pl.BlockSpec((None, HEADS, r, tm // r, HEAD_DIM), lambda i: (i // tps, 0, 0, i % tps, 0)),
                pl.BlockSpec((None, RET_HEADS, tm, RET_DV), lambda i: (i // tps, 0, i % tps, 0)),
                rows, rows, rows] + [full(w) for w in weights]
    out_shape = [jax.ShapeDtypeStruct((m, D_MODEL), F32)]
    out_specs = [rows]
    side_rows = 0
    if side is not None:
        log_g, z, cos_t, sin_t, gn, caches, state, layer = side
        db = z.shape[0]
        assert db % steps == 0
        side_rows = db // steps
        st_block = (side_rows, RET_HEADS, RET_DK, RET_DV)
        args += [log_g, z.reshape(db, 1, IN_COLS), cos_t, sin_t, gn]
        in_specs += [pl.BlockSpec(memory_space=pltpu.SMEM),
                     pl.BlockSpec((side_rows, 1, IN_COLS), lambda i: (i, 0, 0)),
                     full(cos_t), full(sin_t), full(gn)]
        for (win, dil), c in zip(DIL_GROUPS, caches):
            assert c.shape[1] == db and c.shape[2] == win and win // dil == BAND
            args.append(c.reshape(c.shape[0] * db, BAND, dil, 2 * HEADS, HEAD_DIM))
            in_specs.append(pl.BlockSpec((side_rows, BAND, None, 2 * HEADS, HEAD_DIM),
                                         lambda i: (layer * steps + i, 0, 0, 0, 0)))
        args.append(state.reshape((state.shape[0] * db,) + state.shape[2:]))
        in_specs.append(pl.BlockSpec(st_block, lambda i: (layer * steps + i, 0, 0, 0)))
        out_shape += [jax.ShapeDtypeStruct((db, HEADS, HEAD_DIM), F32),
                      jax.ShapeDtypeStruct((db, 1, RET_V), F32),
                      jax.ShapeDtypeStruct((db, RET_HEADS, RET_DK, RET_DV), F32)]
        out_specs += [pl.BlockSpec((side_rows, HEADS, HEAD_DIM), lambda i: (i, 0, 0)),
                      pl.BlockSpec((side_rows, 1, RET_V), lambda i: (i, 0, 0)),
                      pl.BlockSpec(st_block, lambda i: (i, 0, 0, 0))]
    outs = pl.pallas_call(
        functools.partial(_tail_kernel, side_rows=side_rows),
        grid=(steps,),
        in_specs=in_specs,
        out_specs=out_specs,
        out_shape=out_shape,
        scratch_shapes=[pltpu.VMEM((HEADS, tm, HEAD_DIM), F32)],
        compiler_params=_cparams(1, 56),
        name="tail",
    )(*args)
    if side is None:
        return outs[0]
    y, att, ret, state_new = outs
    return y, att, ret.reshape(db, RET_V), state_new


def _rope_tables(pos):
    half = RET_DK // 2
    inv = ROPE_BASE ** (-jnp.arange(half, dtype=F32) / half)
    ang = pos.astype(F32)[:, None] * inv[None, :]
    cos, sin = jnp.cos(ang), jnp.sin(ang)
    return jnp.concatenate([cos, cos], axis=-1), jnp.concatenate([-sin, sin], axis=-1)


IN_PROJ_TM = 256
TAIL_TM = 256
RET_ROWS = 1024


def _layer(xp, xs, caches, state, layer, lw, log_g):
    batch, seq, _ = xp.shape
    db, t, _ = xs.shape
    assert t == 1
    xp2d, xs2d = xp.reshape(batch * seq, D_MODEL), xs.reshape(db, D_MODEL)

    cos_p, sin_p = _rope_tables(jnp.arange(seq, dtype=jnp.int32))
    outs = _in_proj_prompt(xp2d, lw["g_pre_mix"], lw["w_in"], cos_p, sin_p, batch, seq, IN_PROJ_TM)
    qkv = outs[:9]
    rows_p = [kv.reshape(batch, -1, 2, HEADS, HEAD_DIM) for kv in outs[9:12]]
    rq, rk, rv, gsw, ga, gr = outs[12:]
    att = _attn_prompt(qkv, batch, seq)
    ret, state_p = _ret_prompt(log_g, rq, rk, rv, gsw, lw["g_ret_norm"], batch, seq, RET_ROWS)

    cos_s, sin_s = _rope_tables(PAST_LEN + jnp.arange(1, dtype=jnp.int32))
    z = _in_proj_sample(xs2d, lw["g_pre_mix"], lw["w_in"])
    yp, att_s, ret_s, state_s = _tail(att, ret, ga, gr, xp2d, lw, TAIL_TM,
                                      side=(log_g, z, cos_s, sin_s, lw["g_ret_norm"], caches, state, layer))

    att_s = jnp.transpose(att_s, (1, 0, 2)).reshape(1, HEADS, 1, db, HEAD_DIM)
    ret_s = jnp.transpose(ret_s.reshape(db, RET_HEADS, RET_DV), (1, 0, 2))[None]
    ga_s = z[:, COL_GA * TN:COL_GA * TN + D_MODEL]
    gr_s = z[:, COL_GR * TN:COL_GR * TN + D_MODEL]
    ys = _tail(att_s, ret_s, ga_s, gr_s, xs2d, lw, db)
    rows_s = []
    for g in range(N_GROUPS):
        k_new = z[:, (COL_K + g) * TN:(COL_K + g + 1) * TN]
        v_new = z[:, (COL_V + g) * TN:(COL_V + g + 1) * TN]
        rows_s.append(jnp.stack([k_new, v_new], axis=1).reshape(db, 1, 2, HEADS, HEAD_DIM))
    return (yp.reshape(batch, seq, D_MODEL), rows_p, state_p), (ys.reshape(db, 1, D_MODEL), rows_s, state_s)


def _stack(xs):
    return xs[0][None] if len(xs) == 1 else jnp.stack(xs, axis=0)


def kernel(x_prompt, x_sample, cache_kv_d1, cache_kv_d4, cache_kv_d16, state_ret, w_in, w_att_br, w_ret_br, w_out, w_up, w_down, g_ret_norm, g_pre_mix, g_post_mix, g_pre_mlp, g_post_mlp):
    depth = w_in.shape[0]
    log_g = jnp.log1p(-jnp.power(2.0, -5.0 - jnp.arange(RET_HEADS, dtype=F32)))
    caches = (cache_kv_d1, cache_kv_d4, cache_kv_d16)
    xp, xs = x_prompt, x_sample
    p_rows, s_rows = [[], [], []], [[], [], []]
    p_states, s_states = [], []
    for l in range(depth):
        lw = {
            "w_in": w_in[l].astype(BF16), "w_att_br": w_att_br[l].astype(BF16),
            "w_ret_br": w_ret_br[l].astype(BF16), "w_out": w_out[l].astype(BF16),
            "w_up": w_up[l].astype(BF16), "w_down": w_down[l].astype(BF16),
            "g_ret_norm": g_ret_norm[l].reshape(1, RET_V), "g_pre_mix": g_pre_mix[l].reshape(1, D_MODEL),
            "g_post_mix": g_post_mix[l].reshape(1, D_MODEL), "g_pre_mlp": g_pre_mlp[l].reshape(1, D_MODEL),
            "g_post_mlp": g_post_mlp[l].reshape(1, D_MODEL),
        }
        (xp, rows_p, sp), (xs, rows_s, ss) = _layer(xp, xs, caches, state_ret, l, lw, log_g)
        for g in range(N_GROUPS):
            p_rows[g].append(rows_p[g])
            s_rows[g].append(rows_s[g])
        p_states.append(sp)
        s_states.append(ss)
    return (xp, xs, _stack(p_rows[0]), _stack(p_rows[1]), _stack(p_rows[2]), _stack(p_states),
            _stack(s_rows[0]), _stack(s_rows[1]), _stack(s_rows[2]), _stack(s_states))
```

```python
import functools

import jax
import jax.numpy as jnp
from jax import lax
from jax.experimental import pallas as pl
from jax.experimental.pallas import tpu as pltpu

F32 = jnp.float32
BF16 = jnp.bfloat16

D_MODEL = 1024
PAST_LEN = 8192

DIL_GROUPS = ((128, 1), (512, 4), (2048, 16))
N_GROUPS = 3
HEADS = 4
HEAD_DIM = 128
ATT_OUT = HEADS * HEAD_DIM
ATT_COLS = N_GROUPS * ATT_OUT
BAND = 128

RET_HEADS = 4
RET_DK = 128
RET_DV = 256
RET_QK = RET_HEADS * RET_DK
RET_V = RET_HEADS * RET_DV
RET_CHUNK = 128
ROPE_BASE = 10000.0

D_FF = 4 * D_MODEL
IN_COLS = 3 * ATT_COLS + 2 * RET_QK + 2 * RET_V + 2 * D_MODEL
EPS = 1e-6
NEG_INF = -1e30

TN = 512
N_COL_TILES = IN_COLS // TN
COL_Q, COL_K, COL_V, COL_RQ, COL_RK, COL_RV, COL_GSW, COL_GA, COL_GR = 0, 3, 6, 9, 10, 11, 13, 15, 17

V7X_VMEM_BYTES = 64 * 1024 * 1024
MIB = 1024 * 1024


def _cparams(n_axes, vmem_mib):
    assert vmem_mib * MIB < V7X_VMEM_BYTES
    return pltpu.CompilerParams(
        dimension_semantics=("arbitrary",) * n_axes,
        vmem_limit_bytes=vmem_mib * MIB,
    )


def _resident(shape, index_map):
    return pl.BlockSpec(shape, index_map, pipeline_mode=pl.Buffered(1))


def _rms(x):
    return x * lax.rsqrt(jnp.mean(x * x, axis=-1, keepdims=True) + EPS)


def _sigmoid(x):
    return 1.0 / (1.0 + jnp.exp(-x))


def _silu(x):
    return x * _sigmoid(x)


def _dot(a, b):
    return jnp.dot(a, b, preferred_element_type=F32)


def _dot_nt(a, b):
    return lax.dot_general(a, b, (((1,), (1,)), ((), ())), preferred_element_type=F32)


def _rope(x, cos, sin):
    return x * cos + pltpu.roll(x, RET_DK // 2, 1) * sin


N_ATT_TILES = 3 * N_GROUPS


def _norm_rows(x_ref, g_ref):
    return (_rms(x_ref[...]) * g_ref[...]).astype(BF16)


def _head(acc, hh, width=HEAD_DIM):
    return acc[:, hh * width:(hh + 1) * width]


def _in_proj_att_kernel(x_ref, g_ref, w_ref, q0, q1, q2, k0, k1, k2, v0, v1, v2, kvo0, kvo1, kvo2,
                        scr_q, sk0, sk1, sk2, sv0, sv1, sv2, *, first_kept):
    tm = x_ref.shape[0]
    h = _norm_rows(x_ref, g_ref)

    def park(j, scr, scale=None):
        acc = _dot(h, w_ref[:, j * TN:(j + 1) * TN])
        for hh in range(HEADS):
            scr[hh] = _head(acc, hh) if scale is None else _head(acc, hh) * scale

    def residues(scr, dst, dil):
        for hh in range(HEADS):
            for c in range(dil):
                rows = pl.ds(c, tm // dil, stride=dil) if dil > 1 else slice(None)
                dst[hh, c] = scr[hh, rows, :].astype(BF16)

    for g, (_, dil) in enumerate(DIL_GROUPS):
        park(COL_Q + g, scr_q, HEAD_DIM ** -0.5)
        residues(scr_q, (q0, q1, q2)[g], dil)
        park(COL_K + g, (sk0, sk1, sk2)[g])
        residues((sk0, sk1, sk2)[g], (k0, k1, k2)[g], dil)
        park(COL_V + g, (sv0, sv1, sv2)[g])
        residues((sv0, sv1, sv2)[g], (v0, v1, v2)[g], dil)

    for g, kvo in enumerate((kvo0, kvo1, kvo2)):
        def write(g=g, kvo=kvo):
            keep = kvo.shape[0] // (2 * HEADS)
            for which, scr in enumerate(((sk0, sk1, sk2)[g], (sv0, sv1, sv2)[g])):
                for hh in range(HEADS):
                    kvo[pl.ds(which * HEADS + hh, keep, stride=2 * HEADS), :] = scr[hh, tm - keep:, :]
        if first_kept[g] == 0:
            write()
        else:
            pl.when(pl.program_id(1) >= first_kept[g])(write)


def _in_proj_att(x2d, g, w_att, batch, seq, tm):
    assert seq % tm == 0 and tm % (16 * DIL_GROUPS[-1][1]) == 0
    tps = seq // tm
    out_shape, out_specs, first_kept = [], [], []
    for _ in range(3):
        for _, dil in DIL_GROUPS:
            out_shape.append(jax.ShapeDtypeStruct((batch, HEADS, dil, seq // dil, HEAD_DIM), BF16))
            out_specs.append(pl.BlockSpec((None, HEADS, dil, tm // dil, HEAD_DIM), lambda b, t: (b, 0, 0, t, 0)))
    for win, _ in DIL_GROUPS:
        keep = min(win, seq)
        rows = min(keep, tm)
        assert keep % rows == 0
        first = (seq - keep) // rows
        first_kept.append((seq - keep) // tm)
        out_shape.append(jax.ShapeDtypeStruct((batch, keep * 2 * HEADS, HEAD_DIM), F32))
        out_specs.append(pl.BlockSpec(
            (None, rows * 2 * HEADS, HEAD_DIM),
            functools.partial(lambda b, t, first, per: (b, jnp.maximum((t + 1) * per - 1 - first, 0), 0),
                              first=first, per=tm // rows)))
    return pl.pallas_call(
        functools.partial(_in_proj_att_kernel, first_kept=tuple(first_kept)),
        grid=(batch, tps),
        in_specs=[
            pl.BlockSpec((tm, D_MODEL), lambda b, t: (b * tps + t, 0)),
            _resident((1, D_MODEL), lambda b, t: (0, 0)),
            _resident(w_att.shape, lambda b, t: (0, 0)),
        ],
        out_specs=out_specs,
        out_shape=out_shape,
        scratch_shapes=[pltpu.VMEM((HEADS, tm, HEAD_DIM), F32)] * (1 + 2 * N_GROUPS),
        compiler_params=_cparams(2, 48),
        name="in_proj_att",
    )(x2d, g, w_att)


def _in_proj_rest_kernel(x_ref, g_ref, w_ref, cos_ref, sin_ref, wu_ref, wd_ref,
                         rq_ref, rk_ref, rv_ref, gsw_ref, ga_ref, gr_ref, wu_bf_ref, wd_bf_ref):
    h = _norm_rows(x_ref, g_ref)

    def tile(j):
        lo = (j - N_ATT_TILES) * TN
        return _dot(h, w_ref[:, lo:lo + TN])

    def ret_qk(col, dst, scale):
        acc = tile(col)
        cos, sin = cos_ref[...], sin_ref[...]
        for hh in range(RET_HEADS):
            r = _rope(_head(acc, hh), cos, sin)
            dst[hh] = r if scale is None else r * scale

    def ret_wide(col, dst, e, act=None):
        acc = tile(col + e)
        for s in range(2):
            part = _head(acc, s, RET_DV)
            dst[2 * e + s] = (part if act is None else act(part)).astype(dst.dtype)

    def gate(col, dst, e):
        dst[:, e * TN:(e + 1) * TN] = tile(col + e)

    ret_qk(COL_RQ, rq_ref, None)
    wu_bf_ref[...] = wu_ref[...].astype(BF16)
    wd_bf_ref[...] = wd_ref[...].astype(BF16)
    ret_qk(COL_RK, rk_ref, RET_DK ** -0.5)
    for e in range(2):
        ret_wide(COL_RV, rv_ref, e)
        ret_wide(COL_GSW, gsw_ref, e, _silu)
        gate(COL_GA, ga_ref, e)
        gate(COL_GR, gr_ref, e)


def _in_proj_rest(x2d, g, w_rest, cos_t, sin_t, w_up, w_down, batch, seq, tm):
    assert seq % tm == 0
    tps = seq // tm
    steps = batch * tps
    cast_rows = [w.shape[0] // steps for w in (w_up, w_down)]
    assert all(w.shape[0] % steps == 0 and r % 16 == 0 for w, r in zip((w_up, w_down), cast_rows))
    cast_specs = [pl.BlockSpec((r, w.shape[1]), lambda b, t: (b * tps + t, 0))
                  for w, r in zip((w_up, w_down), cast_rows)]
    hm = lambda width, dt: (jax.ShapeDtypeStruct((batch, RET_HEADS, seq, width), dt),
                            pl.BlockSpec((None, RET_HEADS, tm, width), lambda b, t: (b, 0, t, 0)))
    nat = (jax.ShapeDtypeStruct((batch * seq, D_MODEL), F32),
           pl.BlockSpec((tm, D_MODEL), lambda b, t: (b * tps + t, 0)))
    outs = (hm(RET_DK, F32), hm(RET_DK, F32), hm(RET_DV, BF16), hm(RET_DV, F32), nat, nat)
    return pl.pallas_call(
        _in_proj_rest_kernel,
        grid=(batch, tps),
        in_specs=[
            pl.BlockSpec((tm, D_MODEL), lambda b, t: (b * tps + t, 0)),
            _resident((1, D_MODEL), lambda b, t: (0, 0)),
            _resident(w_rest.shape, lambda b, t: (0, 0)),
            pl.BlockSpec((tm, RET_DK), lambda b, t: (t, 0)),
            pl.BlockSpec((tm, RET_DK), lambda b, t: (t, 0)),
        ] + cast_specs,
        out_specs=[spec for _, spec in outs] + cast_specs,
        out_shape=[shp for shp, _ in outs] + [jax.ShapeDtypeStruct(w.shape, BF16) for w in (w_up, w_down)],
        compiler_params=_cparams(2, 48),
        name="in_proj_rest",
    )(x2d, g, w_rest, cos_t, sin_t, w_up, w_down)


def _in_proj_sample_kernel(x_ref, g_ref, w_ref, z_ref, w_att_ref, w_rest_ref, h_ref):
    j = pl.program_id(0)

    @pl.when(j == 0)
    def _():
        h_ref[...] = _norm_rows(x_ref, g_ref)

    w_bf = w_ref[...].astype(BF16)

    @pl.when(j < N_ATT_TILES)
    def _():
        w_att_ref[...] = w_bf

    @pl.when(j >= N_ATT_TILES)
    def _():
        w_rest_ref[...] = w_bf

    z_ref[...] = _dot(h_ref[...], w_bf)


def _in_proj_sample(x2d, g, w):
    m = x2d.shape[0]
    col_tile = lambda rows, f: pl.BlockSpec((rows, TN), lambda j: (0, f(j)))
    n_rest = N_COL_TILES - N_ATT_TILES
    return pl.pallas_call(
        _in_proj_sample_kernel,
        grid=(N_COL_TILES,),
        in_specs=[
            _resident((m, D_MODEL), lambda j: (0, 0)),
            _resident((1, D_MODEL), lambda j: (0, 0)),
            col_tile(D_MODEL, lambda j: j),
        ],
        out_specs=[col_tile(m, lambda j: j),
                   col_tile(D_MODEL, lambda j: jnp.minimum(j, N_ATT_TILES - 1)),
                   col_tile(D_MODEL, lambda j: jnp.maximum(j - N_ATT_TILES, 0))],
        out_shape=[jax.ShapeDtypeStruct((m, IN_COLS), F32),
                   jax.ShapeDtypeStruct((D_MODEL, N_ATT_TILES * TN), BF16),
                   jax.ShapeDtypeStruct((D_MODEL, n_rest * TN), BF16)],
        scratch_shapes=[pltpu.VMEM((m, D_MODEL), BF16)],
        compiler_params=_cparams(1, 24),
        name="in_proj_sample",
    )(x2d, g, w)


def _softmax_block(s, v):
    m = jnp.max(s, axis=1, keepdims=True)
    p = jnp.exp(s - m).astype(BF16)
    uv = _dot(p, jnp.concatenate([v, jnp.ones_like(v)], axis=1))
    return uv[:, :HEAD_DIM], jnp.broadcast_to(m, (s.shape[0], HEAD_DIM)), uv[:, HEAD_DIM:]


ATT_BLOCKS_PER_STEP = 16
ATT_ORDER = 4


def _attn_prompt_kernel(q0, q1, q2, k0, k1, k2, v0, v1, v2, o_ref,
                        u0, u1, u2, m0, m1, m2, l0, l1, l2, *, seq):
    qs, ks, vs = (q0, q1, q2), (k0, k1, k2), (v0, v1, v2)
    us, ms, ls = (u0, u1, u2), (m0, m1, m2), (l0, l1, l2)
    order = ATT_ORDER
    n_out = seq // order
    row1 = lax.broadcasted_iota(jnp.int32, (BAND, BAND), 0)
    col1 = lax.broadcasted_iota(jnp.int32, (BAND, BAND), 1)
    causal = col1 <= row1
    row2 = lax.broadcasted_iota(jnp.int32, (BAND, 2 * BAND), 0)
    col2 = lax.broadcasted_iota(jnp.int32, (BAND, 2 * BAND), 1)
    band = (col2 >= row2) & (col2 <= row2 + BAND)

    aligned = lambda r: r if isinstance(r, int) else pl.multiple_of(r, BAND)

    for g, (_, dil) in enumerate(DIL_GROUPS):
        n = seq // dil
        nb = n // BAND

        def run_blocks(blocks, g=g, n=n, dil=dil):
            scores, values = [], []
            for c, blk, has_prev in blocks:
                own = pl.ds(aligned(blk * BAND), BAND)
                keys = pl.ds(aligned((blk - 1) * BAND), 2 * BAND) if has_prev else own
                mask = band if has_prev else causal
                scores.append(jnp.where(mask, _dot_nt(qs[g][c, own, :], ks[g][c, keys, :]), NEG_INF))
                values.append(vs[g][c, keys, :])
            results = [_softmax_block(s, v) for s, v in zip(scores, values)]
            for (c, blk, _), (u, m, l) in zip(blocks, results):
                if dil <= order:
                    rs = pl.ds(aligned(c * n + blk * BAND), BAND)
                else:
                    step = dil // order
                    rs = pl.ds((c % order) * n_out + c // order + blk * (BAND * step), BAND, stride=step)
                us[g][rs, :] = u
                ms[g][rs, :] = m
                ls[g][rs, :] = l

        per = ATT_BLOCKS_PER_STEP
        assert (dil * nb) % per == 0 and (per % nb == 0 or nb % per == 0)
        if nb >= per:
            def body(i, carry, run_blocks=run_blocks, steps=nb // per):
                c, i0 = i // steps, (i % steps) * per
                run_blocks([(c, i0 + e, True) for e in range(per)])
                return carry
            for c in range(dil):
                run_blocks([(c, e, e > 0) for e in range(per)])
                lax.fori_loop(c * (nb // per) + 1, (c + 1) * (nb // per), body, 0)
        else:
            def body(i, carry, run_blocks=run_blocks, nb=nb, cps=per // nb):
                run_blocks([(i * cps + e // nb, e % nb, e % nb > 0) for e in range(per)])
                return carry
            lax.fori_loop(0, dil * nb // per, body, 0)

    chunks = n_out // BAND

    def combine_one(j):
        c, t = j // chunks, j % chunks
        rows = []
        for _, dil in DIL_GROUPS:
            r = min(dil, order)
            step = order // r
            start = (c % r) * (seq // r) + c // r + t * (BAND * step)
            rows.append(pl.ds(aligned(start), BAND) if step == 1 else pl.ds(start, BAND, stride=step))
        m_g = [ms[g][rows[g], :] for g in range(N_GROUPS)]
        m = jnp.maximum(jnp.maximum(m_g[0], m_g[1]), m_g[2])
        num = jnp.zeros((BAND, HEAD_DIM), F32)
        den = jnp.zeros((BAND, HEAD_DIM), F32)
        for g in range(N_GROUPS):
            w = jnp.exp(m_g[g] - m)
            num = num + w * us[g][rows[g], :]
            den = den + w * ls[g][rows[g], :]
        return num / den

    per = 4
    assert (order * chunks) % per == 0

    def combine(i, carry):
        outs = [combine_one(i * per + e) for e in range(per)]
        for e, o in enumerate(outs):
            o_ref[pl.ds(aligned((i * per + e) * BAND), BAND), :] = o
        return carry

    lax.fori_loop(0, order * chunks // per, combine, 0)


def _attn_prompt(qkv, batch, seq):
    assert seq % (BAND * DIL_GROUPS[-1][1]) == 0
    assert all(ATT_ORDER % d == 0 or d % ATT_ORDER == 0 for _, d in DIL_GROUPS)
    in_specs = [pl.BlockSpec((None, None) + a.shape[2:], lambda b, h: (b, h, 0, 0, 0)) for a in qkv]
    out = pl.pallas_call(
        functools.partial(_attn_prompt_kernel, seq=seq),
        grid=(batch, HEADS),
        in_specs=in_specs,
        out_specs=pl.BlockSpec((None, None, seq, HEAD_DIM), lambda b, h: (b, h, 0, 0)),
        out_shape=jax.ShapeDtypeStruct((batch, HEADS, seq, HEAD_DIM), F32),
        scratch_shapes=[pltpu.VMEM((seq, HEAD_DIM), F32)] * 9,
        compiler_params=_cparams(2, 32),
        name="attend_prompt",
    )(*qkv)
    return out.reshape(batch, HEADS, ATT_ORDER, seq // ATT_ORDER, HEAD_DIM)


def _group_norm_gate(o, gn, gate):
    mu = jnp.mean(o, axis=-1, keepdims=True)
    d = o - mu
    var = jnp.mean(d * d, axis=-1, keepdims=True)
    return gate * (d * lax.rsqrt(var + EPS) * gn)


RET_CHUNKS_PER_STEP = 2


def _ret_prompt_kernel(lg_ref, q_ref, k_ref, v_ref, gate_ref, gn_ref, o_ref, s_out_ref, s_scr):
    c = RET_CHUNK
    rows = q_ref.shape[1]
    t = lax.broadcasted_iota(jnp.int32, (c, 1), 0).astype(F32)
    rel = (lax.broadcasted_iota(jnp.int32, (c, c), 0) - lax.broadcasted_iota(jnp.int32, (c, c), 1)).astype(F32)

    @pl.when(pl.program_id(1) == 0)
    def _():
        s_scr[...] = jnp.zeros_like(s_scr)

    consts = []
    for h in range(RET_HEADS):
        lg = lg_ref[h]
        consts.append((
            jnp.where(rel >= 0, jnp.exp(lg * jnp.maximum(rel, 0.0)), 0.0),
            jnp.exp(lg * (t + 1.0)),
            jnp.exp(lg * (c - 1.0 - t)),
            jnp.exp(lg * jnp.full((1, RET_DV), float(c), F32)),
        ))

    per = RET_CHUNKS_PER_STEP
    assert (rows // c) % per == 0

    def step(i, carry):
        rss = [pl.ds(pl.multiple_of((i * per + e) * c, c), c) for e in range(per)]
        local = {}
        for h in range(RET_HEADS):
            decay, inner, tail, _ = consts[h]
            for e, rs in enumerate(rss):
                q, k, vb = q_ref[h, rs, :], k_ref[h, rs, :], v_ref[h, rs, :]
                a = _dot_nt(q.astype(BF16), k.astype(BF16)) * decay
                local[h, e] = (_dot(a.astype(BF16), vb),
                               (q * inner).astype(BF16),
                               _dot((k * tail).T.astype(BF16), vb))
        outs = {}
        for h in range(RET_HEADS):
            s = s_scr[h]
            for e in range(per):
                intra, q_in, kv = local[h, e]
                outs[h, e] = intra + _dot(q_in, s.astype(BF16))
                s = consts[h][3] * s + kv
            s_scr[h] = s
        for h in range(RET_HEADS):
            gn = gn_ref[:, h * RET_DV:(h + 1) * RET_DV]
            for e, rs in enumerate(rss):
                o_ref[h, rs, :] = _group_norm_gate(outs[h, e], gn, gate_ref[h, rs, :]).astype(o_ref.dtype)
        return carry

    lax.fori_loop(0, rows // (c * per), step, 0)

    @pl.when(pl.program_id(1) == pl.num_programs(1) - 1)
    def _():
        s_out_ref[...] = s_scr[...]


def _ret_prompt(log_g, rq, rk, rv, gsw, gn, batch, seq, rows):
    assert seq % rows == 0 and rows % RET_CHUNK == 0
    hm = lambda width: pl.BlockSpec((None, RET_HEADS, rows, width), lambda b, t: (b, 0, t, 0))
    return pl.pallas_call(
        _ret_prompt_kernel,
        grid=(batch, seq // rows),
        in_specs=[
            pl.BlockSpec(memory_space=pltpu.SMEM),
            hm(RET_DK), hm(RET_DK), hm(RET_DV), hm(RET_DV),
            _resident((1, RET_V), lambda b, t: (0, 0)),
        ],
        out_specs=[
            hm(RET_DV),
            pl.BlockSpec((None, RET_HEADS, RET_DK, RET_DV), lambda b, t: (b, 0, 0, 0)),
        ],
        out_shape=[
            jax.ShapeDtypeStruct((batch, RET_HEADS, seq, RET_DV), BF16),
            jax.ShapeDtypeStruct((batch, RET_HEADS, RET_DK, RET_DV), F32),
        ],
        scratch_shapes=[pltpu.VMEM((RET_HEADS, RET_DK, RET_DV), F32)],
        compiler_params=_cparams(2, 40),
        name="retain_prompt",
    )(log_g, rq, rk, rv, gsw, gn)


def _sample_attend_row(z_ref, caches, r):
    half = BAND // 2

    def heads(tile0, g):
        base = (tile0 + g) * TN
        return jnp.concatenate(
            [z_ref[r, :, base + hh * HEAD_DIM:base + (hh + 1) * HEAD_DIM] for hh in range(HEADS)], axis=0)

    def paired(ref, lo):
        return jnp.concatenate([ref[r, 0:half, lo:lo + HEADS, :], ref[r, half:BAND, lo:lo + HEADS, :]], axis=1)

    twice = lambda a: jnp.concatenate([a, a], axis=0)
    fold = lambda a, op: op(a[0:HEADS], a[HEADS:2 * HEADS])

    us, ms, ls = [], [], []
    for g in range(N_GROUPS):
        q = heads(COL_Q, g) * (HEAD_DIM ** -0.5)
        k_new, v_new = heads(COL_K, g), heads(COL_V, g)
        k_buf, v_buf = paired(caches[g], 0), paired(caches[g], HEADS)
        s_buf = jnp.sum(k_buf * twice(q)[None], axis=-1, keepdims=True)
        s_new = jnp.sum(k_new * q, axis=-1, keepdims=True)
        m = jnp.maximum(fold(jnp.max(s_buf, axis=0), jnp.maximum), s_new)
        p_buf = jnp.exp(s_buf - twice(m)[None])
        p_new = jnp.exp(s_new - m)
        ls.append(fold(jnp.sum(p_buf, axis=0), jnp.add) + p_new)
        us.append(fold(jnp.sum(p_buf * v_buf, axis=0), jnp.add) + p_new * v_new)
        ms.append(m)
    m = jnp.maximum(jnp.maximum(ms[0], ms[1]), ms[2])
    num = jnp.zeros((HEADS, HEAD_DIM), F32)
    den = jnp.zeros((HEADS, 1), F32)
    for g in range(N_GROUPS):
        w = jnp.exp(ms[g] - m)
        num = num + w * us[g]
        den = den + w * ls[g]
    return num / den


def _sample_retain_head(lg_ref, z_ref, cos, sin, gn_ref, s_ref, ret_ref, s_out_ref, h):
    rows = z_ref.shape[0]
    eye = (lax.broadcasted_iota(jnp.int32, (RET_DK, RET_DK), 0)
           == lax.broadcasted_iota(jnp.int32, (RET_DK, RET_DK), 1))

    def column(v):
        return jnp.sum(jnp.where(eye, jnp.broadcast_to(v, (RET_DK, RET_DK)), 0.0), axis=1, keepdims=True)

    def zcols(tile0, width):
        return z_ref[:, 0, tile0 * TN + h * width:tile0 * TN + (h + 1) * width]

    gamma = jnp.exp(lg_ref[h] * jnp.ones((1, RET_DV), F32))
    q_h = _rope(zcols(COL_RQ, RET_DK), cos, sin)
    k_h = _rope(zcols(COL_RK, RET_DK), cos, sin) * (RET_DK ** -0.5)
    v_h = zcols(COL_RV, RET_DV)
    outs = []
    for r in range(rows):
        s_new = gamma * s_ref[r, h] + column(k_h[r:r + 1, :]) * v_h[r:r + 1, :]
        s_out_ref[r, h] = s_new
        outs.append(jnp.sum(column(q_h[r:r + 1, :]) * s_new, axis=0, keepdims=True))
    seg = slice(h * RET_DV, (h + 1) * RET_DV)
    ret_ref[:, 0, seg] = _group_norm_gate(jnp.concatenate(outs, axis=0), gn_ref[:, seg], _silu(zcols(COL_GSW, RET_DV)))


FF_CHUNK = 1024


def _tail_kernel(*refs, side_rows):
    (att_ref, ret_ref, ga_ref, gr_ref, x_ref, wa_ref, wr_ref, wo_ref, gpost_mix_ref, gpre_mlp_ref,
     wu_ref, wd_ref, gpost_mlp_ref) = refs[:13]
    att_scr = refs[-1]
    side = []
    if side_rows:
        lg_ref, z_ref, cos_ref, sin_ref, gn_ref, c0, c1, c2, s_ref, y_ref, att_s_ref, ret_s_ref, s_out_ref = refs[13:-1]

        def attend(r):
            att_s_ref[r] = _sample_attend_row(z_ref, (c0, c1, c2), r)

        def retain(h):
            _sample_retain_head(lg_ref, z_ref, cos_ref[...], sin_ref[...], gn_ref, s_ref, ret_s_ref, s_out_ref, h)

        side = [functools.partial(attend, r) for r in range(side_rows)]
        side += [functools.partial(retain, h) for h in range(RET_HEADS)]
    else:
        y_ref, = refs[13:-1]

    r = att_ref.shape[1]
    for hh in range(HEADS):
        for c in range(r):
            rs = pl.ds(c, att_ref.shape[2], stride=r) if r > 1 else slice(None)
            att_scr[hh, rs, :] = att_ref[hh, c]
    att = jnp.concatenate([att_scr[hh].astype(BF16) for hh in range(HEADS)], axis=1)
    ret = jnp.concatenate([ret_ref[hh].astype(BF16) for hh in range(RET_HEADS)], axis=1)
    m = _sigmoid(ga_ref[...]) * _dot(att, wa_ref[...]) + _sigmoid(gr_ref[...]) * _dot(ret, wr_ref[...])
    x1 = x_ref[...] + _rms(_dot(m.astype(BF16), wo_ref[...])) * gpost_mix_ref[...]
    h = (_rms(x1) * gpre_mlp_ref[...]).astype(BF16)

    n_chunks = D_FF // FF_CHUNK
    acc = jnp.zeros(x_ref.shape, F32)
    for c in range(n_chunks):
        cs = slice(c * FF_CHUNK, (c + 1) * FF_CHUNK)
        u = jnp.maximum(_dot(h, wu_ref[:, cs]), 0.0)
        acc = acc + _dot((u * u).astype(BF16), wd_ref[cs, :])
        for work in side[c::n_chunks]:
            work()
    y_ref[...] = x1 + _rms(acc) * gpost_mlp_ref[...]


def _tail(att, ret, ga, gr, x2d, lw, tm, side=None):
    batch, _, r, ni, _ = att.shape
    seq = r * ni
    m = batch * seq
    assert seq % tm == 0 and tm % r == 0 and x2d.shape[0] == m
    tps = seq // tm
    steps = batch * tps
    rows = pl.BlockSpec((tm, D_MODEL), lambda i: (i, 0))
    full = lambda a: _resident(a.shape, lambda i: (0,) * a.ndim)
    weights = [lw[k] for k in ("w_att_br", "w_ret_br", "w_out", "g_post_mix", "g_pre_mlp",
                               "w_up", "w_down", "g_post_mlp")]
    args = [att, ret, ga, gr, x2d] + weights
    in_specs = [pl.BlockSpec((None, HEADS, r, tm // r, HEAD_DIM), lambda i: (i // tps, 0, 0, i % tps, 0)),
                pl.BlockSpec((None, RET_HEADS, tm, RET_DV), lambda i: (i // tps, 0, i % tps, 0)),
                rows, rows, rows] + [full(w) for w in weights]
    out_shape = [jax.ShapeDtypeStruct((m, D_MODEL), F32)]
    out_specs = [rows]
    side_rows = 0
    if side is not None:
        log_g, z, cos_t, sin_t, gn, caches, state, layer = side
        db = z.shape[0]
        assert db % steps == 0
        side_rows = db // steps
        st_block = (side_rows, RET_HEADS, RET_DK, RET_DV)
        args += [log_g, z.reshape(db, 1, IN_COLS), cos_t, sin_t, gn]
        in_specs += [pl.BlockSpec(memory_space=pltpu.SMEM),
                     pl.BlockSpec((side_rows, 1, IN_COLS), lambda i: (i, 0, 0)),
                     full(cos_t), full(sin_t), full(gn)]
        for (win, dil), c in zip(DIL_GROUPS, caches):
            assert c.shape[1] == db and c.shape[2] == win and win // dil == BAND
            args.append(c.reshape(c.shape[0] * db, BAND, dil, 2 * HEADS, HEAD_DIM))
            in_specs.append(pl.BlockSpec((side_rows, BAND, None, 2 * HEADS, HEAD_DIM),
                                         lambda i: (layer * steps + i, 0, 0, 0, 0)))
        args.append(state.reshape((state.shape[0] * db,) + state.shape[2:]))
        in_specs.append(pl.BlockSpec(st_block, lambda i: (layer * steps + i, 0, 0, 0)))
        out_shape += [jax.ShapeDtypeStruct((db, HEADS, HEAD_DIM), F32),
                      jax.ShapeDtypeStruct((db, 1, RET_V), F32),
                      jax.ShapeDtypeStruct((db, RET_HEADS, RET_DK, RET_DV), F32)]
        out_specs += [pl.BlockSpec((side_rows, HEADS, HEAD_DIM), lambda i: (i, 0, 0)),
                      pl.BlockSpec((side_rows, 1, RET_V), lambda i: (i, 0, 0)),
                      pl.BlockSpec(st_block, lambda i: (i, 0, 0, 0))]
    outs = pl.pallas_call(
        functools.partial(_tail_kernel, side_rows=side_rows),
        grid=(steps,),
        in_specs=in_specs,
        out_specs=out_specs,
        out_shape=out_shape,
        scratch_shapes=[pltpu.VMEM((HEADS, tm, HEAD_DIM), F32)],
        compiler_params=_cparams(1, 56),
        name="tail",
    )(*args)
    if side is None:
        return outs[0]
    y, att, ret, state_new = outs
    return y, att, ret.reshape(db, RET_V), state_new


def _rope_tables(pos):
    half = RET_DK // 2
    inv = ROPE_BASE ** (-jnp.arange(half, dtype=F32) / half)
    ang = pos.astype(F32)[:, None] * inv[None, :]
    cos, sin = jnp.cos(ang), jnp.sin(ang)
    return jnp.concatenate([cos, cos], axis=-1), jnp.concatenate([-sin, sin], axis=-1)


IN_PROJ_TM = 512
TAIL_TM = 256
RET_ROWS = 1024


def _layer(xp, xs, caches, state, layer, lw, log_g):
    batch, seq, _ = xp.shape
    db, t, _ = xs.shape
    assert t == 1
    xp2d, xs2d = xp.reshape(batch * seq, D_MODEL), xs.reshape(db, D_MODEL)

    z, w_att, w_rest = _in_proj_sample(xs2d, lw["g_pre_mix"], lw["w_in"])

    cos_p, sin_p = _rope_tables(jnp.arange(seq, dtype=jnp.int32))
    outs = _in_proj_att(xp2d, lw["g_pre_mix"], w_att, batch, seq, IN_PROJ_TM)
    qkv = outs[:9]
    rows_p = [kv.reshape(batch, -1, 2, HEADS, HEAD_DIM) for kv in outs[9:12]]
    rq, rk, rv, gsw, ga, gr, w_up_bf, w_down_bf = _in_proj_rest(
        xp2d, lw["g_pre_mix"], w_rest, cos_p, sin_p, lw["w_up"], lw["w_down"], batch, seq, IN_PROJ_TM)
    lw = dict(lw, w_up=w_up_bf, w_down=w_down_bf)
    att = _attn_prompt(qkv, batch, seq)
    ret, state_p = _ret_prompt(log_g, rq, rk, rv, gsw, lw["g_ret_norm"], batch, seq, RET_ROWS)

    cos_s, sin_s = _rope_tables(PAST_LEN + jnp.arange(1, dtype=jnp.int32))
    yp, att_s, ret_s, state_s = _tail(att, ret, ga, gr, xp2d, lw, TAIL_TM,
                                      side=(log_g, z, cos_s, sin_s, lw["g_ret_norm"], caches, state, layer))

    att_s = jnp.transpose(att_s, (1, 0, 2)).reshape(1, HEADS, 1, db, HEAD_DIM)
    ret_s = jnp.transpose(ret_s.reshape(db, RET_HEADS, RET_DV), (1, 0, 2))[None]
    ga_s = z[:, COL_GA * TN:COL_GA * TN + D_MODEL]
    gr_s = z[:, COL_GR * TN:COL_GR * TN + D_MODEL]
    ys = _tail(att_s, ret_s, ga_s, gr_s, xs2d, lw, db)
    rows_s = []
    for g in range(N_GROUPS):
        k_new = z[:, (COL_K + g) * TN:(COL_K + g + 1) * TN]
        v_new = z[:, (COL_V + g) * TN:(COL_V + g + 1) * TN]
        rows_s.append(jnp.stack([k_new, v_new], axis=1).reshape(db, 1, 2, HEADS, HEAD_DIM))
    return (yp.reshape(batch, seq, D_MODEL), rows_p, state_p), (ys.reshape(db, 1, D_MODEL), rows_s, state_s)


def _stack(xs):
    return xs[0][None] if len(xs) == 1 else jnp.stack(xs, axis=0)


def kernel(x_prompt, x_sample, cache_kv_d1, cache_kv_d4, cache_kv_d16, state_ret, w_in, w_att_br, w_ret_br, w_out, w_up, w_down, g_ret_norm, g_pre_mix, g_post_mix, g_pre_mlp, g_post_mlp):
    depth = w_in.shape[0]
    log_g = jnp.log1p(-jnp.power(2.0, -5.0 - jnp.arange(RET_HEADS, dtype=F32)))
    caches = (cache_kv_d1, cache_kv_d4, cache_kv_d16)
    xp, xs = x_prompt, x_sample
    p_rows, s_rows = [[], [], []], [[], [], []]
    p_states, s_states = [], []
    for l in range(depth):
        lw = {
            "w_in": w_in[l], "w_up": w_up[l], "w_down": w_down[l],
            "w_att_br": w_att_br[l].astype(BF16), "w_ret_br": w_ret_br[l].astype(BF16),
            "w_out": w_out[l].astype(BF16),
            "g_ret_norm": g_ret_norm[l].reshape(1, RET_V), "g_pre_mix": g_pre_mix[l].reshape(1, D_MODEL),
            "g_post_mix": g_post_mix[l].reshape(1, D_MODEL), "g_pre_mlp": g_pre_mlp[l].reshape(1, D_MODEL),
            "g_post_mlp": g_post_mlp[l].reshape(1, D_MODEL),
        }
        (xp, rows_p, sp), (xs, rows_s, ss) = _layer(xp, xs, caches, state_ret, l, lw, log_g)
        for g in range(N_GROUPS):
            p_rows[g].append(rows_p[g])
            s_rows[g].append(rows_s[g])
        p_states.append(sp)
        s_states.append(ss)
    return (xp, xs, _stack(p_rows[0]), _stack(p_rows[1]), _stack(p_rows[2]), _stack(p_states),
            _stack(s_rows[0]), _stack(s_rows[1]), _stack(s_rows[2]), _stack(s_states))
```

```python
import functools

import jax
import jax.numpy as jnp
from jax import lax
from jax.experimental import pallas as pl
from jax.experimental.pallas import tpu as pltpu

F32 = jnp.float32
BF16 = jnp.bfloat16

D_MODEL = 1024
PAST_LEN = 8192

DIL_GROUPS = ((128, 1), (512, 4), (2048, 16))
N_GROUPS = 3
HEADS = 4
HEAD_DIM = 128
ATT_OUT = HEADS * HEAD_DIM
ATT_COLS = N_GROUPS * ATT_OUT
BAND = 128

RET_HEADS = 4
RET_DK = 128
RET_DV = 256
RET_QK = RET_HEADS * RET_DK
RET_V = RET_HEADS * RET_DV
RET_CHUNK = 128
ROPE_BASE = 10000.0

D_FF = 4 * D_MODEL
IN_COLS = 3 * ATT_COLS + 2 * RET_QK + 2 * RET_V + 2 * D_MODEL
EPS = 1e-6
NEG_INF = -1e30

TN = 512
N_COL_TILES = IN_COLS // TN
COL_Q, COL_K, COL_V, COL_RQ, COL_RK, COL_RV, COL_GSW, COL_GA, COL_GR = 0, 3, 6, 9, 10, 11, 13, 15, 17

V7X_VMEM_BYTES = 64 * 1024 * 1024
MIB = 1024 * 1024


def _cparams(n_axes, vmem_mib):
    assert vmem_mib * MIB < V7X_VMEM_BYTES
    return pltpu.CompilerParams(
        dimension_semantics=("arbitrary",) * n_axes,
        vmem_limit_bytes=vmem_mib * MIB,
    )


def _resident(shape, index_map):
    return pl.BlockSpec(shape, index_map, pipeline_mode=pl.Buffered(1))


def _rms(x):
    return x * lax.rsqrt(jnp.mean(x * x, axis=-1, keepdims=True) + EPS)


def _sigmoid(x):
    return 1.0 / (1.0 + jnp.exp(-x))


def _silu(x):
    return x * _sigmoid(x)


def _dot(a, b):
    return jnp.dot(a, b, preferred_element_type=F32)


def _dot_nt(a, b):
    return lax.dot_general(a, b, (((1,), (1,)), ((), ())), preferred_element_type=F32)


def _rope(x, cos, sin):
    return x * cos + pltpu.roll(x, RET_DK // 2, 1) * sin


N_ATT_TILES = 3 * N_GROUPS


def _norm_rows(x_ref, g_ref):
    return (_rms(x_ref[...]) * g_ref[...]).astype(BF16)


def _head(acc, hh, width=HEAD_DIM):
    return acc[:, hh * width:(hh + 1) * width]


def _in_proj_att_kernel(x_ref, g_ref, w_ref, z_ref, c0, c1, c2,
                        q0, q1, q2, k0, k1, k2, v0, v1, v2, kvo0, kvo1, kvo2, att_s_ref,
                        scr_q, sk0, sk1, sk2, sv0, sv1, sv2, *, first_kept):
    tm = x_ref.shape[0]
    h = _norm_rows(x_ref, g_ref)
    side = list(range(z_ref.shape[0]))

    def park(j, scr, scale=None):
        acc = _dot(h, w_ref[:, j * TN:(j + 1) * TN])
        for hh in range(HEADS):
            scr[hh] = _head(acc, hh) if scale is None else _head(acc, hh) * scale

    def residues(scr, dst, dil):
        for hh in range(HEADS):
            for c in range(dil):
                rows = pl.ds(c, tm // dil, stride=dil) if dil > 1 else slice(None)
                dst[hh, c] = scr[hh, rows, :].astype(BF16)

    def side_row():
        if side:
            r = side.pop(0)
            att_s_ref[r] = _sample_attend_row(z_ref, (c0, c1, c2), r)

    for g, (_, dil) in enumerate(DIL_GROUPS):
        park(COL_Q + g, scr_q, HEAD_DIM ** -0.5)
        residues(scr_q, (q0, q1, q2)[g], dil)
        side_row()
        park(COL_K + g, (sk0, sk1, sk2)[g])
        residues((sk0, sk1, sk2)[g], (k0, k1, k2)[g], dil)
        park(COL_V + g, (sv0, sv1, sv2)[g])
        residues((sv0, sv1, sv2)[g], (v0, v1, v2)[g], dil)
        side_row()
    while side:
        side_row()

    for g, kvo in enumerate((kvo0, kvo1, kvo2)):
        def write(g=g, kvo=kvo):
            keep = kvo.shape[0] // (2 * HEADS)
            for which, scr in enumerate(((sk0, sk1, sk2)[g], (sv0, sv1, sv2)[g])):
                for hh in range(HEADS):
                    kvo[pl.ds(which * HEADS + hh, keep, stride=2 * HEADS), :] = scr[hh, tm - keep:, :]
        if first_kept[g] == 0:
            write()
        else:
            pl.when(pl.program_id(1) >= first_kept[g])(write)


def _in_proj_att(x2d, g, w_att, z, caches, layer, batch, seq, tm):
    assert seq % tm == 0 and tm % (16 * DIL_GROUPS[-1][1]) == 0
    tps = seq // tm
    out_shape, out_specs, first_kept = [], [], []
    for _ in range(3):
        for _, dil in DIL_GROUPS:
            out_shape.append(jax.ShapeDtypeStruct((batch, HEADS, dil, seq // dil, HEAD_DIM), BF16))
            out_specs.append(pl.BlockSpec((None, HEADS, dil, tm // dil, HEAD_DIM), lambda b, t: (b, 0, 0, t, 0)))
    for win, _ in DIL_GROUPS:
        keep = min(win, seq)
        rows = min(keep, tm)
        assert keep % rows == 0
        first = (seq - keep) // rows
        first_kept.append((seq - keep) // tm)
        out_shape.append(jax.ShapeDtypeStruct((batch, keep * 2 * HEADS, HEAD_DIM), F32))
        out_specs.append(pl.BlockSpec(
            (None, rows * 2 * HEADS, HEAD_DIM),
            functools.partial(lambda b, t, first, per: (b, jnp.maximum((t + 1) * per - 1 - first, 0), 0),
                              first=first, per=tm // rows)))
    db = z.shape[0]
    steps = batch * tps
    assert db % steps == 0
    side_rows = db // steps
    step = lambda b, t: layer * steps + b * tps + t
    views, cache_specs = [], []
    for (win, dil), c in zip(DIL_GROUPS, caches):
        assert c.shape[1] == db and c.shape[2] == win and win // dil == BAND
        views.append(c.reshape(c.shape[0] * db, BAND, dil, 2 * HEADS, HEAD_DIM))
        cache_specs.append(pl.BlockSpec((side_rows, BAND, None, 2 * HEADS, HEAD_DIM),
                                        lambda b, t: (step(b, t), 0, 0, 0, 0)))
    out_shape.append(jax.ShapeDtypeStruct((db, HEADS, HEAD_DIM), F32))
    out_specs.append(pl.BlockSpec((side_rows, HEADS, HEAD_DIM), lambda b, t: (b * tps + t, 0, 0)))
    return pl.pallas_call(
        functools.partial(_in_proj_att_kernel, first_kept=tuple(first_kept)),
        grid=(batch, tps),
        in_specs=[
            pl.BlockSpec((tm, D_MODEL), lambda b, t: (b * tps + t, 0)),
            _resident((1, D_MODEL), lambda b, t: (0, 0)),
            _resident(w_att.shape, lambda b, t: (0, 0)),
            pl.BlockSpec((side_rows, 1, IN_COLS), lambda b, t: (b * tps + t, 0, 0)),
        ] + cache_specs,
        out_specs=out_specs,
        out_shape=out_shape,
        scratch_shapes=[pltpu.VMEM((HEADS, tm, HEAD_DIM), F32)] * (1 + 2 * N_GROUPS),
        compiler_params=_cparams(2, 56),
        name="in_proj_att",
    )(x2d, g, w_att, z.reshape(db, 1, IN_COLS), *views)


def _in_proj_rest_kernel(x_ref, g_ref, w_ref, cos_ref, sin_ref, wu_ref, wd_ref,
                         lg_ref, z_ref, cos_s_ref, sin_s_ref, gn_ref, s_ref,
                         rq_ref, rk_ref, rv_ref, gsw_ref, ga_ref, gr_ref, wu_bf_ref, wd_bf_ref,
                         ret_s_ref, s_out_ref):
    h = _norm_rows(x_ref, g_ref)

    def side_head(hh):
        _sample_retain_head(lg_ref, z_ref, cos_s_ref[...], sin_s_ref[...], gn_ref, s_ref, ret_s_ref, s_out_ref, hh)

    def tile(j):
        lo = (j - N_ATT_TILES) * TN
        return _dot(h, w_ref[:, lo:lo + TN])

    def ret_qk(col, dst, scale):
        acc = tile(col)
        cos, sin = cos_ref[...], sin_ref[...]
        for hh in range(RET_HEADS):
            r = _rope(_head(acc, hh), cos, sin)
            dst[hh] = r if scale is None else r * scale

    def ret_wide(col, dst, e, act=None):
        acc = tile(col + e)
        for s in range(2):
            part = _head(acc, s, RET_DV)
            dst[2 * e + s] = (part if act is None else act(part)).astype(dst.dtype)

    def gate(col, dst, e):
        dst[:, e * TN:(e + 1) * TN] = tile(col + e)

    ret_qk(COL_RQ, rq_ref, None)
    wu_bf_ref[...] = wu_ref[...].astype(BF16)
    wd_bf_ref[...] = wd_ref[...].astype(BF16)
    ret_qk(COL_RK, rk_ref, RET_DK ** -0.5)
    for e in range(2):
        ret_wide(COL_RV, rv_ref, e)
        side_head(2 * e)
        ret_wide(COL_GSW, gsw_ref, e, _silu)
        gate(COL_GA, ga_ref, e)
        side_head(2 * e + 1)
        gate(COL_GR, gr_ref, e)


def _in_proj_rest(x2d, g, w_rest, cos_t, sin_t, w_up, w_down, side, batch, seq, tm):
    assert seq % tm == 0
    tps = seq // tm
    steps = batch * tps
    log_g, z, cos_s, sin_s, gn, state, layer = side
    db = z.shape[0]
    assert db % steps == 0 and RET_HEADS == 4
    side_rows = db // steps
    st_block = (side_rows, RET_HEADS, RET_DK, RET_DV)
    full = lambda a: _resident(a.shape, lambda b, t: (0,) * a.ndim)
    cast_rows = [w.shape[0] // steps for w in (w_up, w_down)]
    assert all(w.shape[0] % steps == 0 and r % 16 == 0 for w, r in zip((w_up, w_down), cast_rows))
    cast_specs = [pl.BlockSpec((r, w.shape[1]), lambda b, t: (b * tps + t, 0))
                  for w, r in zip((w_up, w_down), cast_rows)]
    hm = lambda width, dt: (jax.ShapeDtypeStruct((batch, RET_HEADS, seq, width), dt),
                            pl.BlockSpec((None, RET_HEADS, tm, width), lambda b, t: (b, 0, t, 0)))
    nat = (jax.ShapeDtypeStruct((batch * seq, D_MODEL), F32),
           pl.BlockSpec((tm, D_MODEL), lambda b, t: (b * tps + t, 0)))
    outs = (hm(RET_DK, F32), hm(RET_DK, F32), hm(RET_DV, BF16), hm(RET_DV, F32), nat, nat)
    side_outs = ((jax.ShapeDtypeStruct((db, 1, RET_V), F32),
                  pl.BlockSpec((side_rows, 1, RET_V), lambda b, t: (b * tps + t, 0, 0))),
                 (jax.ShapeDtypeStruct((db,) + st_block[1:], F32),
                  pl.BlockSpec(st_block, lambda b, t: (b * tps + t, 0, 0, 0))))
    results = pl.pallas_call(
        _in_proj_rest_kernel,
        grid=(batch, tps),
        in_specs=[
            pl.BlockSpec((tm, D_MODEL), lambda b, t: (b * tps + t, 0)),
            _resident((1, D_MODEL), lambda b, t: (0, 0)),
            _resident(w_rest.shape, lambda b, t: (0, 0)),
            pl.BlockSpec((tm, RET_DK), lambda b, t: (t, 0)),
            pl.BlockSpec((tm, RET_DK), lambda b, t: (t, 0)),
        ] + cast_specs + [
            pl.BlockSpec(memory_space=pltpu.SMEM),
            pl.BlockSpec((side_rows, 1, IN_COLS), lambda b, t: (b * tps + t, 0, 0)),
            full(cos_s), full(sin_s), full(gn),
            pl.BlockSpec(st_block, lambda b, t: (layer * steps + b * tps + t, 0, 0, 0)),
        ],
        out_specs=[spec for _, spec in outs] + cast_specs + [spec for _, spec in side_outs],
        out_shape=([shp for shp, _ in outs] + [jax.ShapeDtypeStruct(w.shape, BF16) for w in (w_up, w_down)]
                   + [shp for shp, _ in side_outs]),
        compiler_params=_cparams(2, 56),
        name="in_proj_rest",
    )(x2d, g, w_rest, cos_t, sin_t, w_up, w_down,
      log_g, z.reshape(db, 1, IN_COLS), cos_s, sin_s, gn, state.reshape((state.shape[0] * db,) + state.shape[2:]))
    results = list(results)
    return results[:-2] + [results[-2].reshape(db, RET_V), results[-1]]


def _in_proj_sample_kernel(x_ref, g_ref, w_ref, z_ref, w_att_ref, w_rest_ref, h_ref):
    j = pl.program_id(0)

    @pl.when(j == 0)
    def _():
        h_ref[...] = _norm_rows(x_ref, g_ref)

    w_bf = w_ref[...].astype(BF16)

    @pl.when(j < N_ATT_TILES)
    def _():
        w_att_ref[...] = w_bf

    @pl.when(j >= N_ATT_TILES)
    def _():
        w_rest_ref[...] = w_bf

    z_ref[...] = _dot(h_ref[...], w_bf)


def _in_proj_sample(x2d, g, w):
    m = x2d.shape[0]
    col_tile = lambda rows, f: pl.BlockSpec((rows, TN), lambda j: (0, f(j)))
    n_rest = N_COL_TILES - N_ATT_TILES
    return pl.pallas_call(
        _in_proj_sample_kernel,
        grid=(N_COL_TILES,),
        in_specs=[
            _resident((m, D_MODEL), lambda j: (0, 0)),
            _resident((1, D_MODEL), lambda j: (0, 0)),
            col_tile(D_MODEL, lambda j: j),
        ],
        out_specs=[col_tile(m, lambda j: j),
                   col_tile(D_MODEL, lambda j: jnp.minimum(j, N_ATT_TILES - 1)),
                   col_tile(D_MODEL, lambda j: jnp.maximum(j - N_ATT_TILES, 0))],
        out_shape=[jax.ShapeDtypeStruct((m, IN_COLS), F32),
                   jax.ShapeDtypeStruct((D_MODEL, N_ATT_TILES * TN), BF16),
                   jax.ShapeDtypeStruct((D_MODEL, n_rest * TN), BF16)],
        scratch_shapes=[pltpu.VMEM((m, D_MODEL), BF16)],
        compiler_params=_cparams(1, 24),
        name="in_proj_sample",
    )(x2d, g, w)


def _softmax_block(s, v):
    m = jnp.max(s, axis=1, keepdims=True)
    p = jnp.exp(s - m).astype(BF16)
    uv = _dot(p, jnp.concatenate([v, jnp.ones_like(v)], axis=1))
    return uv[:, :HEAD_DIM], jnp.broadcast_to(m, (s.shape[0], HEAD_DIM)), uv[:, HEAD_DIM:]


ATT_BLOCKS_PER_STEP = 16
ATT_ORDER = 4


def _attn_prompt_kernel(q0, q1, q2, k0, k1, k2, v0, v1, v2, o_ref,
                        u0, u1, u2, m0, m1, m2, l0, l1, l2, *, seq):
    qs, ks, vs = (q0, q1, q2), (k0, k1, k2), (v0, v1, v2)
    us, ms, ls = (u0, u1, u2), (m0, m1, m2), (l0, l1, l2)
    order = ATT_ORDER
    n_out = seq // order
    row1 = lax.broadcasted_iota(jnp.int32, (BAND, BAND), 0)
    col1 = lax.broadcasted_iota(jnp.int32, (BAND, BAND), 1)
    causal = col1 <= row1
    row2 = lax.broadcasted_iota(jnp.int32, (BAND, 2 * BAND), 0)
    col2 = lax.broadcasted_iota(jnp.int32, (BAND, 2 * BAND), 1)
    band = (col2 >= row2) & (col2 <= row2 + BAND)

    aligned = lambda r: r if isinstance(r, int) else pl.multiple_of(r, BAND)

    for g, (_, dil) in enumerate(DIL_GROUPS):
        n = seq // dil
        nb = n // BAND

        def run_blocks(blocks, g=g, n=n, dil=dil):
            scores, values = [], []
            for c, blk, has_prev in blocks:
                own = pl.ds(aligned(blk * BAND), BAND)
                keys = pl.ds(aligned((blk - 1) * BAND), 2 * BAND) if has_prev else own
                mask = band if has_prev else causal
                scores.append(jnp.where(mask, _dot_nt(qs[g][c, own, :], ks[g][c, keys, :]), NEG_INF))
                values.append(vs[g][c, keys, :])
            results = [_softmax_block(s, v) for s, v in zip(scores, values)]
            for (c, blk, _), (u, m, l) in zip(blocks, results):
                if dil <= order:
                    rs = pl.ds(aligned(c * n + blk * BAND), BAND)
                else:
                    step = dil // order
                    rs = pl.ds((c % order) * n_out + c // order + blk * (BAND * step), BAND, stride=step)
                us[g][rs, :] = u
                ms[g][rs, :] = m
                ls[g][rs, :] = l

        per = ATT_BLOCKS_PER_STEP
        assert (dil * nb) % per == 0 and (per % nb == 0 or nb % per == 0)
        if nb >= per:
            def body(i, carry, run_blocks=run_blocks, steps=nb // per):
                c, i0 = i // steps, (i % steps) * per
                run_blocks([(c, i0 + e, True) for e in range(per)])
                return carry
            for c in range(dil):
                run_blocks([(c, e, e > 0) for e in range(per)])
                lax.fori_loop(c * (nb // per) + 1, (c + 1) * (nb // per), body, 0)
        else:
            def body(i, carry, run_blocks=run_blocks, nb=nb, cps=per // nb):
                run_blocks([(i * cps + e // nb, e % nb, e % nb > 0) for e in range(per)])
                return carry
            lax.fori_loop(0, dil * nb // per, body, 0)

    chunks = n_out // BAND

    def combine_one(j):
        c, t = j // chunks, j % chunks
        rows = []
        for _, dil in DIL_GROUPS:
            r = min(dil, order)
            step = order // r
            start = (c % r) * (seq // r) + c // r + t * (BAND * step)
            rows.append(pl.ds(aligned(start), BAND) if step == 1 else pl.ds(start, BAND, stride=step))
        m_g = [ms[g][rows[g], :] for g in range(N_GROUPS)]
        m = jnp.maximum(jnp.maximum(m_g[0], m_g[1]), m_g[2])
        num = jnp.zeros((BAND, HEAD_DIM), F32)
        den = jnp.zeros((BAND, HEAD_DIM), F32)
        for g in range(N_GROUPS):
            w = jnp.exp(m_g[g] - m)
            num = num + w * us[g][rows[g], :]
            den = den + w * ls[g][rows[g], :]
        return num / den

    per = 4
    assert (order * chunks) % per == 0

    def combine(i, carry):
        outs = [combine_one(i * per + e) for e in range(per)]
        for e, o in enumerate(outs):
            o_ref[pl.ds(aligned((i * per + e) * BAND), BAND), :] = o
        return carry

    lax.fori_loop(0, order * chunks // per, combine, 0)


def _attn_prompt(qkv, batch, seq):
    assert seq % (BAND * DIL_GROUPS[-1][1]) == 0
    assert all(ATT_ORDER % d == 0 or d % ATT_ORDER == 0 for _, d in DIL_GROUPS)
    in_specs = [pl.BlockSpec((None, None) + a.shape[2:], lambda b, h: (b, h, 0, 0, 0)) for a in qkv]
    out = pl.pallas_call(
        functools.partial(_attn_prompt_kernel, seq=seq),
        grid=(batch, HEADS),
        in_specs=in_specs,
        out_specs=pl.BlockSpec((None, None, seq, HEAD_DIM), lambda b, h: (b, h, 0, 0)),
        out_shape=jax.ShapeDtypeStruct((batch, HEADS, seq, HEAD_DIM), F32),
        scratch_shapes=[pltpu.VMEM((seq, HEAD_DIM), F32)] * 9,
        compiler_params=_cparams(2, 32),
        name="attend_prompt",
    )(*qkv)
    return out.reshape(batch, HEADS, ATT_ORDER, seq // ATT_ORDER, HEAD_DIM)


def _group_norm_gate(o, gn, gate):
    mu = jnp.mean(o, axis=-1, keepdims=True)
    d = o - mu
    var = jnp.mean(d * d, axis=-1, keepdims=True)
    return gate * (d * lax.rsqrt(var + EPS) * gn)


RET_CHUNKS_PER_STEP = 8


def _ret_prompt_kernel(lg_ref, q_ref, k_ref, v_ref, gate_ref, gn_ref, o_ref, s_out_ref, s_scr):
    c = RET_CHUNK
    rows = q_ref.shape[1]
    t = lax.broadcasted_iota(jnp.int32, (c, 1), 0).astype(F32)
    rel = (lax.broadcasted_iota(jnp.int32, (c, c), 0) - lax.broadcasted_iota(jnp.int32, (c, c), 1)).astype(F32)

    @pl.when(pl.program_id(1) == 0)
    def _():
        s_scr[...] = jnp.zeros_like(s_scr)

    consts = []
    for h in range(RET_HEADS):
        lg = lg_ref[h]
        consts.append((
            jnp.where(rel >= 0, jnp.exp(lg * jnp.maximum(rel, 0.0)), 0.0),
            jnp.exp(lg * (t + 1.0)),
            jnp.exp(lg * (c - 1.0 - t)),
            jnp.exp(lg * jnp.full((1, RET_DV), float(c), F32)),
        ))

    per = RET_CHUNKS_PER_STEP
    assert (rows // c) % per == 0

    def step(i, carry):
        rss = [pl.ds(pl.multiple_of((i * per + e) * c, c), c) for e in range(per)]
        local = {}
        for h in range(RET_HEADS):
            decay, inner, tail, _ = consts[h]
            for e, rs in enumerate(rss):
                q, k, vb = q_ref[h, rs, :], k_ref[h, rs, :], v_ref[h, rs, :]
                a = _dot_nt(q.astype(BF16), k.astype(BF16)) * decay
                local[h, e] = (_dot(a.astype(BF16), vb),
                               (q * inner).astype(BF16),
                               _dot((k * tail).T.astype(BF16), vb))
        outs = {}
        for h in range(RET_HEADS):
            s = s_scr[h]
            for e in range(per):
                intra, q_in, kv = local[h, e]
                outs[h, e] = intra + _dot(q_in, s.astype(BF16))
                s = consts[h][3] * s + kv
            s_scr[h] = s
        for h in range(RET_HEADS):
            gn = gn_ref[:, h * RET_DV:(h + 1) * RET_DV]
            for e, rs in enumerate(rss):
                o_ref[h, rs, :] = _group_norm_gate(outs[h, e], gn, gate_ref[h, rs, :]).astype(o_ref.dtype)
        return carry

    lax.fori_loop(0, rows // (c * per), step, 0)

    @pl.when(pl.program_id(1) == pl.num_programs(1) - 1)
    def _():
        s_out_ref[...] = s_scr[...]


def _ret_prompt(log_g, rq, rk, rv, gsw, gn, batch, seq, rows):
    assert seq % rows == 0 and rows % RET_CHUNK == 0
    hm = lambda width: pl.BlockSpec((None, RET_HEADS, rows, width), lambda b, t: (b, 0, t, 0))
    return pl.pallas_call(
        _ret_prompt_kernel,
        grid=(batch, seq // rows),
        in_specs=[
            pl.BlockSpec(memory_space=pltpu.SMEM),
            hm(RET_DK), hm(RET_DK), hm(RET_DV), hm(RET_DV),
            _resident((1, RET_V), lambda b, t: (0, 0)),
        ],
        out_specs=[
            hm(RET_DV),
            pl.BlockSpec((None, RET_HEADS, RET_DK, RET_DV), lambda b, t: (b, 0, 0, 0)),
        ],
        out_shape=[
            jax.ShapeDtypeStruct((batch, RET_HEADS, seq, RET_DV), BF16),
            jax.ShapeDtypeStruct((batch, RET_HEADS, RET_DK, RET_DV), F32),
        ],
        scratch_shapes=[pltpu.VMEM((RET_HEADS, RET_DK, RET_DV), F32)],
        compiler_params=_cparams(2, 40),
        name="retain_prompt",
    )(log_g, rq, rk, rv, gsw, gn)


def _sample_attend_row(z_ref, caches, r):
    half = BAND // 2

    def heads(tile0, g):
        base = (tile0 + g) * TN
        return jnp.concatenate(
            [z_ref[r, :, base + hh * HEAD_DIM:base + (hh + 1) * HEAD_DIM] for hh in range(HEADS)], axis=0)

    def paired(ref, lo):
        return jnp.concatenate([ref[r, 0:half, lo:lo + HEADS, :], ref[r, half:BAND, lo:lo + HEADS, :]], axis=1)

    twice = lambda a: jnp.concatenate([a, a], axis=0)
    fold = lambda a, op: op(a[0:HEADS], a[HEADS:2 * HEADS])

    us, ms, ls = [], [], []
    for g in range(N_GROUPS):
        q = heads(COL_Q, g) * (HEAD_DIM ** -0.5)
        k_new, v_new = heads(COL_K, g), heads(COL_V, g)
        k_buf, v_buf = paired(caches[g], 0), paired(caches[g], HEADS)
        s_buf = jnp.sum(k_buf * twice(q)[None], axis=-1, keepdims=True)
        s_new = jnp.sum(k_new * q, axis=-1, keepdims=True)
        m = jnp.maximum(fold(jnp.max(s_buf, axis=0), jnp.maximum), s_new)
        p_buf = jnp.exp(s_buf - twice(m)[None])
        p_new = jnp.exp(s_new - m)
        ls.append(fold(jnp.sum(p_buf, axis=0), jnp.add) + p_new)
        us.append(fold(jnp.sum(p_buf * v_buf, axis=0), jnp.add) + p_new * v_new)
        ms.append(m)
    m = jnp.maximum(jnp.maximum(ms[0], ms[1]), ms[2])
    num = jnp.zeros((HEADS, HEAD_DIM), F32)
    den = jnp.zeros((HEADS, 1), F32)
    for g in range(N_GROUPS):
        w = jnp.exp(ms[g] - m)
        num = num + w * us[g]
        den = den + w * ls[g]
    return num / den


def _sample_retain_head(lg_ref, z_ref, cos, sin, gn_ref, s_ref, ret_ref, s_out_ref, h):
    rows = z_ref.shape[0]
    eye = (lax.broadcasted_iota(jnp.int32, (RET_DK, RET_DK), 0)
           == lax.broadcasted_iota(jnp.int32, (RET_DK, RET_DK), 1))

    def column(v):
        return jnp.sum(jnp.where(eye, jnp.broadcast_to(v, (RET_DK, RET_DK)), 0.0), axis=1, keepdims=True)

    def zcols(tile0, width):
        return z_ref[:, 0, tile0 * TN + h * width:tile0 * TN + (h + 1) * width]

    gamma = jnp.exp(lg_ref[h] * jnp.ones((1, RET_DV), F32))
    q_h = _rope(zcols(COL_RQ, RET_DK), cos, sin)
    k_h = _rope(zcols(COL_RK, RET_DK), cos, sin) * (RET_DK ** -0.5)
    v_h = zcols(COL_RV, RET_DV)
    outs = []
    for r in range(rows):
        s_new = gamma * s_ref[r, h] + column(k_h[r:r + 1, :]) * v_h[r:r + 1, :]
        s_out_ref[r, h] = s_new
        outs.append(jnp.sum(column(q_h[r:r + 1, :]) * s_new, axis=0, keepdims=True))
    seg = slice(h * RET_DV, (h + 1) * RET_DV)
    ret_ref[:, 0, seg] = _group_norm_gate(jnp.concatenate(outs, axis=0), gn_ref[:, seg], _silu(zcols(COL_GSW, RET_DV)))


FF_CHUNK = 1024


def _tail_kernel(att_ref, ret_ref, ga_ref, gr_ref, x_ref, wa_ref, wr_ref, wo_ref, gpost_mix_ref, gpre_mlp_ref,
                 wu_ref, wd_ref, gpost_mlp_ref, y_ref, att_scr):
    r = att_ref.shape[1]
    for hh in range(HEADS):
        for c in range(r):
            rs = pl.ds(c, att_ref.shape[2], stride=r) if r > 1 else slice(None)
            att_scr[hh, rs, :] = att_ref[hh, c]
    att = jnp.concatenate([att_scr[hh].astype(BF16) for hh in range(HEADS)], axis=1)
    ret = jnp.concatenate([ret_ref[hh].astype(BF16) for hh in range(RET_HEADS)], axis=1)
    m = _sigmoid(ga_ref[...]) * _dot(att, wa_ref[...]) + _sigmoid(gr_ref[...]) * _dot(ret, wr_ref[...])
    x1 = x_ref[...] + _rms(_dot(m.astype(BF16), wo_ref[...])) * gpost_mix_ref[...]
    h = (_rms(x1) * gpre_mlp_ref[...]).astype(BF16)

    n_chunks = D_FF // FF_CHUNK
    acc = jnp.zeros(x_ref.shape, F32)
    for c in range(n_chunks):
        cs = slice(c * FF_CHUNK, (c + 1) * FF_CHUNK)
        u = jnp.maximum(_dot(h, wu_ref[:, cs]), 0.0)
        acc = acc + _dot((u * u).astype(BF16), wd_ref[cs, :])
    y_ref[...] = x1 + _rms(acc) * gpost_mlp_ref[...]


def _tail(att, ret, ga, gr, x2d, lw, tm):
    batch, _, r, ni, _ = att.shape
    seq = r * ni
    m = batch * seq
    assert seq % tm == 0 and tm % r == 0 and x2d.shape[0] == m
    tps = seq // tm
    steps = batch * tps
    rows = pl.BlockSpec((tm, D_MODEL), lambda i: (i, 0))
    full = lambda a: _resident(a.shape, lambda i: (0,) * a.ndim)
    weights = [lw[k] for k in ("w_att_br", "w_ret_br", "w_out", "g_post_mix", "g_pre_mlp",
                               "w_up", "w_down", "g_post_mlp")]
    args = [att, ret, ga, gr, x2d] + weights
    in_specs = [pl.BlockSpec((None, HEADS, r, tm // r, HEAD_DIM), lambda i: (i // tps, 0, 0, i % tps, 0)),
                pl.BlockSpec((None, RET_HEADS, tm, RET_DV), lambda i: (i // tps, 0, i % tps, 0)),
                rows, rows, rows] + [full(w) for w in weights]
    return pl.pallas_call(
        _tail_kernel,
        grid=(steps,),
        in_specs=in_specs,
        out_specs=rows,
        out_shape=jax.ShapeDtypeStruct((m, D_MODEL), F32),
        scratch_shapes=[pltpu.VMEM((HEADS, tm, HEAD_DIM), F32)],
        compiler_params=_cparams(1, 56),
        name="tail",
    )(*args)


def _rope_tables(pos):
    half = RET_DK // 2
    inv = ROPE_BASE ** (-jnp.arange(half, dtype=F32) / half)
    ang = pos.astype(F32)[:, None] * inv[None, :]
    cos, sin = jnp.cos(ang), jnp.sin(ang)
    return jnp.concatenate([cos, cos], axis=-1), jnp.concatenate([-sin, sin], axis=-1)


IN_PROJ_TM = 512
TAIL_TM = 512
RET_ROWS = 1024


def _layer(xp, xs, caches, state, layer, lw, log_g):
    batch, seq, _ = xp.shape
    db, t, _ = xs.shape
    assert t == 1
    xp2d, xs2d = xp.reshape(batch * seq, D_MODEL), xs.reshape(db, D_MODEL)

    z, w_att, w_rest = _in_proj_sample(xs2d, lw["g_pre_mix"], lw["w_in"])

    cos_p, sin_p = _rope_tables(jnp.arange(seq, dtype=jnp.int32))
    cos_s, sin_s = _rope_tables(PAST_LEN + jnp.arange(1, dtype=jnp.int32))
    outs = _in_proj_att(xp2d, lw["g_pre_mix"], w_att, z, caches, layer, batch, seq, IN_PROJ_TM)
    qkv, att_s = outs[:9], outs[12]
    rows_p = [kv.reshape(batch, -1, 2, HEADS, HEAD_DIM) for kv in outs[9:12]]
    rq, rk, rv, gsw, ga, gr, w_up_bf, w_down_bf, ret_s, state_s = _in_proj_rest(
        xp2d, lw["g_pre_mix"], w_rest, cos_p, sin_p, lw["w_up"], lw["w_down"],
        (log_g, z, cos_s, sin_s, lw["g_ret_norm"], state, layer), batch, seq, IN_PROJ_TM)
    lw = dict(lw, w_up=w_up_bf, w_down=w_down_bf)
    att = _attn_prompt(qkv, batch, seq)
    ret, state_p = _ret_prompt(log_g, rq, rk, rv, gsw, lw["g_ret_norm"], batch, seq, RET_ROWS)
    yp = _tail(att, ret, ga, gr, xp2d, lw, TAIL_TM)

    att_s = jnp.transpose(att_s, (1, 0, 2)).reshape(1, HEADS, 1, db, HEAD_DIM)
    ret_s = jnp.transpose(ret_s.reshape(db, RET_HEADS, RET_DV), (1, 0, 2))[None]
    ga_s = z[:, COL_GA * TN:COL_GA * TN + D_MODEL]
    gr_s = z[:, COL_GR * TN:COL_GR * TN + D_MODEL]
    ys = _tail(att_s, ret_s, ga_s, gr_s, xs2d, lw, db)
    rows_s = []
    for g in range(N_GROUPS):
        k_new = z[:, (COL_K + g) * TN:(COL_K + g + 1) * TN]
        v_new = z[:, (COL_V + g) * TN:(COL_V + g + 1) * TN]
        rows_s.append(jnp.stack([k_new, v_new], axis=1).reshape(db, 1, 2, HEADS, HEAD_DIM))
    return (yp.reshape(batch, seq, D_MODEL), rows_p, state_p), (ys.reshape(db, 1, D_MODEL), rows_s, state_s)


def _stack(xs):
    return xs[0][None] if len(xs) == 1 else jnp.stack(xs, axis=0)


def kernel(x_prompt, x_sample, cache_kv_d1, cache_kv_d4, cache_kv_d16, state_ret, w_in, w_att_br, w_ret_br, w_out, w_up, w_down, g_ret_norm, g_pre_mix, g_post_mix, g_pre_mlp, g_post_mlp):
    depth = w_in.shape[0]
    log_g = jnp.log1p(-jnp.power(2.0, -5.0 - jnp.arange(RET_HEADS, dtype=F32)))
    caches = (cache_kv_d1, cache_kv_d4, cache_kv_d16)
    xp, xs = x_prompt, x_sample
    p_rows, s_rows = [[], [], []], [[], [], []]
    p_states, s_states = [], []
    for l in range(depth):
        lw = {
            "w_in": w_in[l], "w_up": w_up[l], "w_down": w_down[l],
            "w_att_br": w_att_br[l].astype(BF16), "w_ret_br": w_ret_br[l].astype(BF16),
            "w_out": w_out[l].astype(BF16),
            "g_ret_norm": g_ret_norm[l].reshape(1, RET_V), "g_pre_mix": g_pre_mix[l].reshape(1, D_MODEL),
            "g_post_mix": g_post_mix[l].reshape(1, D_MODEL), "g_pre_mlp": g_pre_mlp[l].reshape(1, D_MODEL),
            "g_post_mlp": g_post_mlp[l].reshape(1, D_MODEL),
        }
        (xp, rows_p, sp), (xs, rows_s, ss) = _layer(xp, xs, caches, state_ret, l, lw, log_g)
        for g in range(N_GROUPS):
            p_rows[g].append(rows_p[g])
            s_rows[g].append(rows_s[g])
        p_states.append(sp)
        s_states.append(ss)
    return (xp, xs, _stack(p_rows[0]), _stack(p_rows[1]), _stack(p_rows[2]), _stack(p_states),
            _stack(s_rows[0]), _stack(s_rows[1]), _stack(s_rows[2]), _stack(s_states))
```

```python
import functools

import jax
import jax.numpy as jnp
from jax import lax
from jax.experimental import pallas as pl
from jax.experimental.pallas import tpu as pltpu

F32 = jnp.float32
BF16 = jnp.bfloat16

D_MODEL = 1024
PAST_LEN = 8192

DIL_GROUPS = ((128, 1), (512, 4), (2048, 16))
N_GROUPS = 3
HEADS = 4
HEAD_DIM = 128
ATT_OUT = HEADS * HEAD_DIM
ATT_COLS = N_GROUPS * ATT_OUT
BAND = 128

RET_HEADS = 4
RET_DK = 128
RET_DV = 256
RET_QK = RET_HEADS * RET_DK
RET_V = RET_HEADS * RET_DV
RET_CHUNK = 128
ROPE_BASE = 10000.0

D_FF = 4 * D_MODEL
IN_COLS = 3 * ATT_COLS + 2 * RET_QK + 2 * RET_V + 2 * D_MODEL
EPS = 1e-6
NEG_INF = -1e30

TN = 512
N_COL_TILES = IN_COLS // TN
COL_Q, COL_K, COL_V, COL_RQ, COL_RK, COL_RV, COL_GSW, COL_GA, COL_GR = 0, 3, 6, 9, 10, 11, 13, 15, 17

V7X_VMEM_BYTES = 64 * 1024 * 1024
MIB = 1024 * 1024


def _cparams(n_axes, vmem_mib):
    assert vmem_mib * MIB < V7X_VMEM_BYTES
    return pltpu.CompilerParams(
        dimension_semantics=("arbitrary",) * n_axes,
        vmem_limit_bytes=vmem_mib * MIB,
    )


def _resident(shape, index_map):
    return pl.BlockSpec(shape, index_map, pipeline_mode=pl.Buffered(1))


def _rms(x):
    return x * lax.rsqrt(jnp.mean(x * x, axis=-1, keepdims=True) + EPS)


def _sigmoid(x):
    return 1.0 / (1.0 + jnp.exp(-x))


def _silu(x):
    return x * _sigmoid(x)


def _dot(a, b):
    return jnp.dot(a, b, preferred_element_type=F32)


def _dot_nt(a, b):
    return lax.dot_general(a, b, (((1,), (1,)), ((), ())), preferred_element_type=F32)


def _rope(x, cos, sin):
    return x * cos + pltpu.roll(x, RET_DK // 2, 1) * sin


N_ATT_TILES = 3 * N_GROUPS


def _norm_rows(x_ref, g_ref):
    return (_rms(x_ref[...]) * g_ref[...]).astype(BF16)


def _head(acc, hh, width=HEAD_DIM):
    return acc[:, hh * width:(hh + 1) * width]


def _in_proj_att_kernel(x_ref, g_ref, w_ref, wu_ref, wd_ref,
                        q0, q1, q2, k0, k1, k2, v0, v1, v2, kvo0, kvo1, kvo2, wu_bf_ref, wd_bf_ref,
                        scr_q, sk0, sk1, sk2, sv0, sv1, sv2, *, first_kept):
    tm = x_ref.shape[0]
    h = _norm_rows(x_ref, g_ref)
    wu_bf_ref[...] = wu_ref[...].astype(BF16)
    wd_bf_ref[...] = wd_ref[...].astype(BF16)

    def park(j, scr, scale=None):
        acc = _dot(h, w_ref[:, j * TN:(j + 1) * TN])
        for hh in range(HEADS):
            scr[hh] = _head(acc, hh) if scale is None else _head(acc, hh) * scale

    def residues(scr, dst, dil):
        for hh in range(HEADS):
            for c in range(dil):
                rows = pl.ds(c, tm // dil, stride=dil) if dil > 1 else slice(None)
                dst[hh, c] = scr[hh, rows, :].astype(BF16)

    for g, (_, dil) in enumerate(DIL_GROUPS):
        park(COL_Q + g, scr_q, HEAD_DIM ** -0.5)
        residues(scr_q, (q0, q1, q2)[g], dil)
        park(COL_K + g, (sk0, sk1, sk2)[g])
        residues((sk0, sk1, sk2)[g], (k0, k1, k2)[g], dil)
        park(COL_V + g, (sv0, sv1, sv2)[g])
        residues((sv0, sv1, sv2)[g], (v0, v1, v2)[g], dil)

    for g, kvo in enumerate((kvo0, kvo1, kvo2)):
        def write(g=g, kvo=kvo):
            keep = kvo.shape[0] // (2 * HEADS)
            for which, scr in enumerate(((sk0, sk1, sk2)[g], (sv0, sv1, sv2)[g])):
                for hh in range(HEADS):
                    kvo[pl.ds(which * HEADS + hh, keep, stride=2 * HEADS), :] = scr[hh, tm - keep:, :]
        if first_kept[g] == 0:
            write()
        else:
            pl.when(pl.program_id(1) >= first_kept[g])(write)


def _in_proj_att(x2d, g, w_att, w_up, w_down, batch, seq, tm):
    assert seq % tm == 0 and tm % (16 * DIL_GROUPS[-1][1]) == 0
    tps = seq // tm
    out_shape, out_specs, first_kept = [], [], []
    for _ in range(3):
        for _, dil in DIL_GROUPS:
            out_shape.append(jax.ShapeDtypeStruct((batch, HEADS, dil, seq // dil, HEAD_DIM), BF16))
            out_specs.append(pl.BlockSpec((None, HEADS, dil, tm // dil, HEAD_DIM), lambda b, t: (b, 0, 0, t, 0)))
    for win, _ in DIL_GROUPS:
        keep = min(win, seq)
        rows = min(keep, tm)
        assert keep % rows == 0
        first = (seq - keep) // rows
        first_kept.append((seq - keep) // tm)
        out_shape.append(jax.ShapeDtypeStruct((batch, keep * 2 * HEADS, HEAD_DIM), F32))
        out_specs.append(pl.BlockSpec(
            (None, rows * 2 * HEADS, HEAD_DIM),
            functools.partial(lambda b, t, first, per: (b, jnp.maximum((t + 1) * per - 1 - first, 0), 0),
                              first=first, per=tm // rows)))
    steps = batch * tps
    cast_rows = [w.shape[0] // steps for w in (w_up, w_down)]
    assert all(w.shape[0] % steps == 0 and r % 16 == 0 for w, r in zip((w_up, w_down), cast_rows))
    cast_specs = [pl.BlockSpec((r, w.shape[1]), lambda b, t: (b * tps + t, 0))
                  for w, r in zip((w_up, w_down), cast_rows)]
    return pl.pallas_call(
        functools.partial(_in_proj_att_kernel, first_kept=tuple(first_kept)),
        grid=(batch, tps),
        in_specs=[
            pl.BlockSpec((tm, D_MODEL), lambda b, t: (b * tps + t, 0)),
            _resident((1, D_MODEL), lambda b, t: (0, 0)),
            _resident(w_att.shape, lambda b, t: (0, 0)),
        ] + cast_specs,
        out_specs=out_specs + cast_specs,
        out_shape=out_shape + [jax.ShapeDtypeStruct(w.shape, BF16) for w in (w_up, w_down)],
        scratch_shapes=[pltpu.VMEM((HEADS, tm, HEAD_DIM), F32)] * (1 + 2 * N_GROUPS),
        compiler_params=_cparams(2, 48),
        name="in_proj_att",
    )(x2d, g, w_att, w_up, w_down)


def _in_proj_rest_kernel(x_ref, g_ref, w_ref, cos_ref, sin_ref,
                         lg_ref, z_ref, cos_s_ref, sin_s_ref, gn_ref, s_ref, c0, c1, c2,
                         rq_ref, rk_ref, rv_ref, gsw_ref, ga_ref, gr_ref, ret_s_ref, s_out_ref, att_s_ref):
    h = _norm_rows(x_ref, g_ref)
    rows_left = list(range(z_ref.shape[0]))

    def side_head(hh):
        _sample_retain_head(lg_ref, z_ref, cos_s_ref[...], sin_s_ref[...], gn_ref, s_ref, ret_s_ref, s_out_ref, hh)

    def side_row():
        if rows_left:
            r = rows_left.pop(0)
            att_s_ref[r] = _sample_attend_row(z_ref, (c0, c1, c2), r)

    def tile(j):
        lo = (j - N_ATT_TILES) * TN
        return _dot(h, w_ref[:, lo:lo + TN])

    def ret_qk(col, dst, scale):
        acc = tile(col)
        cos, sin = cos_ref[...], sin_ref[...]
        for hh in range(RET_HEADS):
            r = _rope(_head(acc, hh), cos, sin)
            dst[hh] = r if scale is None else r * scale

    def ret_wide(col, dst, e, act=None):
        acc = tile(col + e)
        for s in range(2):
            part = _head(acc, s, RET_DV)
            dst[2 * e + s] = (part if act is None else act(part)).astype(dst.dtype)

    def gate(col, dst, e):
        dst[:, e * TN:(e + 1) * TN] = tile(col + e)

    ret_qk(COL_RQ, rq_ref, None)
    side_row()
    ret_qk(COL_RK, rk_ref, RET_DK ** -0.5)
    side_row()
    for e in range(2):
        ret_wide(COL_RV, rv_ref, e)
        side_head(2 * e)
        ret_wide(COL_GSW, gsw_ref, e, _silu)
        side_row()
        gate(COL_GA, ga_ref, e)
        side_head(2 * e + 1)
        gate(COL_GR, gr_ref, e)
    while rows_left:
        side_row()


def _in_proj_rest(x2d, g, w_rest, cos_t, sin_t, side, batch, seq, tm):
    assert seq % tm == 0
    tps = seq // tm
    steps = batch * tps
    log_g, z, cos_s, sin_s, gn, state, caches, layer = side
    db = z.shape[0]
    assert db % steps == 0 and RET_HEADS == 4
    side_rows = db // steps
    st_block = (side_rows, RET_HEADS, RET_DK, RET_DV)
    full = lambda a: _resident(a.shape, lambda b, t: (0,) * a.ndim)
    views, cache_specs = [], []
    for (win, dil), c in zip(DIL_GROUPS, caches):
        assert c.shape[1] == db and c.shape[2] == win and win // dil == BAND
        views.append(c.reshape(c.shape[0] * db, BAND, dil, 2 * HEADS, HEAD_DIM))
        cache_specs.append(pl.BlockSpec((side_rows, BAND, None, 2 * HEADS, HEAD_DIM),
                                        lambda b, t: (layer * steps + b * tps + t, 0, 0, 0, 0)))
    hm = lambda width, dt: (jax.ShapeDtypeStruct((batch, RET_HEADS, seq, width), dt),
                            pl.BlockSpec((None, RET_HEADS, tm, width), lambda b, t: (b, 0, t, 0)))
    nat = (jax.ShapeDtypeStruct((batch * seq, D_MODEL), F32),
           pl.BlockSpec((tm, D_MODEL), lambda b, t: (b * tps + t, 0)))
    outs = (hm(RET_DK, F32), hm(RET_DK, F32), hm(RET_DV, BF16), hm(RET_DV, F32), nat, nat)
    side_outs = ((jax.ShapeDtypeStruct((db, 1, RET_V), F32),
                  pl.BlockSpec((side_rows, 1, RET_V), lambda b, t: (b * tps + t, 0, 0))),
                 (jax.ShapeDtypeStruct((db,) + st_block[1:], F32),
                  pl.BlockSpec(st_block, lambda b, t: (b * tps + t, 0, 0, 0))),
                 (jax.ShapeDtypeStruct((db, HEADS, HEAD_DIM), F32),
                  pl.BlockSpec((side_rows, HEADS, HEAD_DIM), lambda b, t: (b * tps + t, 0, 0))))
    results = pl.pallas_call(
        _in_proj_rest_kernel,
        grid=(batch, tps),
        in_specs=[
            pl.BlockSpec((tm, D_MODEL), lambda b, t: (b * tps + t, 0)),
            _resident((1, D_MODEL), lambda b, t: (0, 0)),
            _resident(w_rest.shape, lambda b, t: (0, 0)),
            pl.BlockSpec((tm, RET_DK), lambda b, t: (t, 0)),
            pl.BlockSpec((tm, RET_DK), lambda b, t: (t, 0)),
            pl.BlockSpec(memory_space=pltpu.SMEM),
            pl.BlockSpec((side_rows, 1, IN_COLS), lambda b, t: (b * tps + t, 0, 0)),
            full(cos_s), full(sin_s), full(gn),
            pl.BlockSpec(st_block, lambda b, t: (layer * steps + b * tps + t, 0, 0, 0)),
        ] + cache_specs,
        out_specs=[spec for _, spec in outs + side_outs],
        out_shape=[shp for shp, _ in outs + side_outs],
        compiler_params=_cparams(2, 58),
        name="in_proj_rest",
    )(x2d, g, w_rest, cos_t, sin_t, log_g, z.reshape(db, 1, IN_COLS), cos_s, sin_s, gn,
      state.reshape((state.shape[0] * db,) + state.shape[2:]), *views)
    results = list(results)
    results[-3] = results[-3].reshape(db, RET_V)
    return results


def _in_proj_sample_kernel(x_ref, g_ref, w_ref, z_ref, w_att_ref, w_rest_ref, h_ref):
    j = pl.program_id(0)

    @pl.when(j == 0)
    def _():
        h_ref[...] = _norm_rows(x_ref, g_ref)

    w_bf = w_ref[...].astype(BF16)

    @pl.when(j < N_ATT_TILES)
    def _():
        w_att_ref[...] = w_bf

    @pl.when(j >= N_ATT_TILES)
    def _():
        w_rest_ref[...] = w_bf

    z_ref[...] = _dot(h_ref[...], w_bf)


def _in_proj_sample(x2d, g, w):
    m = x2d.shape[0]
    col_tile = lambda rows, f: pl.BlockSpec((rows, TN), lambda j: (0, f(j)))
    n_rest = N_COL_TILES - N_ATT_TILES
    return pl.pallas_call(
        _in_proj_sample_kernel,
        grid=(N_COL_TILES,),
        in_specs=[
            _resident((m, D_MODEL), lambda j: (0, 0)),
            _resident((1, D_MODEL), lambda j: (0, 0)),
            col_tile(D_MODEL, lambda j: j),
        ],
        out_specs=[col_tile(m, lambda j: j),
                   col_tile(D_MODEL, lambda j: jnp.minimum(j, N_ATT_TILES - 1)),
                   col_tile(D_MODEL, lambda j: jnp.maximum(j - N_ATT_TILES, 0))],
        out_shape=[jax.ShapeDtypeStruct((m, IN_COLS), F32),
                   jax.ShapeDtypeStruct((D_MODEL, N_ATT_TILES * TN), BF16),
                   jax.ShapeDtypeStruct((D_MODEL, n_rest * TN), BF16)],
        scratch_shapes=[pltpu.VMEM((m, D_MODEL), BF16)],
        compiler_params=_cparams(1, 24),
        name="in_proj_sample",
    )(x2d, g, w)


def _softmax_block(s, v):
    m = jnp.max(s, axis=1, keepdims=True)
    p = jnp.exp(s - m).astype(BF16)
    uv = _dot(p, jnp.concatenate([v, jnp.ones_like(v)], axis=1))
    return uv[:, :HEAD_DIM], jnp.broadcast_to(m, (s.shape[0], HEAD_DIM)), uv[:, HEAD_DIM:]


ATT_BLOCKS_PER_STEP = 16
ATT_ORDER = 4


def _attn_prompt_kernel(q0, q1, q2, k0, k1, k2, v0, v1, v2, o_ref,
                        u0, u1, u2, m0, m1, m2, l0, l1, l2, *, seq):
    qs, ks, vs = (q0, q1, q2), (k0, k1, k2), (v0, v1, v2)
    us, ms, ls = (u0, u1, u2), (m0, m1, m2), (l0, l1, l2)
    order = ATT_ORDER
    n_out = seq // order
    row1 = lax.broadcasted_iota(jnp.int32, (BAND, BAND), 0)
    col1 = lax.broadcasted_iota(jnp.int32, (BAND, BAND), 1)
    causal = col1 <= row1
    row2 = lax.broadcasted_iota(jnp.int32, (BAND, 2 * BAND), 0)
    col2 = lax.broadcasted_iota(jnp.int32, (BAND, 2 * BAND), 1)
    band = (col2 >= row2) & (col2 <= row2 + BAND)

    aligned = lambda r: r if isinstance(r, int) else pl.multiple_of(r, BAND)

    for g, (_, dil) in enumerate(DIL_GROUPS):
        n = seq // dil
        nb = n // BAND

        def run_blocks(blocks, g=g, n=n, dil=dil):
            scores, values = [], []
            for c, blk, has_prev in blocks:
                own = pl.ds(aligned(blk * BAND), BAND)
                keys = pl.ds(aligned((blk - 1) * BAND), 2 * BAND) if has_prev else own
                mask = band if has_prev else causal
                scores.append(jnp.where(mask, _dot_nt(qs[g][c, own, :], ks[g][c, keys, :]), NEG_INF))
                values.append(vs[g][c, keys, :])
            results = [_softmax_block(s, v) for s, v in zip(scores, values)]
            for (c, blk, _), (u, m, l) in zip(blocks, results):
                if dil <= order:
                    rs = pl.ds(aligned(c * n + blk * BAND), BAND)
                else:
                    step = dil // order
                    rs = pl.ds((c % order) * n_out + c // order + blk * (BAND * step), BAND, stride=step)
                us[g][rs, :] = u
                ms[g][rs, :] = m
                ls[g][rs, :] = l

        per = ATT_BLOCKS_PER_STEP
        assert (dil * nb) % per == 0 and (per % nb == 0 or nb % per == 0)
        if nb >= per:
            def body(i, carry, run_blocks=run_blocks, steps=nb // per):
                c, i0 = i // steps, (i % steps) * per
                run_blocks([(c, i0 + e, True) for e in range(per)])
                return carry
            for c in range(dil):
                run_blocks([(c, e, e > 0) for e in range(per)])
                lax.fori_loop(c * (nb // per) + 1, (c + 1) * (nb // per), body, 0)
        else:
            def body(i, carry, run_blocks=run_blocks, nb=nb, cps=per // nb):
                run_blocks([(i * cps + e // nb, e % nb, e % nb > 0) for e in range(per)])
                return carry
            lax.fori_loop(0, dil * nb // per, body, 0)

    chunks = n_out // BAND

    def combine_one(j):
        c, t = j // chunks, j % chunks
        rows = []
        for _, dil in DIL_GROUPS:
            r = min(dil, order)
            step = order // r
            start = (c % r) * (seq // r) + c // r + t * (BAND * step)
            rows.append(pl.ds(aligned(start), BAND) if step == 1 else pl.ds(start, BAND, stride=step))
        m_g = [ms[g][rows[g], :] for g in range(N_GROUPS)]
        m = jnp.maximum(jnp.maximum(m_g[0], m_g[1]), m_g[2])
        num = jnp.zeros((BAND, HEAD_DIM), F32)
        den = jnp.zeros((BAND, HEAD_DIM), F32)
        for g in range(N_GROUPS):
            w = jnp.exp(m_g[g] - m)
            num = num + w * us[g][rows[g], :]
            den = den + w * ls[g][rows[g], :]
        return num / den

    per = 4
    assert (order * chunks) % per == 0

    def combine(i, carry):
        outs = [combine_one(i * per + e) for e in range(per)]
        for e, o in enumerate(outs):
            o_ref[pl.ds(aligned((i * per + e) * BAND), BAND), :] = o
        return carry

    lax.fori_loop(0, order * chunks // per, combine, 0)


def _attn_prompt(qkv, batch, seq):
    assert seq % (BAND * DIL_GROUPS[-1][1]) == 0
    assert all(ATT_ORDER % d == 0 or d % ATT_ORDER == 0 for _, d in DIL_GROUPS)
    in_specs = [pl.BlockSpec((None, None) + a.shape[2:], lambda b, h: (b, h, 0, 0, 0)) for a in qkv]
    out = pl.pallas_call(
        functools.partial(_attn_prompt_kernel, seq=seq),
        grid=(batch, HEADS),
        in_specs=in_specs,
        out_specs=pl.BlockSpec((None, None, seq, HEAD_DIM), lambda b, h: (b, h, 0, 0)),
        out_shape=jax.ShapeDtypeStruct((batch, HEADS, seq, HEAD_DIM), F32),
        scratch_shapes=[pltpu.VMEM((seq, HEAD_DIM), F32)] * 9,
        compiler_params=_cparams(2, 32),
        name="attend_prompt",
    )(*qkv)
    return out.reshape(batch, HEADS, ATT_ORDER, seq // ATT_ORDER, HEAD_DIM)


def _group_norm_gate(o, gn, gate):
    mu = jnp.mean(o, axis=-1, keepdims=True)
    d = o - mu
    var = jnp.mean(d * d, axis=-1, keepdims=True)
    return gate * (d * lax.rsqrt(var + EPS) * gn)


RET_CHUNKS_PER_STEP = 8


def _ret_prompt_kernel(lg_ref, q_ref, k_ref, v_ref, gate_ref, gn_ref, o_ref, s_out_ref, s_scr):
    c = RET_CHUNK
    rows = q_ref.shape[1]
    t = lax.broadcasted_iota(jnp.int32, (c, 1), 0).astype(F32)
    rel = (lax.broadcasted_iota(jnp.int32, (c, c), 0) - lax.broadcasted_iota(jnp.int32, (c, c), 1)).astype(F32)

    @pl.when(pl.program_id(1) == 0)
    def _():
        s_scr[...] = jnp.zeros_like(s_scr)

    consts = []
    for h in range(RET_HEADS):
        lg = lg_ref[h]
        consts.append((
            jnp.where(rel >= 0, jnp.exp(lg * jnp.maximum(rel, 0.0)), 0.0),
            jnp.exp(lg * (t + 1.0)),
            jnp.exp(lg * (c - 1.0 - t)),
            jnp.exp(lg * jnp.full((1, RET_DV), float(c), F32)),
        ))

    per = RET_CHUNKS_PER_STEP
    assert (rows // c) % per == 0

    def step(i, carry):
        rss = [pl.ds(pl.multiple_of((i * per + e) * c, c), c) for e in range(per)]
        local = {}
        for h in range(RET_HEADS):
            decay, inner, tail, _ = consts[h]
            for e, rs in enumerate(rss):
                q, k, vb = q_ref[h, rs, :], k_ref[h, rs, :], v_ref[h, rs, :]
                a = _dot_nt(q.astype(BF16), k.astype(BF16)) * decay
                local[h, e] = (_dot(a.astype(BF16), vb),
                               (q * inner).astype(BF16),
                               _dot((k * tail).T.astype(BF16), vb))
        outs = {}
        for h in range(RET_HEADS):
            s = s_scr[h]
            for e in range(per):
                intra, q_in, kv = local[h, e]
                outs[h, e] = intra + _dot(q_in, s.astype(BF16))
                s = consts[h][3] * s + kv
            s_scr[h] = s
        for h in range(RET_HEADS):
            gn = gn_ref[:, h * RET_DV:(h + 1) * RET_DV]
            for e, rs in enumerate(rss):
                o_ref[h, rs, :] = _group_norm_gate(outs[h, e], gn, gate_ref[h, rs, :]).astype(o_ref.dtype)
        return carry

    lax.fori_loop(0, rows // (c * per), step, 0)

    @pl.when(pl.program_id(1) == pl.num_programs(1) - 1)
    def _():
        s_out_ref[...] = s_scr[...]


def _ret_prompt(log_g, rq, rk, rv, gsw, gn, batch, seq, rows):
    assert seq % rows == 0 and rows % RET_CHUNK == 0
    hm = lambda width: pl.BlockSpec((None, RET_HEADS, rows, width), lambda b, t: (b, 0, t, 0))
    return pl.pallas_call(
        _ret_prompt_kernel,
        grid=(batch, seq // rows),
        in_specs=[
            pl.BlockSpec(memory_space=pltpu.SMEM),
            hm(RET_DK), hm(RET_DK), hm(RET_DV), hm(RET_DV),
            _resident((1, RET_V), lambda b, t: (0, 0)),
        ],
        out_specs=[
            hm(RET_DV),
            pl.BlockSpec((None, RET_HEADS, RET_DK, RET_DV), lambda b, t: (b, 0, 0, 0)),
        ],
        out_shape=[
            jax.ShapeDtypeStruct((batch, RET_HEADS, seq, RET_DV), BF16),
            jax.ShapeDtypeStruct((batch, RET_HEADS, RET_DK, RET_DV), F32),
        ],
        scratch_shapes=[pltpu.VMEM((RET_HEADS, RET_DK, RET_DV), F32)],
        compiler_params=_cparams(2, 40),
        name="retain_prompt",
    )(log_g, rq, rk, rv, gsw, gn)


def _sample_attend_row(z_ref, caches, r):
    half = BAND // 2

    def heads(tile0, g):
        base = (tile0 + g) * TN
        return jnp.concatenate(
            [z_ref[r, :, base + hh * HEAD_DIM:base + (hh + 1) * HEAD_DIM] for hh in range(HEADS)], axis=0)

    def paired(ref, lo):
        return jnp.concatenate([ref[r, 0:half, lo:lo + HEADS, :], ref[r, half:BAND, lo:lo + HEADS, :]], axis=1)

    twice = lambda a: jnp.concatenate([a, a], axis=0)
    fold = lambda a, op: op(a[0:HEADS], a[HEADS:2 * HEADS])

    us, ms, ls = [], [], []
    for g in range(N_GROUPS):
        q = heads(COL_Q, g) * (HEAD_DIM ** -0.5)
        k_new, v_new = heads(COL_K, g), heads(COL_V, g)
        k_buf, v_buf = paired(caches[g], 0), paired(caches[g], HEADS)
        s_buf = jnp.sum(k_buf * twice(q)[None], axis=-1, keepdims=True)
        s_new = jnp.sum(k_new * q, axis=-1, keepdims=True)
        m = jnp.maximum(fold(jnp.max(s_buf, axis=0), jnp.maximum), s_new)
        p_buf = jnp.exp(s_buf - twice(m)[None])
        p_new = jnp.exp(s_new - m)
        ls.append(fold(jnp.sum(p_buf, axis=0), jnp.add) + p_new)
        us.append(fold(jnp.sum(p_buf * v_buf, axis=0), jnp.add) + p_new * v_new)
        ms.append(m)
    m = jnp.maximum(jnp.maximum(ms[0], ms[1]), ms[2])
    num = jnp.zeros((HEADS, HEAD_DIM), F32)
    den = jnp.zeros((HEADS, 1), F32)
    for g in range(N_GROUPS):
        w = jnp.exp(ms[g] - m)
        num = num + w * us[g]
        den = den + w * ls[g]
    return num / den


def _sample_retain_head(lg_ref, z_ref, cos, sin, gn_ref, s_ref, ret_ref, s_out_ref, h):
    rows = z_ref.shape[0]
    eye = (lax.broadcasted_iota(jnp.int32, (RET_DK, RET_DK), 0)
           == lax.broadcasted_iota(jnp.int32, (RET_DK, RET_DK), 1))

    def column(v):
        return jnp.sum(jnp.where(eye, jnp.broadcast_to(v, (RET_DK, RET_DK)), 0.0), axis=1, keepdims=True)

    def zcols(tile0, width):
        return z_ref[:, 0, tile0 * TN + h * width:tile0 * TN + (h + 1) * width]

    gamma = jnp.exp(lg_ref[h] * jnp.ones((1, RET_DV), F32))
    q_h = _rope(zcols(COL_RQ, RET_DK), cos, sin)
    k_h = _rope(zcols(COL_RK, RET_DK), cos, sin) * (RET_DK ** -0.5)
    v_h = zcols(COL_RV, RET_DV)
    outs = []
    for r in range(rows):
        s_new = gamma * s_ref[r, h] + column(k_h[r:r + 1, :]) * v_h[r:r + 1, :]
        s_out_ref[r, h] = s_new
        outs.append(jnp.sum(column(q_h[r:r + 1, :]) * s_new, axis=0, keepdims=True))
    seg = slice(h * RET_DV, (h + 1) * RET_DV)
    ret_ref[:, 0, seg] = _group_norm_gate(jnp.concatenate(outs, axis=0), gn_ref[:, seg], _silu(zcols(COL_GSW, RET_DV)))


FF_CHUNK = 1024


def _tail_kernel(att_ref, ret_ref, ga_ref, gr_ref, x_ref, wa_ref, wr_ref, wo_ref, gpost_mix_ref, gpre_mlp_ref,
                 wu_ref, wd_ref, gpost_mlp_ref, y_ref, att_scr):
    r = att_ref.shape[1]
    for hh in range(HEADS):
        for c in range(r):
            rs = pl.ds(c, att_ref.shape[2], stride=r) if r > 1 else slice(None)
            att_scr[hh, rs, :] = att_ref[hh, c]
    att = jnp.concatenate([att_scr[hh].astype(BF16) for hh in range(HEADS)], axis=1)
    ret = jnp.concatenate([ret_ref[hh].astype(BF16) for hh in range(RET_HEADS)], axis=1)
    m = _sigmoid(ga_ref[...]) * _dot(att, wa_ref[...]) + _sigmoid(gr_ref[...]) * _dot(ret, wr_ref[...])
    x1 = x_ref[...] + _rms(_dot(m.astype(BF16), wo_ref[...])) * gpost_mix_ref[...]
    h = (_rms(x1) * gpre_mlp_ref[...]).astype(BF16)

    n_chunks = D_FF // FF_CHUNK
    acc = jnp.zeros(x_ref.shape, F32)
    for c in range(n_chunks):
        cs = slice(c * FF_CHUNK, (c + 1) * FF_CHUNK)
        u = jnp.maximum(_dot(h, wu_ref[:, cs]), 0.0)
        acc = acc + _dot((u * u).astype(BF16), wd_ref[cs, :])
    y_ref[...] = x1 + _rms(acc) * gpost_mlp_ref[...]


def _tail(att, ret, ga, gr, x2d, lw, tm):
    batch, _, r, ni, _ = att.shape
    seq = r * ni
    m = batch * seq
    assert seq % tm == 0 and tm % r == 0 and x2d.shape[0] == m
    tps = seq // tm
    steps = batch * tps
    rows = pl.BlockSpec((tm, D_MODEL), lambda i: (i, 0))
    full = lambda a: _resident(a.shape, lambda i: (0,) * a.ndim)
    weights = [lw[k] for k in ("w_att_br", "w_ret_br", "w_out", "g_post_mix", "g_pre_mlp",
                               "w_up", "w_down", "g_post_mlp")]
    args = [att, ret, ga, gr, x2d] + weights
    in_specs = [pl.BlockSpec((None, HEADS, r, tm // r, HEAD_DIM), lambda i: (i // tps, 0, 0, i % tps, 0)),
                pl.BlockSpec((None, RET_HEADS, tm, RET_DV), lambda i: (i // tps, 0, i % tps, 0)),
                rows, rows, rows] + [full(w) for w in weights]
    return pl.pallas_call(
        _tail_kernel,
        grid=(steps,),
        in_specs=in_specs,
        out_specs=rows,
        out_shape=jax.ShapeDtypeStruct((m, D_MODEL), F32),
        scratch_shapes=[pltpu.VMEM((HEADS, tm, HEAD_DIM), F32)],
        compiler_params=_cparams(1, 56),
        name="tail",
    )(*args)


def _rope_tables(pos):
    half = RET_DK // 2
    inv = ROPE_BASE ** (-jnp.arange(half, dtype=F32) / half)
    ang = pos.astype(F32)[:, None] * inv[None, :]
    cos, sin = jnp.cos(ang), jnp.sin(ang)
    return jnp.concatenate([cos, cos], axis=-1), jnp.concatenate([-sin, sin], axis=-1)


IN_PROJ_TM = 512
TAIL_TM = 512
RET_ROWS = 1024


def _layer(xp, xs, caches, state, layer, lw, log_g):
    batch, seq, _ = xp.shape
    db, t, _ = xs.shape
    assert t == 1
    xp2d, xs2d = xp.reshape(batch * seq, D_MODEL), xs.reshape(db, D_MODEL)

    z, w_att, w_rest = _in_proj_sample(xs2d, lw["g_pre_mix"], lw["w_in"])

    cos_p, sin_p = _rope_tables(jnp.arange(seq, dtype=jnp.int32))
    cos_s, sin_s = _rope_tables(PAST_LEN + jnp.arange(1, dtype=jnp.int32))
    outs = _in_proj_att(xp2d, lw["g_pre_mix"], w_att, lw["w_up"], lw["w_down"], batch, seq, IN_PROJ_TM)
    qkv = outs[:9]
    rows_p = [kv.reshape(batch, -1, 2, HEADS, HEAD_DIM) for kv in outs[9:12]]
    lw = dict(lw, w_up=outs[12], w_down=outs[13])
    rq, rk, rv, gsw, ga, gr, ret_s, state_s, att_s = _in_proj_rest(
        xp2d, lw["g_pre_mix"], w_rest, cos_p, sin_p,
        (log_g, z, cos_s, sin_s, lw["g_ret_norm"], state, caches, layer), batch, seq, IN_PROJ_TM)
    att = _attn_prompt(qkv, batch, seq)
    ret, state_p = _ret_prompt(log_g, rq, rk, rv, gsw, lw["g_ret_norm"], batch, seq, RET_ROWS)
    yp = _tail(att, ret, ga, gr, xp2d, lw, TAIL_TM)

    att_s = jnp.transpose(att_s, (1, 0, 2)).reshape(1, HEADS, 1, db, HEAD_DIM)
    ret_s = jnp.transpose(ret_s.reshape(db, RET_HEADS, RET_DV), (1, 0, 2))[None]
    ga_s = z[:, COL_GA * TN:COL_GA * TN + D_MODEL]
    gr_s = z[:, COL_GR * TN:COL_GR * TN + D_MODEL]
    ys = _tail(att_s, ret_s, ga_s, gr_s, xs2d, lw, db)
    rows_s = []
    for g in range(N_GROUPS):
        k_new = z[:, (COL_K + g) * TN:(COL_K + g + 1) * TN]
        v_new = z[:, (COL_V + g) * TN:(COL_V + g + 1) * TN]
        rows_s.append(jnp.stack([k_new, v_new], axis=1).reshape(db, 1, 2, HEADS, HEAD_DIM))
    return (yp.reshape(batch, seq, D_MODEL), rows_p, state_p), (ys.reshape(db, 1, D_MODEL), rows_s, state_s)


def _stack(xs):
    return xs[0][None] if len(xs) == 1 else jnp.stack(xs, axis=0)


def kernel(x_prompt, x_sample, cache_kv_d1, cache_kv_d4, cache_kv_d16, state_ret, w_in, w_att_br, w_ret_br, w_out, w_up, w_down, g_ret_norm, g_pre_mix, g_post_mix, g_pre_mlp, g_post_mlp):
    depth = w_in.shape[0]
    log_g = jnp.log1p(-jnp.power(2.0, -5.0 - jnp.arange(RET_HEADS, dtype=F32)))
    caches = (cache_kv_d1, cache_kv_d4, cache_kv_d16)
    xp, xs = x_prompt, x_sample
    p_rows, s_rows = [[], [], []], [[], [], []]
    p_states, s_states = [], []
    for l in range(depth):
        lw = {
            "w_in": w_in[l], "w_up": w_up[l], "w_down": w_down[l],
            "w_att_br": w_att_br[l].astype(BF16), "w_ret_br": w_ret_br[l].astype(BF16),
            "w_out": w_out[l].astype(BF16),
            "g_ret_norm": g_ret_norm[l].reshape(1, RET_V), "g_pre_mix": g_pre_mix[l].reshape(1, D_MODEL),
            "g_post_mix": g_post_mix[l].reshape(1, D_MODEL), "g_pre_mlp": g_pre_mlp[l].reshape(1, D_MODEL),
            "g_post_mlp": g_post_mlp[l].reshape(1, D_MODEL),
        }
        (xp, rows_p, sp), (xs, rows_s, ss) = _layer(xp, xs, caches, state_ret, l, lw, log_g)
        for g in range(N_GROUPS):
            p_rows[g].append(rows_p[g])
            s_rows[g].append(rows_s[g])
        p_states.append(sp)
        s_states.append(ss)
    return (xp, xs, _stack(p_rows[0]), _stack(p_rows[1]), _stack(p_rows[2]), _stack(p_states),
            _stack(s_rows[0]), _stack(s_rows[1]), _stack(s_rows[2]), _stack(s_states))
```

```python
import functools

import jax
import jax.numpy as jnp
from jax import lax
from jax.experimental import pallas as pl
from jax.experimental.pallas import tpu as pltpu

F32 = jnp.float32
BF16 = jnp.bfloat16

D_MODEL = 1024
PAST_LEN = 8192

DIL_GROUPS = ((128, 1), (512, 4), (2048, 16))
N_GROUPS = 3
HEADS = 4
HEAD_DIM = 128
ATT_OUT = HEADS * HEAD_DIM
ATT_COLS = N_GROUPS * ATT_OUT
BAND = 128

RET_HEADS = 4
RET_DK = 128
RET_DV = 256
RET_QK = RET_HEADS * RET_DK
RET_V = RET_HEADS * RET_DV
RET_CHUNK = 128
ROPE_BASE = 10000.0

D_FF = 4 * D_MODEL
IN_COLS = 3 * ATT_COLS + 2 * RET_QK + 2 * RET_V + 2 * D_MODEL
EPS = 1e-6
NEG_INF = -1e30

TN = 512
N_COL_TILES = IN_COLS // TN
COL_Q, COL_K, COL_V, COL_RQ, COL_RK, COL_RV, COL_GSW, COL_GA, COL_GR = 0, 3, 6, 9, 10, 11, 13, 15, 17

V7X_VMEM_BYTES = 64 * 1024 * 1024
MIB = 1024 * 1024


def _cparams(n_axes, vmem_mib):
    assert vmem_mib * MIB < V7X_VMEM_BYTES
    return pltpu.CompilerParams(
        dimension_semantics=("arbitrary",) * n_axes,
        vmem_limit_bytes=vmem_mib * MIB,
    )


def _resident(shape, index_map):
    return pl.BlockSpec(shape, index_map, pipeline_mode=pl.Buffered(1))


def _rms(x):
    return x * lax.rsqrt(jnp.mean(x * x, axis=-1, keepdims=True) + EPS)


def _sigmoid(x):
    return 1.0 / (1.0 + jnp.exp(-x))


def _silu(x):
    return x * _sigmoid(x)


def _dot(a, b):
    return jnp.dot(a, b, preferred_element_type=F32)


def _dot_nt(a, b):
    return lax.dot_general(a, b, (((1,), (1,)), ((), ())), preferred_element_type=F32)


def _rope(x, cos, sin):
    return x * cos + pltpu.roll(x, RET_DK // 2, 1) * sin


N_ATT_TILES = 3 * N_GROUPS


def _norm_rows(x_ref, g_ref):
    return (_rms(x_ref[...]) * g_ref[...]).astype(BF16)


def _head(acc, hh, width=HEAD_DIM):
    return acc[:, hh * width:(hh + 1) * width]


def _in_proj_att_kernel(x_ref, g_ref, w_ref, wu_ref, wd_ref,
                        q0, q1, q2, k0, k1, k2, v0, v1, v2, kvo0, kvo1, kvo2, wu_bf_ref, wd_bf_ref,
                        scr_q, sk0, sk1, sk2, sv0, sv1, sv2, *, first_kept):
    tm = x_ref.shape[0]
    h = _norm_rows(x_ref, g_ref)
    wu_bf_ref[...] = wu_ref[...].astype(BF16)
    wd_bf_ref[...] = wd_ref[...].astype(BF16)

    def park(j, scr, scale=None):
        acc = _dot(h, w_ref[:, j * TN:(j + 1) * TN])
        for hh in range(HEADS):
            scr[hh] = _head(acc, hh) if scale is None else _head(acc, hh) * scale

    def residues(scr, dst, dil):
        for hh in range(HEADS):
            for c in range(dil):
                rows = pl.ds(c, tm // dil, stride=dil) if dil > 1 else slice(None)
                dst[hh, c] = scr[hh, rows, :].astype(BF16)

    for g, (_, dil) in enumerate(DIL_GROUPS):
        park(COL_Q + g, scr_q, HEAD_DIM ** -0.5)
        residues(scr_q, (q0, q1, q2)[g], dil)
        park(COL_K + g, (sk0, sk1, sk2)[g])
        residues((sk0, sk1, sk2)[g], (k0, k1, k2)[g], dil)
        park(COL_V + g, (sv0, sv1, sv2)[g])
        residues((sv0, sv1, sv2)[g], (v0, v1, v2)[g], dil)

    for g, kvo in enumerate((kvo0, kvo1, kvo2)):
        def write(g=g, kvo=kvo):
            keep = kvo.shape[0] // (2 * HEADS)
            for which, scr in enumerate(((sk0, sk1, sk2)[g], (sv0, sv1, sv2)[g])):
                for hh in range(HEADS):
                    kvo[pl.ds(which * HEADS + hh, keep, stride=2 * HEADS), :] = scr[hh, tm - keep:, :]
        if first_kept[g] == 0:
            write()
        else:
            pl.when(pl.program_id(1) >= first_kept[g])(write)


def _in_proj_att(x2d, g, w_att, w_up, w_down, batch, seq, tm):
    assert seq % tm == 0 and tm % (16 * DIL_GROUPS[-1][1]) == 0
    tps = seq // tm
    out_shape, out_specs, first_kept = [], [], []
    for _ in range(3):
        for _, dil in DIL_GROUPS:
            out_shape.append(jax.ShapeDtypeStruct((batch, HEADS, dil, seq // dil, HEAD_DIM), BF16))
            out_specs.append(pl.BlockSpec((None, HEADS, dil, tm // dil, HEAD_DIM), lambda b, t: (b, 0, 0, t, 0)))
    for win, _ in DIL_GROUPS:
        keep = min(win, seq)
        rows = min(keep, tm)
        assert keep % rows == 0
        first = (seq - keep) // rows
        first_kept.append((seq - keep) // tm)
        out_shape.append(jax.ShapeDtypeStruct((batch, keep * 2 * HEADS, HEAD_DIM), F32))
        out_specs.append(pl.BlockSpec(
            (None, rows * 2 * HEADS, HEAD_DIM),
            functools.partial(lambda b, t, first, per: (b, jnp.maximum((t + 1) * per - 1 - first, 0), 0),
                              first=first, per=tm // rows)))
    steps = batch * tps
    cast_rows = [w.shape[0] // steps for w in (w_up, w_down)]
    assert all(w.shape[0] % steps == 0 and r % 16 == 0 for w, r in zip((w_up, w_down), cast_rows))
    cast_specs = [pl.BlockSpec((r, w.shape[1]), lambda b, t: (b * tps + t, 0))
                  for w, r in zip((w_up, w_down), cast_rows)]
    return pl.pallas_call(
        functools.partial(_in_proj_att_kernel, first_kept=tuple(first_kept)),
        grid=(batch, tps),
        in_specs=[
            pl.BlockSpec((tm, D_MODEL), lambda b, t: (b * tps + t, 0)),
            _resident((1, D_MODEL), lambda b, t: (0, 0)),
            _resident(w_att.shape, lambda b, t: (0, 0)),
        ] + cast_specs,
        out_specs=out_specs + cast_specs,
        out_shape=out_shape + [jax.ShapeDtypeStruct(w.shape, BF16) for w in (w_up, w_down)],
        scratch_shapes=[pltpu.VMEM((HEADS, tm, HEAD_DIM), F32)] * (1 + 2 * N_GROUPS),
        compiler_params=_cparams(2, 48),
        name="in_proj_att",
    )(x2d, g, w_att, w_up, w_down)


def _in_proj_rest_kernel(x_ref, g_ref, w_ref, cos_ref, sin_ref, z_ref, c0, c1, c2,
                         rq_ref, rk_ref, rv_ref, gsw_ref, ga_ref, gr_ref, att_s_ref):
    h = _norm_rows(x_ref, g_ref)
    rows_left = list(range(z_ref.shape[0]))

    def side_row():
        if rows_left:
            r = rows_left.pop(0)
            att_s_ref[r] = _sample_attend_row(z_ref, (c0, c1, c2), r)

    def tile(j):
        lo = (j - N_ATT_TILES) * TN
        return _dot(h, w_ref[:, lo:lo + TN])

    def ret_qk(col, dst, scale):
        acc = tile(col)
        cos, sin = cos_ref[...], sin_ref[...]
        for hh in range(RET_HEADS):
            r = _rope(_head(acc, hh), cos, sin)
            dst[hh] = r if scale is None else r * scale

    def ret_wide(col, dst, e, act=None):
        acc = tile(col + e)
        for s in range(2):
            part = _head(acc, s, RET_DV)
            dst[2 * e + s] = (part if act is None else act(part)).astype(dst.dtype)

    def gate(col, dst, e):
        dst[:, e * TN:(e + 1) * TN] = tile(col + e)

    ret_qk(COL_RQ, rq_ref, None)
    side_row()
    ret_qk(COL_RK, rk_ref, RET_DK ** -0.5)
    side_row()
    for e in range(2):
        ret_wide(COL_RV, rv_ref, e)
        ret_wide(COL_GSW, gsw_ref, e, _silu)
        side_row()
        gate(COL_GA, ga_ref, e)
        gate(COL_GR, gr_ref, e)
    while rows_left:
        side_row()


def _in_proj_rest(x2d, g, w_rest, cos_t, sin_t, z, caches, layer, batch, seq, tm):
    assert seq % tm == 0
    tps = seq // tm
    steps = batch * tps
    db = z.shape[0]
    assert db % steps == 0
    side_rows = db // steps
    views, cache_specs = [], []
    for (win, dil), c in zip(DIL_GROUPS, caches):
        assert c.shape[1] == db and c.shape[2] == win and win // dil == BAND
        views.append(c.reshape(c.shape[0] * db, BAND, dil, 2 * HEADS, HEAD_DIM))
        cache_specs.append(pl.BlockSpec((side_rows, BAND, None, 2 * HEADS, HEAD_DIM),
                                        lambda b, t: (layer * steps + b * tps + t, 0, 0, 0, 0)))
    hm = lambda width, dt: (jax.ShapeDtypeStruct((batch, RET_HEADS, seq, width), dt),
                            pl.BlockSpec((None, RET_HEADS, tm, width), lambda b, t: (b, 0, t, 0)))
    nat = (jax.ShapeDtypeStruct((batch * seq, D_MODEL), F32),
           pl.BlockSpec((tm, D_MODEL), lambda b, t: (b * tps + t, 0)))
    outs = (hm(RET_DK, F32), hm(RET_DK, F32), hm(RET_DV, BF16), hm(RET_DV, F32), nat, nat)
    side_outs = ((jax.ShapeDtypeStruct((db, HEADS, HEAD_DIM), F32),
                  pl.BlockSpec((side_rows, HEADS, HEAD_DIM), lambda b, t: (b * tps + t, 0, 0))),)
    return pl.pallas_call(
        _in_proj_rest_kernel,
        grid=(batch, tps),
        in_specs=[
            pl.BlockSpec((tm, D_MODEL), lambda b, t: (b * tps + t, 0)),
            _resident((1, D_MODEL), lambda b, t: (0, 0)),
            _resident(w_rest.shape, lambda b, t: (0, 0)),
            pl.BlockSpec((tm, RET_DK), lambda b, t: (t, 0)),
            pl.BlockSpec((tm, RET_DK), lambda b, t: (t, 0)),
            pl.BlockSpec((side_rows, 1, IN_COLS), lambda b, t: (b * tps + t, 0, 0)),
        ] + cache_specs,
        out_specs=[spec for _, spec in outs + side_outs],
        out_shape=[shp for shp, _ in outs + side_outs],
        compiler_params=_cparams(2, 56),
        name="in_proj_rest",
    )(x2d, g, w_rest, cos_t, sin_t, z.reshape(db, 1, IN_COLS), *views)


def _in_proj_sample_kernel(x_ref, g_ref, w_ref, z_ref, w_att_ref, w_rest_ref, h_ref):
    j = pl.program_id(0)

    @pl.when(j == 0)
    def _():
        h_ref[...] = _norm_rows(x_ref, g_ref)

    w_bf = w_ref[...].astype(BF16)

    @pl.when(j < N_ATT_TILES)
    def _():
        w_att_ref[...] = w_bf

    @pl.when(j >= N_ATT_TILES)
    def _():
        w_rest_ref[...] = w_bf

    z_ref[...] = _dot(h_ref[...], w_bf)


def _in_proj_sample(x2d, g, w):
    m = x2d.shape[0]
    col_tile = lambda rows, f: pl.BlockSpec((rows, TN), lambda j: (0, f(j)))
    n_rest = N_COL_TILES - N_ATT_TILES
    return pl.pallas_call(
        _in_proj_sample_kernel,
        grid=(N_COL_TILES,),
        in_specs=[
            _resident((m, D_MODEL), lambda j: (0, 0)),
            _resident((1, D_MODEL), lambda j: (0, 0)),
            col_tile(D_MODEL, lambda j: j),
        ],
        out_specs=[col_tile(m, lambda j: j),
                   col_tile(D_MODEL, lambda j: jnp.minimum(j, N_ATT_TILES - 1)),
                   col_tile(D_MODEL, lambda j: jnp.maximum(j - N_ATT_TILES, 0))],
        out_shape=[jax.ShapeDtypeStruct((m, IN_COLS), F32),
                   jax.ShapeDtypeStruct((D_MODEL, N_ATT_TILES * TN), BF16),
                   jax.ShapeDtypeStruct((D_MODEL, n_rest * TN), BF16)],
        scratch_shapes=[pltpu.VMEM((m, D_MODEL), BF16)],
        compiler_params=_cparams(1, 24),
        name="in_proj_sample",
    )(x2d, g, w)


def _softmax_block(s, v):
    m = jnp.max(s, axis=1, keepdims=True)
    p = jnp.exp(s - m).astype(BF16)
    uv = _dot(p, jnp.concatenate([v, jnp.ones_like(v)], axis=1))
    return uv[:, :HEAD_DIM], jnp.broadcast_to(m, (s.shape[0], HEAD_DIM)), uv[:, HEAD_DIM:]


ATT_BLOCKS_PER_STEP = 16
ATT_ORDER = 4


def _attn_prompt_kernel(q0, q1, q2, k0, k1, k2, v0, v1, v2, o_ref,
                        u0, u1, u2, m0, m1, m2, l0, l1, l2, *, seq):
    qs, ks, vs = (q0, q1, q2), (k0, k1, k2), (v0, v1, v2)
    us, ms, ls = (u0, u1, u2), (m0, m1, m2), (l0, l1, l2)
    order = ATT_ORDER
    n_out = seq // order
    row1 = lax.broadcasted_iota(jnp.int32, (BAND, BAND), 0)
    col1 = lax.broadcasted_iota(jnp.int32, (BAND, BAND), 1)
    causal = col1 <= row1
    row2 = lax.broadcasted_iota(jnp.int32, (BAND, 2 * BAND), 0)
    col2 = lax.broadcasted_iota(jnp.int32, (BAND, 2 * BAND), 1)
    band = (col2 >= row2) & (col2 <= row2 + BAND)

    aligned = lambda r: r if isinstance(r, int) else pl.multiple_of(r, BAND)

    for g, (_, dil) in enumerate(DIL_GROUPS):
        n = seq // dil
        nb = n // BAND

        def run_blocks(blocks, g=g, n=n, dil=dil):
            scores, values = [], []
            for c, blk, has_prev in blocks:
                own = pl.ds(aligned(blk * BAND), BAND)
                keys = pl.ds(aligned((blk - 1) * BAND), 2 * BAND) if has_prev else own
                mask = band if has_prev else causal
                scores.append(jnp.where(mask, _dot_nt(qs[g][c, own, :], ks[g][c, keys, :]), NEG_INF))
                values.append(vs[g][c, keys, :])
            results = [_softmax_block(s, v) for s, v in zip(scores, values)]
            for (c, blk, _), (u, m, l) in zip(blocks, results):
                if dil <= order:
                    rs = pl.ds(aligned(c * n + blk * BAND), BAND)
                else:
                    step = dil // order
                    rs = pl.ds((c % order) * n_out + c // order + blk * (BAND * step), BAND, stride=step)
                us[g][rs, :] = u
                ms[g][rs, :] = m
                ls[g][rs, :] = l

        per = ATT_BLOCKS_PER_STEP
        assert (dil * nb) % per == 0 and (per % nb == 0 or nb % per == 0)
        if nb >= per:
            def body(i, carry, run_blocks=run_blocks, steps=nb // per):
                c, i0 = i // steps, (i % steps) * per
                run_blocks([(c, i0 + e, True) for e in range(per)])
                return carry
            for c in range(dil):
                run_blocks([(c, e, e > 0) for e in range(per)])
                lax.fori_loop(c * (nb // per) + 1, (c + 1) * (nb // per), body, 0)
        else:
            def body(i, carry, run_blocks=run_blocks, nb=nb, cps=per // nb):
                run_blocks([(i * cps + e // nb, e % nb, e % nb > 0) for e in range(per)])
                return carry
            lax.fori_loop(0, dil * nb // per, body, 0)

    chunks = n_out // BAND

    def combine_one(j):
        c, t = j // chunks, j % chunks
        rows = []
        for _, dil in DIL_GROUPS:
            r = min(dil, order)
            step = order // r
            start = (c % r) * (seq // r) + c // r + t * (BAND * step)
            rows.append(pl.ds(aligned(start), BAND) if step == 1 else pl.ds(start, BAND, stride=step))
        m_g = [ms[g][rows[g], :] for g in range(N_GROUPS)]
        m = jnp.maximum(jnp.maximum(m_g[0], m_g[1]), m_g[2])
        num = jnp.zeros((BAND, HEAD_DIM), F32)
        den = jnp.zeros((BAND, HEAD_DIM), F32)
        for g in range(N_GROUPS):
            w = jnp.exp(m_g[g] - m)
            num = num + w * us[g][rows[g], :]
            den = den + w * ls[g][rows[g], :]
        return num / den

    per = 4
    assert (order * chunks) % per == 0

    def combine(i, carry):
        outs = [combine_one(i * per + e) for e in range(per)]
        for e, o in enumerate(outs):
            o_ref[pl.ds(aligned((i * per + e) * BAND), BAND), :] = o
        return carry

    lax.fori_loop(0, order * chunks // per, combine, 0)


def _attn_prompt(qkv, batch, seq):
    assert seq % (BAND * DIL_GROUPS[-1][1]) == 0
    assert all(ATT_ORDER % d == 0 or d % ATT_ORDER == 0 for _, d in DIL_GROUPS)
    in_specs = [pl.BlockSpec((None, None) + a.shape[2:], lambda b, h: (b, h, 0, 0, 0)) for a in qkv]
    out = pl.pallas_call(
        functools.partial(_attn_prompt_kernel, seq=seq),
        grid=(batch, HEADS),
        in_specs=in_specs,
        out_specs=pl.BlockSpec((None, None, seq, HEAD_DIM), lambda b, h: (b, h, 0, 0)),
        out_shape=jax.ShapeDtypeStruct((batch, HEADS, seq, HEAD_DIM), F32),
        scratch_shapes=[pltpu.VMEM((seq, HEAD_DIM), F32)] * 9,
        compiler_params=_cparams(2, 32),
        name="attend_prompt",
    )(*qkv)
    return out.reshape(batch, HEADS, ATT_ORDER, seq // ATT_ORDER, HEAD_DIM)


def _group_norm_gate(o, gn, gate):
    mu = jnp.mean(o, axis=-1, keepdims=True)
    d = o - mu
    var = jnp.mean(d * d, axis=-1, keepdims=True)
    return gate * (d * lax.rsqrt(var + EPS) * gn)


RET_CHUNKS_PER_STEP = 8


def _ret_prompt_kernel(lg_ref, q_ref, k_ref, v_ref, gate_ref, gn_ref, o_ref, s_out_ref, s_scr):
    c = RET_CHUNK
    rows = q_ref.shape[1]
    t = lax.broadcasted_iota(jnp.int32, (c, 1), 0).astype(F32)
    rel = (lax.broadcasted_iota(jnp.int32, (c, c), 0) - lax.broadcasted_iota(jnp.int32, (c, c), 1)).astype(F32)

    @pl.when(pl.program_id(1) == 0)
    def _():
        s_scr[...] = jnp.zeros_like(s_scr)

    consts = []
    for h in range(RET_HEADS):
        lg = lg_ref[h]
        consts.append((
            jnp.where(rel >= 0, jnp.exp(lg * jnp.maximum(rel, 0.0)), 0.0),
            jnp.exp(lg * (t + 1.0)),
            jnp.exp(lg * (c - 1.0 - t)),
            jnp.exp(lg * jnp.full((1, RET_DV), float(c), F32)),
        ))

    per = RET_CHUNKS_PER_STEP
    assert (rows // c) % per == 0

    def step(i, carry):
        rss = [pl.ds(pl.multiple_of((i * per + e) * c, c), c) for e in range(per)]
        local = {}
        for h in range(RET_HEADS):
            decay, inner, tail, _ = consts[h]
            for e, rs in enumerate(rss):
                q, k, vb = q_ref[h, rs, :], k_ref[h, rs, :], v_ref[h, rs, :]
                a = _dot_nt(q.astype(BF16), k.astype(BF16)) * decay
                local[h, e] = (_dot(a.astype(BF16), vb),
                               (q * inner).astype(BF16),
                               _dot((k * tail).T.astype(BF16), vb))
        outs = {}
        for h in range(RET_HEADS):
            s = s_scr[h]
            for e in range(per):
                intra, q_in, kv = local[h, e]
                outs[h, e] = intra + _dot(q_in, s.astype(BF16))
                s = consts[h][3] * s + kv
            s_scr[h] = s
        for h in range(RET_HEADS):
            gn = gn_ref[:, h * RET_DV:(h + 1) * RET_DV]
            for e, rs in enumerate(rss):
                o_ref[h, rs, :] = _group_norm_gate(outs[h, e], gn, gate_ref[h, rs, :]).astype(o_ref.dtype)
        return carry

    lax.fori_loop(0, rows // (c * per), step, 0)

    @pl.when(pl.program_id(1) == pl.num_programs(1) - 1)
    def _():
        s_out_ref[...] = s_scr[...]


def _ret_prompt(log_g, rq, rk, rv, gsw, gn, batch, seq, rows):
    assert seq % rows == 0 and rows % RET_CHUNK == 0
    hm = lambda width: pl.BlockSpec((None, RET_HEADS, rows, width), lambda b, t: (b, 0, t, 0))
    return pl.pallas_call(
        _ret_prompt_kernel,
        grid=(batch, seq // rows),
        in_specs=[
            pl.BlockSpec(memory_space=pltpu.SMEM),
            hm(RET_DK), hm(RET_DK), hm(RET_DV), hm(RET_DV),
            _resident((1, RET_V), lambda b, t: (0, 0)),
        ],
        out_specs=[
            hm(RET_DV),
            pl.BlockSpec((None, RET_HEADS, RET_DK, RET_DV), lambda b, t: (b, 0, 0, 0)),
        ],
        out_shape=[
            jax.ShapeDtypeStruct((batch, RET_HEADS, seq, RET_DV), BF16),
            jax.ShapeDtypeStruct((batch, RET_HEADS, RET_DK, RET_DV), F32),
        ],
        scratch_shapes=[pltpu.VMEM((RET_HEADS, RET_DK, RET_DV), F32)],
        compiler_params=_cparams(2, 40),
        name="retain_prompt",
    )(log_g, rq, rk, rv, gsw, gn)


def _sample_attend_row(z_ref, caches, r):
    half = BAND // 2

    def heads(tile0, g):
        base = (tile0 + g) * TN
        return jnp.concatenate(
            [z_ref[r, :, base + hh * HEAD_DIM:base + (hh + 1) * HEAD_DIM] for hh in range(HEADS)], axis=0)

    def paired(ref, lo):
        return jnp.concatenate([ref[r, 0:half, lo:lo + HEADS, :], ref[r, half:BAND, lo:lo + HEADS, :]], axis=1)

    twice = lambda a: jnp.concatenate([a, a], axis=0)
    fold = lambda a, op: op(a[0:HEADS], a[HEADS:2 * HEADS])

    us, ms, ls = [], [], []
    for g in range(N_GROUPS):
        q = heads(COL_Q, g) * (HEAD_DIM ** -0.5)
        k_new, v_new = heads(COL_K, g), heads(COL_V, g)
        k_buf, v_buf = paired(caches[g], 0), paired(caches[g], HEADS)
        s_buf = jnp.sum(k_buf * twice(q)[None], axis=-1, keepdims=True)
        s_new = jnp.sum(k_new * q, axis=-1, keepdims=True)
        m = jnp.maximum(fold(jnp.max(s_buf, axis=0), jnp.maximum), s_new)
        p_buf = jnp.exp(s_buf - twice(m)[None])
        p_new = jnp.exp(s_new - m)
        ls.append(fold(jnp.sum(p_buf, axis=0), jnp.add) + p_new)
        us.append(fold(jnp.sum(p_buf * v_buf, axis=0), jnp.add) + p_new * v_new)
        ms.append(m)
    m = jnp.maximum(jnp.maximum(ms[0], ms[1]), ms[2])
    num = jnp.zeros((HEADS, HEAD_DIM), F32)
    den = jnp.zeros((HEADS, 1), F32)
    for g in range(N_GROUPS):
        w = jnp.exp(ms[g] - m)
        num = num + w * us[g]
        den = den + w * ls[g]
    return num / den


def _sample_retain_head(lg_ref, z_ref, cos, sin, gn_ref, s_ref, ret_ref, s_out_ref, h):
    rows = z_ref.shape[0]
    eye = (lax.broadcasted_iota(jnp.int32, (RET_DK, RET_DK), 0)
           == lax.broadcasted_iota(jnp.int32, (RET_DK, RET_DK), 1))

    def column(v):
        return jnp.sum(jnp.where(eye, jnp.broadcast_to(v, (RET_DK, RET_DK)), 0.0), axis=1, keepdims=True)

    def zcols(tile0, width):
        return z_ref[:, 0, tile0 * TN + h * width:tile0 * TN + (h + 1) * width]

    gamma = jnp.exp(lg_ref[h] * jnp.ones((1, RET_DV), F32))
    q_h = _rope(zcols(COL_RQ, RET_DK), cos, sin)
    k_h = _rope(zcols(COL_RK, RET_DK), cos, sin) * (RET_DK ** -0.5)
    v_h = zcols(COL_RV, RET_DV)
    outs = []
    for r in range(rows):
        s_new = gamma * s_ref[r, h] + column(k_h[r:r + 1, :]) * v_h[r:r + 1, :]
        s_out_ref[r, h] = s_new
        outs.append(jnp.sum(column(q_h[r:r + 1, :]) * s_new, axis=0, keepdims=True))
    seg = slice(h * RET_DV, (h + 1) * RET_DV)
    ret_ref[:, 0, seg] = _group_norm_gate(jnp.concatenate(outs, axis=0), gn_ref[:, seg], _silu(zcols(COL_GSW, RET_DV)))


FF_CHUNK = 1024


def _tail_kernel(*refs, with_side):
    (att_ref, ret_ref, ga_ref, gr_ref, x_ref, wa_ref, wr_ref, wo_ref, gpost_mix_ref, gpre_mlp_ref,
     wu_ref, wd_ref, gpost_mlp_ref) = refs[:13]
    att_scr = refs[-1]
    side = []
    if with_side:
        lg_ref, z_ref, cos_ref, sin_ref, gn_ref, s_ref, y_ref, ret_s_ref, s_out_ref = refs[13:-1]
        side = [functools.partial(_sample_retain_head, lg_ref, z_ref, cos_ref[...], sin_ref[...], gn_ref, s_ref,
                                  ret_s_ref, s_out_ref, hh) for hh in range(RET_HEADS)]
    else:
        y_ref, = refs[13:-1]
    r = att_ref.shape[1]
    for hh in range(HEADS):
        for c in range(r):
            rs = pl.ds(c, att_ref.shape[2], stride=r) if r > 1 else slice(None)
            att_scr[hh, rs, :] = att_ref[hh, c]
    att = jnp.concatenate([att_scr[hh].astype(BF16) for hh in range(HEADS)], axis=1)
    ret = jnp.concatenate([ret_ref[hh].astype(BF16) for hh in range(RET_HEADS)], axis=1)
    m = _sigmoid(ga_ref[...]) * _dot(att, wa_ref[...]) + _sigmoid(gr_ref[...]) * _dot(ret, wr_ref[...])
    x1 = x_ref[...] + _rms(_dot(m.astype(BF16), wo_ref[...])) * gpost_mix_ref[...]
    h = (_rms(x1) * gpre_mlp_ref[...]).astype(BF16)

    n_chunks = D_FF // FF_CHUNK
    acc = jnp.zeros(x_ref.shape, F32)
    for c in range(n_chunks):
        cs = slice(c * FF_CHUNK, (c + 1) * FF_CHUNK)
        u = jnp.maximum(_dot(h, wu_ref[:, cs]), 0.0)
        acc = acc + _dot((u * u).astype(BF16), wd_ref[cs, :])
        for work in side[c::n_chunks]:
            work()
    y_ref[...] = x1 + _rms(acc) * gpost_mlp_ref[...]


def _tail(att, ret, ga, gr, x2d, lw, tm, side=None):
    batch, _, r, ni, _ = att.shape
    seq = r * ni
    m = batch * seq
    assert seq % tm == 0 and tm % r == 0 and x2d.shape[0] == m
    tps = seq // tm
    steps = batch * tps
    rows = pl.BlockSpec((tm, D_MODEL), lambda i: (i, 0))
    full = lambda a: _resident(a.shape, lambda i: (0,) * a.ndim)
    weights = [lw[k] for k in ("w_att_br", "w_ret_br", "w_out", "g_post_mix", "g_pre_mlp",
                               "w_up", "w_down", "g_post_mlp")]
    args = [att, ret, ga, gr, x2d] + weights
    in_specs = [pl.BlockSpec((None, HEADS, r, tm // r, HEAD_DIM), lambda i: (i // tps, 0, 0, i % tps, 0)),
                pl.BlockSpec((None, RET_HEADS, tm, RET_DV), lambda i: (i // tps, 0, i % tps, 0)),
                rows, rows, rows] + [full(w) for w in weights]
    out_shape = [jax.ShapeDtypeStruct((m, D_MODEL), F32)]
    out_specs = [rows]
    if side is not None:
        log_g, z, cos_s, sin_s, gn, state, layer = side
        db = z.shape[0]
        assert db % steps == 0
        side_rows = db // steps
        st_block = (side_rows, RET_HEADS, RET_DK, RET_DV)
        args += [log_g, z.reshape(db, 1, IN_COLS), cos_s, sin_s, gn,
                 state.reshape((state.shape[0] * db,) + state.shape[2:])]
        in_specs += [pl.BlockSpec(memory_space=pltpu.SMEM),
                     pl.BlockSpec((side_rows, 1, IN_COLS), lambda i: (i, 0, 0)),
                     full(cos_s), full(sin_s), full(gn),
                     pl.BlockSpec(st_block, lambda i: (layer * steps + i, 0, 0, 0))]
        out_shape += [jax.ShapeDtypeStruct((db, 1, RET_V), F32), jax.ShapeDtypeStruct((db,) + st_block[1:], F32)]
        out_specs += [pl.BlockSpec((side_rows, 1, RET_V), lambda i: (i, 0, 0)),
                      pl.BlockSpec(st_block, lambda i: (i, 0, 0, 0))]
    outs = pl.pallas_call(
        functools.partial(_tail_kernel, with_side=side is not None),
        grid=(steps,),
        in_specs=in_specs,
        out_specs=out_specs,
        out_shape=out_shape,
        scratch_shapes=[pltpu.VMEM((HEADS, tm, HEAD_DIM), F32)],
        compiler_params=_cparams(1, 56),
        name="tail",
    )(*args)
    if side is None:
        return outs[0]
    return outs[0], outs[1].reshape(db, RET_V), outs[2]


def _rope_tables(pos):
    half = RET_DK // 2
    inv = ROPE_BASE ** (-jnp.arange(half, dtype=F32) / half)
    ang = pos.astype(F32)[:, None] * inv[None, :]
    cos, sin = jnp.cos(ang), jnp.sin(ang)
    return jnp.concatenate([cos, cos], axis=-1), jnp.concatenate([-sin, sin], axis=-1)


IN_PROJ_TM = 512
TAIL_TM = 512
RET_ROWS = 1024


def _layer(xp, xs, caches, state, layer, lw, log_g):
    batch, seq, _ = xp.shape
    db, t, _ = xs.shape
    assert t == 1
    xp2d, xs2d = xp.reshape(batch * seq, D_MODEL), xs.reshape(db, D_MODEL)

    z, w_att, w_rest = _in_proj_sample(xs2d, lw["g_pre_mix"], lw["w_in"])

    cos_p, sin_p = _rope_tables(jnp.arange(seq, dtype=jnp.int32))
    cos_s, sin_s = _rope_tables(PAST_LEN + jnp.arange(1, dtype=jnp.int32))
    outs = _in_proj_att(xp2d, lw["g_pre_mix"], w_att, lw["w_up"], lw["w_down"], batch, seq, IN_PROJ_TM)
    qkv = outs[:9]
    rows_p = [kv.reshape(batch, -1, 2, HEADS, HEAD_DIM) for kv in outs[9:12]]
    lw = dict(lw, w_up=outs[12], w_down=outs[13])
    rq, rk, rv, gsw, ga, gr, att_s = _in_proj_rest(
        xp2d, lw["g_pre_mix"], w_rest, cos_p, sin_p, z, caches, layer, batch, seq, IN_PROJ_TM)
    att = _attn_prompt(qkv, batch, seq)
    ret, state_p = _ret_prompt(log_g, rq, rk, rv, gsw, lw["g_ret_norm"], batch, seq, RET_ROWS)
    yp, ret_s, state_s = _tail(att, ret, ga, gr, xp2d, lw, TAIL_TM,
                               side=(log_g, z, cos_s, sin_s, lw["g_ret_norm"], state, layer))

    att_s = jnp.transpose(att_s, (1, 0, 2)).reshape(1, HEADS, 1, db, HEAD_DIM)
    ret_s = jnp.transpose(ret_s.reshape(db, RET_HEADS, RET_DV), (1, 0, 2))[None]
    ga_s = z[:, COL_GA * TN:COL_GA * TN + D_MODEL]
    gr_s = z[:, COL_GR * TN:COL_GR * TN + D_MODEL]
    ys = _tail(att_s, ret_s, ga_s, gr_s, xs2d, lw, db)
    rows_s = []
    for g in range(N_GROUPS):
        k_new = z[:, (COL_K + g) * TN:(COL_K + g + 1) * TN]
        v_new = z[:, (COL_V + g) * TN:(COL_V + g + 1) * TN]
        rows_s.append(jnp.stack([k_new, v_new], axis=1).reshape(db, 1, 2, HEADS, HEAD_DIM))
    return (yp.reshape(batch, seq, D_MODEL), rows_p, state_p), (ys.reshape(db, 1, D_MODEL), rows_s, state_s)


def _stack(xs):
    return xs[0][None] if len(xs) == 1 else jnp.stack(xs, axis=0)


def kernel(x_prompt, x_sample, cache_kv_d1, cache_kv_d4, cache_kv_d16, state_ret, w_in, w_att_br, w_ret_br, w_out, w_up, w_down, g_ret_norm, g_pre_mix, g_post_mix, g_pre_mlp, g_post_mlp):
    depth = w_in.shape[0]
    log_g = jnp.log1p(-jnp.power(2.0, -5.0 - jnp.arange(RET_HEADS, dtype=F32)))
    caches = (cache_kv_d1, cache_kv_d4, cache_kv_d16)
    xp, xs = x_prompt, x_sample
    p_rows, s_rows = [[], [], []], [[], [], []]
    p_states, s_states = [], []
    for l in range(depth):
        lw = {
            "w_in": w_in[l], "w_up": w_up[l], "w_down": w_down[l],
            "w_att_br": w_att_br[l].astype(BF16), "w_ret_br": w_ret_br[l].astype(BF16),
            "w_out": w_out[l].astype(BF16),
            "g_ret_norm": g_ret_norm[l].reshape(1, RET_V), "g_pre_mix": g_pre_mix[l].reshape(1, D_MODEL),
            "g_post_mix": g_post_mix[l].reshape(1, D_MODEL), "g_pre_mlp": g_pre_mlp[l].reshape(1, D_MODEL),
            "g_post_mlp": g_post_mlp[l].reshape(1, D_MODEL),
        }
        (xp, rows_p, sp), (xs, rows_s, ss) = _layer(xp, xs, caches, state_ret, l, lw, log_g)
        for g in range(N_GROUPS):
            p_rows[g].append(rows_p[g])
            s_rows[g].append(rows_s[g])
        p_states.append(sp)
        s_states.append(ss)
    return (xp, xs, _stack(p_rows[0]), _stack(p_rows[1]), _stack(p_rows[2]), _stack(p_states),
            _stack(s_rows[0]), _stack(s_rows[1]), _stack(s_rows[2]), _stack(s_states))
```

```python
import functools

import jax
import jax.numpy as jnp
from jax import lax
from jax.experimental import pallas as pl
from jax.experimental.pallas import tpu as pltpu

F32 = jnp.float32
BF16 = jnp.bfloat16

D_MODEL = 1024
PAST_LEN = 8192

DIL_GROUPS = ((128, 1), (512, 4), (2048, 16))
N_GROUPS = 3
HEADS = 4
HEAD_DIM = 128
ATT_OUT = HEADS * HEAD_DIM
ATT_COLS = N_GROUPS * ATT_OUT
BAND = 128

RET_HEADS = 4
RET_DK = 128
RET_DV = 256
RET_QK = RET_HEADS * RET_DK
RET_V = RET_HEADS * RET_DV
RET_CHUNK = 128
ROPE_BASE = 10000.0

D_FF = 4 * D_MODEL
IN_COLS = 3 * ATT_COLS + 2 * RET_QK + 2 * RET_V + 2 * D_MODEL
EPS = 1e-6
NEG_INF = -1e30

TN = 512
N_COL_TILES = IN_COLS // TN
COL_Q, COL_K, COL_V, COL_RQ, COL_RK, COL_RV, COL_GSW, COL_GA, COL_GR = 0, 3, 6, 9, 10, 11, 13, 15, 17

V7X_VMEM_BYTES = 64 * 1024 * 1024
MIB = 1024 * 1024


def _cparams(n_axes, vmem_mib):
    assert vmem_mib * MIB < V7X_VMEM_BYTES
    return pltpu.CompilerParams(
        dimension_semantics=("arbitrary",) * n_axes,
        vmem_limit_bytes=vmem_mib * MIB,
    )


def _resident(shape, index_map):
    return pl.BlockSpec(shape, index_map, pipeline_mode=pl.Buffered(1))


def _rms(x):
    return x * lax.rsqrt(jnp.mean(x * x, axis=-1, keepdims=True) + EPS)


def _sigmoid(x):
    return 1.0 / (1.0 + jnp.exp(-x))


def _silu(x):
    return x * _sigmoid(x)


def _dot(a, b):
    return jnp.dot(a, b, preferred_element_type=F32)


def _dot_nt(a, b):
    return lax.dot_general(a, b, (((1,), (1,)), ((), ())), preferred_element_type=F32)


def _rope(x, cos, sin):
    return x * cos + pltpu.roll(x, RET_DK // 2, 1) * sin


N_ATT_TILES = 3 * N_GROUPS
MAX_ROW_STRIDE = 4


def _norm_rows(x_ref, g_ref):
    return (_rms(x_ref[...]) * g_ref[...]).astype(BF16)


def _head(acc, hh, width=HEAD_DIM):
    return acc[:, hh * width:(hh + 1) * width]


def _in_proj_att_kernel(x_ref, g_ref, w_ref, wu_ref, wd_ref,
                        q0, q1, q2, k0, k1, k2, v0, v1, v2, kvo0, kvo1, kvo2, wu_bf_ref, wd_bf_ref,
                        scr_q, sk0, sk1, sk2, sv0, sv1, sv2, scr_mid, *, first_kept):
    tm = x_ref.shape[0]
    h = _norm_rows(x_ref, g_ref)
    wu_bf_ref[...] = wu_ref[...].astype(BF16)
    wd_bf_ref[...] = wd_ref[...].astype(BF16)

    def park(j, scr, scale=None):
        acc = _dot(h, w_ref[:, j * TN:(j + 1) * TN])
        for hh in range(HEADS):
            scr[hh] = _head(acc, hh) if scale is None else _head(acc, hh) * scale

    def residues(scr, dst, dil):
        s = MAX_ROW_STRIDE
        if dil <= s:
            for hh in range(HEADS):
                for c in range(dil):
                    rows = pl.ds(c, tm // dil, stride=dil) if dil > 1 else slice(None)
                    dst[hh, c] = scr[hh, rows, :].astype(BF16)
            return
        assert dil % s == 0 and dil // s <= s
        for hh in range(HEADS):
            for c in range(s):
                scr_mid[hh, c * (tm // s):(c + 1) * (tm // s), :] = scr[hh, pl.ds(c, tm // s, stride=s), :]
        for hh in range(HEADS):
            for c in range(dil):
                start = (c % s) * (tm // s) + c // s
                dst[hh, c] = scr_mid[hh, pl.ds(start, tm // dil, stride=dil // s), :].astype(BF16)

    for g, (_, dil) in enumerate(DIL_GROUPS):
        park(COL_Q + g, scr_q, HEAD_DIM ** -0.5)
        residues(scr_q, (q0, q1, q2)[g], dil)
        park(COL_K + g, (sk0, sk1, sk2)[g])
        residues((sk0, sk1, sk2)[g], (k0, k1, k2)[g], dil)
        park(COL_V + g, (sv0, sv1, sv2)[g])
        residues((sv0, sv1, sv2)[g], (v0, v1, v2)[g], dil)

    for g, kvo in enumerate((kvo0, kvo1, kvo2)):
        def write(g=g, kvo=kvo):
            keep = kvo.shape[0] // (2 * HEADS)
            for which, scr in enumerate(((sk0, sk1, sk2)[g], (sv0, sv1, sv2)[g])):
                for hh in range(HEADS):
                    kvo[pl.ds(which * HEADS + hh, keep, stride=2 * HEADS), :] = scr[hh, tm - keep:, :]
        if first_kept[g] == 0:
            write()
        else:
            pl.when(pl.program_id(1) >= first_kept[g])(write)


def _in_proj_att(x2d, g, w_att, w_up, w_down, batch, seq, tm):
    assert seq % tm == 0 and tm % (16 * DIL_GROUPS[-1][1]) == 0
    tps = seq // tm
    out_shape, out_specs, first_kept = [], [], []
    for _ in range(3):
        for _, dil in DIL_GROUPS:
            out_shape.append(jax.ShapeDtypeStruct((batch, HEADS, dil, seq // dil, HEAD_DIM), BF16))
            out_specs.append(pl.BlockSpec((None, HEADS, dil, tm // dil, HEAD_DIM), lambda b, t: (b, 0, 0, t, 0)))
    for win, _ in DIL_GROUPS:
        keep = min(win, seq)
        rows = min(keep, tm)
        assert keep % rows == 0
        first = (seq - keep) // rows
        first_kept.append((seq - keep) // tm)
        out_shape.append(jax.ShapeDtypeStruct((batch, keep * 2 * HEADS, HEAD_DIM), F32))
        out_specs.append(pl.BlockSpec(
            (None, rows * 2 * HEADS, HEAD_DIM),
            functools.partial(lambda b, t, first, per: (b, jnp.maximum((t + 1) * per - 1 - first, 0), 0),
                              first=first, per=tm // rows)))
    steps = batch * tps
    cast_rows = [w.shape[0] // steps for w in (w_up, w_down)]
    assert all(w.shape[0] % steps == 0 and r % 16 == 0 for w, r in zip((w_up, w_down), cast_rows))
    cast_specs = [pl.BlockSpec((r, w.shape[1]), lambda b, t: (b * tps + t, 0))
                  for w, r in zip((w_up, w_down), cast_rows)]
    return pl.pallas_call(
        functools.partial(_in_proj_att_kernel, first_kept=tuple(first_kept)),
        grid=(batch, tps),
        in_specs=[
            pl.BlockSpec((tm, D_MODEL), lambda b, t: (b * tps + t, 0)),
            _resident((1, D_MODEL), lambda b, t: (0, 0)),
            _resident(w_att.shape, lambda b, t: (0, 0)),
        ] + cast_specs,
        out_specs=out_specs + cast_specs,
        out_shape=out_shape + [jax.ShapeDtypeStruct(w.shape, BF16) for w in (w_up, w_down)],
        scratch_shapes=[pltpu.VMEM((HEADS, tm, HEAD_DIM), F32)] * (2 + 2 * N_GROUPS),
        compiler_params=_cparams(2, 48),
        name="in_proj_att",
    )(x2d, g, w_att, w_up, w_down)


def _in_proj_rest_kernel(x_ref, g_ref, w_ref, cos_ref, sin_ref, z_ref, c0, c1, c2,
                         rq_ref, rk_ref, rv_ref, gsw_ref, ga_ref, gr_ref, att_s_ref):
    h = _norm_rows(x_ref, g_ref)
    rows_left = list(range(z_ref.shape[0]))

    def side_row():
        if rows_left:
            r = rows_left.pop(0)
            att_s_ref[r] = _sample_attend_row(z_ref, (c0, c1, c2), r)

    def tile(j):
        lo = (j - N_ATT_TILES) * TN
        return _dot(h, w_ref[:, lo:lo + TN])

    def ret_qk(col, dst, scale):
        acc = tile(col)
        cos, sin = cos_ref[...], sin_ref[...]
        for hh in range(RET_HEADS):
            r = _rope(_head(acc, hh), cos, sin)
            dst[hh] = r if scale is None else r * scale

    def ret_wide(col, dst, e, act=None):
        acc = tile(col + e)
        for s in range(2):
            part = _head(acc, s, RET_DV)
            dst[2 * e + s] = (part if act is None else act(part)).astype(dst.dtype)

    def gate(col, dst, e):
        dst[:, e * TN:(e + 1) * TN] = tile(col + e)

    ret_qk(COL_RQ, rq_ref, None)
    side_row()
    ret_qk(COL_RK, rk_ref, RET_DK ** -0.5)
    side_row()
    for e in range(2):
        ret_wide(COL_RV, rv_ref, e)
        ret_wide(COL_GSW, gsw_ref, e, _silu)
        side_row()
        gate(COL_GA, ga_ref, e)
        gate(COL_GR, gr_ref, e)
    while rows_left:
        side_row()


def _in_proj_rest(x2d, g, w_rest, cos_t, sin_t, z, caches, layer, batch, seq, tm):
    assert seq % tm == 0
    tps = seq // tm
    steps = batch * tps
    db = z.shape[0]
    assert db % steps == 0
    side_rows = db // steps
    views, cache_specs = [], []
    for (win, dil), c in zip(DIL_GROUPS, caches):
        assert c.shape[1] == db and c.shape[2] == win and win // dil == BAND
        views.append(c.reshape(c.shape[0] * db, BAND, dil, 2 * HEADS, HEAD_DIM))
        cache_specs.append(pl.BlockSpec((side_rows, BAND, None, 2 * HEADS, HEAD_DIM),
                                        lambda b, t: (layer * steps + b * tps + t, 0, 0, 0, 0)))
    hm = lambda width, dt: (jax.ShapeDtypeStruct((batch, RET_HEADS, seq, width), dt),
                            pl.BlockSpec((None, RET_HEADS, tm, width), lambda b, t: (b, 0, t, 0)))
    nat = (jax.ShapeDtypeStruct((batch * seq, D_MODEL), F32),
           pl.BlockSpec((tm, D_MODEL), lambda b, t: (b * tps + t, 0)))
    outs = (hm(RET_DK, F32), hm(RET_DK, F32), hm(RET_DV, BF16), hm(RET_DV, F32), nat, nat)
    side_outs = ((jax.ShapeDtypeStruct((db, HEADS, HEAD_DIM), F32),
                  pl.BlockSpec((side_rows, HEADS, HEAD_DIM), lambda b, t: (b * tps + t, 0, 0))),)
    return pl.pallas_call(
        _in_proj_rest_kernel,
        grid=(batch, tps),
        in_specs=[
            pl.BlockSpec((tm, D_MODEL), lambda b, t: (b * tps + t, 0)),
            _resident((1, D_MODEL), lambda b, t: (0, 0)),
            _resident(w_rest.shape, lambda b, t: (0, 0)),
            pl.BlockSpec((tm, RET_DK), lambda b, t: (t, 0)),
            pl.BlockSpec((tm, RET_DK), lambda b, t: (t, 0)),
            pl.BlockSpec((side_rows, 1, IN_COLS), lambda b, t: (b * tps + t, 0, 0)),
        ] + cache_specs,
        out_specs=[spec for _, spec in outs + side_outs],
        out_shape=[shp for shp, _ in outs + side_outs],
        compiler_params=_cparams(2, 56),
        name="in_proj_rest",
    )(x2d, g, w_rest, cos_t, sin_t, z.reshape(db, 1, IN_COLS), *views)


def _in_proj_sample_kernel(x_ref, g_ref, w_ref, z_ref, w_att_ref, w_rest_ref, h_ref):
    j = pl.program_id(0)

    @pl.when(j == 0)
    def _():
        h_ref[...] = _norm_rows(x_ref, g_ref)

    w_bf = w_ref[...].astype(BF16)

    @pl.when(j < N_ATT_TILES)
    def _():
        w_att_ref[...] = w_bf

    @pl.when(j >= N_ATT_TILES)
    def _():
        w_rest_ref[...] = w_bf

    z_ref[...] = _dot(h_ref[...], w_bf)


def _in_proj_sample(x2d, g, w):
    m = x2d.shape[0]
    col_tile = lambda rows, f: pl.BlockSpec((rows, TN), lambda j: (0, f(j)))
    n_rest = N_COL_TILES - N_ATT_TILES
    return pl.pallas_call(
        _in_proj_sample_kernel,
        grid=(N_COL_TILES,),
        in_specs=[
            _resident((m, D_MODEL), lambda j: (0, 0)),
            _resident((1, D_MODEL), lambda j: (0, 0)),
            col_tile(D_MODEL, lambda j: j),
        ],
        out_specs=[col_tile(m, lambda j: j),
                   col_tile(D_MODEL, lambda j: jnp.minimum(j, N_ATT_TILES - 1)),
                   col_tile(D_MODEL, lambda j: jnp.maximum(j - N_ATT_TILES, 0))],
        out_shape=[jax.ShapeDtypeStruct((m, IN_COLS), F32),
                   jax.ShapeDtypeStruct((D_MODEL, N_ATT_TILES * TN), BF16),
                   jax.ShapeDtypeStruct((D_MODEL, n_rest * TN), BF16)],
        scratch_shapes=[pltpu.VMEM((m, D_MODEL), BF16)],
        compiler_params=_cparams(1, 24),
        name="in_proj_sample",
    )(x2d, g, w)


def _softmax_block(s, v):
    m = jnp.max(s, axis=1, keepdims=True)
    p = jnp.exp(s - m).astype(BF16)
    uv = _dot(p, jnp.concatenate([v, jnp.ones_like(v)], axis=1))
    return uv[:, :HEAD_DIM], jnp.broadcast_to(m, (s.shape[0], HEAD_DIM)), uv[:, HEAD_DIM:]


ATT_BLOCKS_PER_STEP = 16
ATT_ORDER = MAX_ROW_STRIDE


def _attn_prompt_kernel(q0, q1, q2, k0, k1, k2, v0, v1, v2, o_ref,
                        u0, u1, u2, m0, m1, m2, l0, l1, l2, *, seq):
    qs, ks, vs = (q0, q1, q2), (k0, k1, k2), (v0, v1, v2)
    us, ms, ls = (u0, u1, u2), (m0, m1, m2), (l0, l1, l2)
    order = ATT_ORDER
    n_out = seq // order
    row1 = lax.broadcasted_iota(jnp.int32, (BAND, BAND), 0)
    col1 = lax.broadcasted_iota(jnp.int32, (BAND, BAND), 1)
    causal = col1 <= row1
    row2 = lax.broadcasted_iota(jnp.int32, (BAND, 2 * BAND), 0)
    col2 = lax.broadcasted_iota(jnp.int32, (BAND, 2 * BAND), 1)
    band = (col2 >= row2) & (col2 <= row2 + BAND)

    aligned = lambda r: r if isinstance(r, int) else pl.multiple_of(r, BAND)

    for g, (_, dil) in enumerate(DIL_GROUPS):
        n = seq // dil
        nb = n // BAND

        def run_blocks(blocks, g=g, n=n, dil=dil):
            scores, values = [], []
            for c, blk, has_prev in blocks:
                own = pl.ds(aligned(blk * BAND), BAND)
                keys = pl.ds(aligned((blk - 1) * BAND), 2 * BAND) if has_prev else own
                mask = band if has_prev else causal
                scores.append(jnp.where(mask, _dot_nt(qs[g][c, own, :], ks[g][c, keys, :]), NEG_INF))
                values.append(vs[g][c, keys, :])
            results = [_softmax_block(s, v) for s, v in zip(scores, values)]
            for (c, blk, _), (u, m, l) in zip(blocks, results):
                if dil <= order:
                    rs = pl.ds(aligned(c * n + blk * BAND), BAND)
                else:
                    step = dil // order
                    rs = pl.ds((c % order) * n_out + c // order + blk * (BAND * step), BAND, stride=step)
                us[g][rs, :] = u
                ms[g][rs, :] = m
                ls[g][rs, :] = l

        per = ATT_BLOCKS_PER_STEP
        assert (dil * nb) % per == 0 and (per % nb == 0 or nb % per == 0)
        if nb >= per:
            def body(i, carry, run_blocks=run_blocks, steps=nb // per):
                c, i0 = i // steps, (i % steps) * per
                run_blocks([(c, i0 + e, True) for e in range(per)])
                return carry
            for c in range(dil):
                run_blocks([(c, e, e > 0) for e in range(per)])
                lax.fori_loop(c * (nb // per) + 1, (c + 1) * (nb // per), body, 0)
        else:
            def body(i, carry, run_blocks=run_blocks, nb=nb, cps=per // nb):
                run_blocks([(i * cps + e // nb, e % nb, e % nb > 0) for e in range(per)])
                return carry
            lax.fori_loop(0, dil * nb // per, body, 0)

    chunks = n_out // BAND

    def combine_one(j):
        c, t = j // chunks, j % chunks
        rows = []
        for _, dil in DIL_GROUPS:
            r = min(dil, order)
            step = order // r
            start = (c % r) * (seq // r) + c // r + t * (BAND * step)
            rows.append(pl.ds(aligned(start), BAND) if step == 1 else pl.ds(start, BAND, stride=step))
        m_g = [ms[g][rows[g], :] for g in range(N_GROUPS)]
        m = jnp.maximum(jnp.maximum(m_g[0], m_g[1]), m_g[2])
        num = jnp.zeros((BAND, HEAD_DIM), F32)
        den = jnp.zeros((BAND, HEAD_DIM), F32)
        for g in range(N_GROUPS):
            w = jnp.exp(m_g[g] - m)
            num = num + w * us[g][rows[g], :]
            den = den + w * ls[g][rows[g], :]
        return num / den

    per = 4
    assert (order * chunks) % per == 0

    def combine(i, carry):
        outs = [combine_one(i * per + e) for e in range(per)]
        for e, o in enumerate(outs):
            o_ref[pl.ds(aligned((i * per + e) * BAND), BAND), :] = o
        return carry

    lax.fori_loop(0, order * chunks // per, combine, 0)


def _attn_prompt(qkv, batch, seq):
    assert seq % (BAND * DIL_GROUPS[-1][1]) == 0
    assert all(ATT_ORDER % d == 0 or d % ATT_ORDER == 0 for _, d in DIL_GROUPS)
    in_specs = [pl.BlockSpec((None, None) + a.shape[2:], lambda b, h: (b, h, 0, 0, 0)) for a in qkv]
    out = pl.pallas_call(
        functools.partial(_attn_prompt_kernel, seq=seq),
        grid=(batch, HEADS),
        in_specs=in_specs,
        out_specs=pl.BlockSpec((None, None, seq, HEAD_DIM), lambda b, h: (b, h, 0, 0)),
        out_shape=jax.ShapeDtypeStruct((batch, HEADS, seq, HEAD_DIM), F32),
        scratch_shapes=[pltpu.VMEM((seq, HEAD_DIM), F32)] * 9,
        compiler_params=_cparams(2, 32),
        name="attend_prompt",
    )(*qkv)
    return out.reshape(batch, HEADS, ATT_ORDER, seq // ATT_ORDER, HEAD_DIM)


def _group_norm_gate(o, gn, gate):
    mu = jnp.mean(o, axis=-1, keepdims=True)
    d = o - mu
    var = jnp.mean(d * d, axis=-1, keepdims=True)
    return gate * (d * lax.rsqrt(var + EPS) * gn)


RET_CHUNKS_PER_STEP = 8


def _ret_prompt_kernel(lg_ref, q_ref, k_ref, v_ref, gate_ref, gn_ref, o_ref, s_out_ref, s_scr):
    c = RET_CHUNK
    rows = q_ref.shape[1]
    t = lax.broadcasted_iota(jnp.int32, (c, 1), 0).astype(F32)
    rel = (lax.broadcasted_iota(jnp.int32, (c, c), 0) - lax.broadcasted_iota(jnp.int32, (c, c), 1)).astype(F32)

    @pl.when(pl.program_id(1) == 0)
    def _():
        s_scr[...] = jnp.zeros_like(s_scr)

    consts = []
    for h in range(RET_HEADS):
        lg = lg_ref[h]
        consts.append((
            jnp.where(rel >= 0, jnp.exp(lg * jnp.maximum(rel, 0.0)), 0.0),
            jnp.exp(lg * (t + 1.0)),
            jnp.exp(lg * (c - 1.0 - t)),
            jnp.exp(lg * jnp.full((1, RET_DV), float(c), F32)),
        ))

    per = RET_CHUNKS_PER_STEP
    assert (rows // c) % per == 0

    def step(i, carry):
        rss = [pl.ds(pl.multiple_of((i * per + e) * c, c), c) for e in range(per)]
        local = {}
        for h in range(RET_HEADS):
            decay, inner, tail, _ = consts[h]
            for e, rs in enumerate(rss):
                q, k, vb = q_ref[h, rs, :], k_ref[h, rs, :], v_ref[h, rs, :]
                a = _dot_nt(q.astype(BF16), k.astype(BF16)) * decay
                local[h, e] = (_dot(a.astype(BF16), vb),
                               (q * inner).astype(BF16),
                               _dot((k * tail).T.astype(BF16), vb))
        outs = {}
        for h in range(RET_HEADS):
            s = s_scr[h]
            for e in range(per):
                intra, q_in, kv = local[h, e]
                outs[h, e] = intra + _dot(q_in, s.astype(BF16))
                s = consts[h][3] * s + kv
            s_scr[h] = s
        for h in range(RET_HEADS):
            gn = gn_ref[:, h * RET_DV:(h + 1) * RET_DV]
            for e, rs in enumerate(rss):
                o_ref[h, rs, :] = _group_norm_gate(outs[h, e], gn, gate_ref[h, rs, :]).astype(o_ref.dtype)
        return carry

    lax.fori_loop(0, rows // (c * per), step, 0)

    @pl.when(pl.program_id(1) == pl.num_programs(1) - 1)
    def _():
        s_out_ref[...] = s_scr[...]


def _ret_prompt(log_g, rq, rk, rv, gsw, gn, batch, seq, rows):
    assert seq % rows == 0 and rows % RET_CHUNK == 0
    hm = lambda width: pl.BlockSpec((None, RET_HEADS, rows, width), lambda b, t: (b, 0, t, 0))
    return pl.pallas_call(
        _ret_prompt_kernel,
        grid=(batch, seq // rows),
        in_specs=[
            pl.BlockSpec(memory_space=pltpu.SMEM),
            hm(RET_DK), hm(RET_DK), hm(RET_DV), hm(RET_DV),
            _resident((1, RET_V), lambda b, t: (0, 0)),
        ],
        out_specs=[
            hm(RET_DV),
            pl.BlockSpec((None, RET_HEADS, RET_DK, RET_DV), lambda b, t: (b, 0, 0, 0)),
        ],
        out_shape=[
            jax.ShapeDtypeStruct((batch, RET_HEADS, seq, RET_DV), BF16),
            jax.ShapeDtypeStruct((batch, RET_HEADS, RET_DK, RET_DV), F32),
        ],
        scratch_shapes=[pltpu.VMEM((RET_HEADS, RET_DK, RET_DV), F32)],
        compiler_params=_cparams(2, 40),
        name="retain_prompt",
    )(log_g, rq, rk, rv, gsw, gn)


def _sample_attend_row(z_ref, caches, r):
    half = BAND // 2

    def heads(tile0, g):
        base = (tile0 + g) * TN
        return jnp.concatenate(
            [z_ref[r, :, base + hh * HEAD_DIM:base + (hh + 1) * HEAD_DIM] for hh in range(HEADS)], axis=0)

    def paired(ref, lo):
        return jnp.concatenate([ref[r, 0:half, lo:lo + HEADS, :], ref[r, half:BAND, lo:lo + HEADS, :]], axis=1)

    twice = lambda a: jnp.concatenate([a, a], axis=0)
    fold = lambda a, op: op(a[0:HEADS], a[HEADS:2 * HEADS])

    us, ms, ls = [], [], []
    for g in range(N_GROUPS):
        q = heads(COL_Q, g) * (HEAD_DIM ** -0.5)
        k_new, v_new = heads(COL_K, g), heads(COL_V, g)
        k_buf, v_buf = paired(caches[g], 0), paired(caches[g], HEADS)
        s_buf = jnp.sum(k_buf * twice(q)[None], axis=-1, keepdims=True)
        s_new = jnp.sum(k_new * q, axis=-1, keepdims=True)
        m = jnp.maximum(fold(jnp.max(s_buf, axis=0), jnp.maximum), s_new)
        p_buf = jnp.exp(s_buf - twice(m)[None])
        p_new = jnp.exp(s_new - m)
        ls.append(fold(jnp.sum(p_buf, axis=0), jnp.add) + p_new)
        us.append(fold(jnp.sum(p_buf * v_buf, axis=0), jnp.add) + p_new * v_new)
        ms.append(m)
    m = jnp.maximum(jnp.maximum(ms[0], ms[1]), ms[2])
    num = jnp.zeros((HEADS, HEAD_DIM), F32)
    den = jnp.zeros((HEADS, 1), F32)
    for g in range(N_GROUPS):
        w = jnp.exp(ms[g] - m)
        num = num + w * us[g]
        den = den + w * ls[g]
    return num / den


def _sample_retain_head(lg_ref, z_ref, cos, sin, gn_ref, s_ref, ret_ref, s_out_ref, h):
    rows = z_ref.shape[0]
    eye = (lax.broadcasted_iota(jnp.int32, (RET_DK, RET_DK), 0)
           == lax.broadcasted_iota(jnp.int32, (RET_DK, RET_DK), 1))

    def column(v):
        return jnp.sum(jnp.where(eye, jnp.broadcast_to(v, (RET_DK, RET_DK)), 0.0), axis=1, keepdims=True)

    def zcols(tile0, width):
        return z_ref[:, 0, tile0 * TN + h * width:tile0 * TN + (h + 1) * width]

    gamma = jnp.exp(lg_ref[h] * jnp.ones((1, RET_DV), F32))
    q_h = _rope(zcols(COL_RQ, RET_DK), cos, sin)
    k_h = _rope(zcols(COL_RK, RET_DK), cos, sin) * (RET_DK ** -0.5)
    v_h = zcols(COL_RV, RET_DV)
    outs = []
    for r in range(rows):
        s_new = gamma * s_ref[r, h] + column(k_h[r:r + 1, :]) * v_h[r:r + 1, :]
        s_out_ref[r, h] = s_new
        outs.append(jnp.sum(column(q_h[r:r + 1, :]) * s_new, axis=0, keepdims=True))
    seg = slice(h * RET_DV, (h + 1) * RET_DV)
    ret_ref[:, 0, seg] = _group_norm_gate(jnp.concatenate(outs, axis=0), gn_ref[:, seg], _silu(zcols(COL_GSW, RET_DV)))


FF_CHUNK = 1024


def _tail_kernel(*refs, with_side):
    (att_ref, ret_ref, ga_ref, gr_ref, x_ref, wa_ref, wr_ref, wo_ref, gpost_mix_ref, gpre_mlp_ref,
     wu_ref, wd_ref, gpost_mlp_ref) = refs[:13]
    att_scr = refs[-1]
    side = []
    if with_side:
        lg_ref, z_ref, cos_ref, sin_ref, gn_ref, s_ref, y_ref, ret_s_ref, s_out_ref = refs[13:-1]
        side = [functools.partial(_sample_retain_head, lg_ref, z_ref, cos_ref[...], sin_ref[...], gn_ref, s_ref,
                                  ret_s_ref, s_out_ref, hh) for hh in range(RET_HEADS)]
    else:
        y_ref, = refs[13:-1]
    r = att_ref.shape[1]
    for hh in range(HEADS):
        for c in range(r):
            rs = pl.ds(c, att_ref.shape[2], stride=r) if r > 1 else slice(None)
            att_scr[hh, rs, :] = att_ref[hh, c]
    att = jnp.concatenate([att_scr[hh].astype(BF16) for hh in range(HEADS)], axis=1)
    ret = jnp.concatenate([ret_ref[hh].astype(BF16) for hh in range(RET_HEADS)], axis=1)
    m = _sigmoid(ga_ref[...]) * _dot(att, wa_ref[...]) + _sigmoid(gr_ref[...]) * _dot(ret, wr_ref[...])
    x1 = x_ref[...] + _rms(_dot(m.astype(BF16), wo_ref[...])) * gpost_mix_ref[...]
    h = (_rms(x1) * gpre_mlp_ref[...]).astype(BF16)

    n_chunks = D_FF // FF_CHUNK
    acc = jnp.zeros(x_ref.shape, F32)
    for c in range(n_chunks):
        cs = slice(c * FF_CHUNK, (c + 1) * FF_CHUNK)
        u = jnp.maximum(_dot(h, wu_ref[:, cs]), 0.0)
        acc = acc + _dot((u * u).astype(BF16), wd_ref[cs, :])
        for work in side[c::n_chunks]:
            work()
    y_ref[...] = x1 + _rms(acc) * gpost_mlp_ref[...]


def _tail(att, ret, ga, gr, x2d, lw, tm, side=None):
    batch, _, r, ni, _ = att.shape
    seq = r * ni
    m = batch * seq
    assert seq % tm == 0 and tm % r == 0 and x2d.shape[0] == m
    tps = seq // tm
    steps = batch * tps
    rows = pl.BlockSpec((tm, D_MODEL), lambda i: (i, 0))
    full = lambda a: _resident(a.shape, lambda i: (0,) * a.ndim)
    weights = [lw[k] for k in ("w_att_br", "w_ret_br", "w_out", "g_post_mix", "g_pre_mlp",
                               "w_up", "w_down", "g_post_mlp")]
    args = [att, ret, ga, gr, x2d] + weights
    in_specs = [pl.BlockSpec((None, HEADS, r, tm // r, HEAD_DIM), lambda i: (i // tps, 0, 0, i % tps, 0)),
                pl.BlockSpec((None, RET_HEADS, tm, RET_DV), lambda i: (i // tps, 0, i % tps, 0)),
                rows, rows, rows] + [full(w) for w in weights]
    out_shape = [jax.ShapeDtypeStruct((m, D_MODEL), F32)]
    out_specs = [rows]
    if side is not None:
        log_g, z, cos_s, sin_s, gn, state, layer = side
        db = z.shape[0]
        assert db % steps == 0
        side_rows = db // steps
        st_block = (side_rows, RET_HEADS, RET_DK, RET_DV)
        args += [log_g, z.reshape(db, 1, IN_COLS), cos_s, sin_s, gn,
                 state.reshape((state.shape[0] * db,) + state.shape[2:])]
        in_specs += [pl.BlockSpec(memory_space=pltpu.SMEM),
                     pl.BlockSpec((side_rows, 1, IN_COLS), lambda i: (i, 0, 0)),
                     full(cos_s), full(sin_s), full(gn),
                     pl.BlockSpec(st_block, lambda i: (layer * steps + i, 0, 0, 0))]
        out_shape += [jax.ShapeDtypeStruct((db, 1, RET_V), F32), jax.ShapeDtypeStruct((db,) + st_block[1:], F32)]
        out_specs += [pl.BlockSpec((side_rows, 1, RET_V), lambda i: (i, 0, 0)),
                      pl.BlockSpec(st_block, lambda i: (i, 0, 0, 0))]
    outs = pl.pallas_call(
        functools.partial(_tail_kernel, with_side=side is not None),
        grid=(steps,),
        in_specs=in_specs,
        out_specs=out_specs,
        out_shape=out_shape,
        scratch_shapes=[pltpu.VMEM((HEADS, tm, HEAD_DIM), F32)],
        compiler_params=_cparams(1, 56),
        name="tail",
    )(*args)
    if side is None:
        return outs[0]
    return outs[0], outs[1].reshape(db, RET_V), outs[2]


def _rope_tables(pos):
    half = RET_DK // 2
    inv = ROPE_BASE ** (-jnp.arange(half, dtype=F32) / half)
    ang = pos.astype(F32)[:, None] * inv[None, :]
    cos, sin = jnp.cos(ang), jnp.sin(ang)
    return jnp.concatenate([cos, cos], axis=-1), jnp.concatenate([-sin, sin], axis=-1)


IN_PROJ_TM = 512
TAIL_TM = 512
RET_ROWS = 1024


def _layer(xp, xs, caches, state, layer, lw, log_g):
    batch, seq, _ = xp.shape
    db, t, _ = xs.shape
    assert t == 1
    xp2d, xs2d = xp.reshape(batch * seq, D_MODEL), xs.reshape(db, D_MODEL)

    z, w_att, w_rest = _in_proj_sample(xs2d, lw["g_pre_mix"], lw["w_in"])

    cos_p, sin_p = _rope_tables(jnp.arange(seq, dtype=jnp.int32))
    cos_s, sin_s = _rope_tables(PAST_LEN + jnp.arange(1, dtype=jnp.int32))
    outs = _in_proj_att(xp2d, lw["g_pre_mix"], w_att, lw["w_up"], lw["w_down"], batch, seq, IN_PROJ_TM)
    qkv = outs[:9]
    rows_p = [kv.reshape(batch, -1, 2, HEADS, HEAD_DIM) for kv in outs[9:12]]
    lw = dict(lw, w_up=outs[12], w_down=outs[13])
    rq, rk, rv, gsw, ga, gr, att_s = _in_proj_rest(
        xp2d, lw["g_pre_mix"], w_rest, cos_p, sin_p, z, caches, layer, batch, seq, IN_PROJ_TM)
    att = _attn_prompt(qkv, batch, seq)
    ret, state_p = _ret_prompt(log_g, rq, rk, rv, gsw, lw["g_ret_norm"], batch, seq, RET_ROWS)
    yp, ret_s, state_s = _tail(att, ret, ga, gr, xp2d, lw, TAIL_TM,
                               side=(log_g, z, cos_s, sin_s, lw["g_ret_norm"], state, layer))

    att_s = jnp.transpose(att_s, (1, 0, 2)).reshape(1, HEADS, 1, db, HEAD_DIM)
    ret_s = jnp.transpose(ret_s.reshape(db, RET_HEADS, RET_DV), (1, 0, 2))[None]
    ga_s = z[:, COL_GA * TN:COL_GA * TN + D_MODEL]
    gr_s = z[:, COL_GR * TN:COL_GR * TN + D_MODEL]
    ys = _tail(att_s, ret_s, ga_s, gr_s, xs2d, lw, db)
    rows_s = []
    for g in range(N_GROUPS):
        k_new = z[:, (COL_K + g) * TN:(COL_K + g + 1) * TN]
        v_new = z[:, (COL_V + g) * TN:(COL_V + g + 1) * TN]
        rows_s.append(jnp.stack([k_new, v_new], axis=1).reshape(db, 1, 2, HEADS, HEAD_DIM))
    return (yp.reshape(batch, seq, D_MODEL), rows_p, state_p), (ys.reshape(db, 1, D_MODEL), rows_s, state_s)


def _stack(xs):
    return xs[0][None] if len(xs) == 1 else jnp.stack(xs, axis=0)


def kernel(x_prompt, x_sample, cache_kv_d1, cache_kv_d4, cache_kv_d16, state_ret, w_in, w_att_br, w_ret_br, w_out, w_up, w_down, g_ret_norm, g_pre_mix, g_post_mix, g_pre_mlp, g_post_mlp):
    depth = w_in.shape[0]
    log_g = jnp.log1p(-jnp.power(2.0, -5.0 - jnp.arange(RET_HEADS, dtype=F32)))
    caches = (cache_kv_d1, cache_kv_d4, cache_kv_d16)
    xp, xs = x_prompt, x_sample
    p_rows, s_rows = [[], [], []], [[], [], []]
    p_states, s_states = [], []
    for l in range(depth):
        lw = {
            "w_in": w_in[l], "w_up": w_up[l], "w_down": w_down[l],
            "w_att_br": w_att_br[l].astype(BF16), "w_ret_br": w_ret_br[l].astype(BF16),
            "w_out": w_out[l].astype(BF16),
            "g_ret_norm": g_ret_norm[l].reshape(1, RET_V), "g_pre_mix": g_pre_mix[l].reshape(1, D_MODEL),
            "g_post_mix": g_post_mix[l].reshape(1, D_MODEL), "g_pre_mlp": g_pre_mlp[l].reshape(1, D_MODEL),
            "g_post_mlp": g_post_mlp[l].reshape(1, D_MODEL),
        }
        (xp, rows_p, sp), (xs, rows_s, ss) = _layer(xp, xs, caches, state_ret, l, lw, log_g)
        for g in range(N_GROUPS):
            p_rows[g].append(rows_p[g])
            s_rows[g].append(rows_s[g])
        p_states.append(sp)
        s_states.append(ss)
    return (xp, xs, _stack(p_rows[0]), _stack(p_rows[1]), _stack(p_rows[2]), _stack(p_states),
            _stack(s_rows[0]), _stack(s_rows[1]), _stack(s_rows[2]), _stack(s_states))
```

```python
import functools

import jax
import jax.numpy as jnp
from jax import lax
from jax.experimental import pallas as pl
from jax.experimental.pallas import tpu as pltpu

F32 = jnp.float32
BF16 = jnp.bfloat16

D_MODEL = 1024
PAST_LEN = 8192

DIL_GROUPS = ((128, 1), (512, 4), (2048, 16))
N_GROUPS = 3
HEADS = 4
HEAD_DIM = 128
ATT_OUT = HEADS * HEAD_DIM
ATT_COLS = N_GROUPS * ATT_OUT
BAND = 128

RET_HEADS = 4
RET_DK = 128
RET_DV = 256
RET_QK = RET_HEADS * RET_DK
RET_V = RET_HEADS * RET_DV
RET_CHUNK = 128
ROPE_BASE = 10000.0

D_FF = 4 * D_MODEL
IN_COLS = 3 * ATT_COLS + 2 * RET_QK + 2 * RET_V + 2 * D_MODEL
EPS = 1e-6
NEG_INF = -1e30

TN = 512
N_COL_TILES = IN_COLS // TN
COL_Q, COL_K, COL_V, COL_RQ, COL_RK, COL_RV, COL_GSW, COL_GA, COL_GR = 0, 3, 6, 9, 10, 11, 13, 15, 17

V7X_VMEM_BYTES = 64 * 1024 * 1024
MIB = 1024 * 1024


def _cparams(n_axes, vmem_mib):
    assert vmem_mib * MIB < V7X_VMEM_BYTES
    return pltpu.CompilerParams(
        dimension_semantics=("arbitrary",) * n_axes,
        vmem_limit_bytes=vmem_mib * MIB,
    )


def _resident(shape, index_map):
    return pl.BlockSpec(shape, index_map, pipeline_mode=pl.Buffered(1))


def _rms(x):
    return x * lax.rsqrt(jnp.mean(x * x, axis=-1, keepdims=True) + EPS)


def _sigmoid(x):
    return 1.0 / (1.0 + jnp.exp(-x))


def _silu(x):
    return x * _sigmoid(x)


def _dot(a, b):
    return jnp.dot(a, b, preferred_element_type=F32)


def _dot_nt(a, b):
    return lax.dot_general(a, b, (((1,), (1,)), ((), ())), preferred_element_type=F32)


def _rope(x, cos, sin):
    return x * cos + pltpu.roll(x, RET_DK // 2, 1) * sin


N_ATT_TILES = 3 * N_GROUPS
MAX_ROW_STRIDE = 4


def _norm_rows(x_ref, g_ref):
    return (_rms(x_ref[...]) * g_ref[...]).astype(BF16)


def _head(acc, hh, width=HEAD_DIM):
    return acc[:, hh * width:(hh + 1) * width]


def _in_proj_att_kernel(*refs, first_kept, n_cast):
    x_ref, g_ref, w_ref = refs[:3]
    cast_in, outs = refs[3:3 + n_cast], refs[3 + n_cast:]
    q0, q1, q2, k0, k1, k2, v0, v1, v2, kvo0, kvo1, kvo2 = outs[:12]
    cast_out = outs[12:12 + n_cast]
    scr_q, sk0, sk1, sk2, sv0, sv1, sv2, scr_mid = outs[12 + n_cast:]
    tm = x_ref.shape[0]
    h = _norm_rows(x_ref, g_ref)
    for src, dst in zip(cast_in, cast_out):
        dst[...] = src[...].astype(BF16)

    def park(j, scr, scale=None):
        acc = _dot(h, w_ref[:, j * TN:(j + 1) * TN])
        for hh in range(HEADS):
            scr[hh] = _head(acc, hh) if scale is None else _head(acc, hh) * scale

    def residues(scr, dst, dil):
        s = MAX_ROW_STRIDE
        if dil <= s:
            for hh in range(HEADS):
                for c in range(dil):
                    rows = pl.ds(c, tm // dil, stride=dil) if dil > 1 else slice(None)
                    dst[hh, c] = scr[hh, rows, :].astype(BF16)
            return
        assert dil % s == 0 and dil // s <= s
        for hh in range(HEADS):
            for c in range(s):
                scr_mid[hh, c * (tm // s):(c + 1) * (tm // s), :] = scr[hh, pl.ds(c, tm // s, stride=s), :]
        for hh in range(HEADS):
            for c in range(dil):
                start = (c % s) * (tm // s) + c // s
                dst[hh, c] = scr_mid[hh, pl.ds(start, tm // dil, stride=dil // s), :].astype(BF16)

    def kv_rows(g):
        kvo = (kvo0, kvo1, kvo2)[g]
        keep = kvo.shape[0] // (2 * HEADS)
        for which, scr in enumerate(((sk0, sk1, sk2)[g], (sv0, sv1, sv2)[g])):
            for hh in range(HEADS):
                kvo[pl.ds(which * HEADS + hh, keep, stride=2 * HEADS), :] = scr[hh, tm - keep:, :]

    for g in sorted(range(N_GROUPS), key=lambda g_: first_kept[g_]):
        dil = DIL_GROUPS[g][1]
        park(COL_K + g, (sk0, sk1, sk2)[g])
        residues((sk0, sk1, sk2)[g], (k0, k1, k2)[g], dil)
        park(COL_V + g, (sv0, sv1, sv2)[g])
        residues((sv0, sv1, sv2)[g], (v0, v1, v2)[g], dil)
        if first_kept[g] == 0:
            kv_rows(g)
        park(COL_Q + g, scr_q, HEAD_DIM ** -0.5)
        residues(scr_q, (q0, q1, q2)[g], dil)

    for g in range(N_GROUPS):
        if first_kept[g] > 0:
            pl.when(pl.program_id(1) >= first_kept[g])(functools.partial(kv_rows, g))


def _in_proj_att(x2d, g, w_att, cast_ws, batch, seq, tm):
    assert seq % tm == 0 and tm % (16 * DIL_GROUPS[-1][1]) == 0
    tps = seq // tm
    out_shape, out_specs, first_kept = [], [], []
    for _ in range(3):
        for _, dil in DIL_GROUPS:
            out_shape.append(jax.ShapeDtypeStruct((batch, HEADS, dil, seq // dil, HEAD_DIM), BF16))
            out_specs.append(pl.BlockSpec((None, HEADS, dil, tm // dil, HEAD_DIM), lambda b, t: (b, 0, 0, t, 0)))
    for win, _ in DIL_GROUPS:
        keep = min(win, seq)
        rows = min(keep, tm)
        assert keep % rows == 0
        first = (seq - keep) // rows
        first_kept.append((seq - keep) // tm)
        out_shape.append(jax.ShapeDtypeStruct((batch, keep * 2 * HEADS, HEAD_DIM), F32))
        out_specs.append(pl.BlockSpec(
            (None, rows * 2 * HEADS, HEAD_DIM),
            functools.partial(lambda b, t, first, per: (b, jnp.maximum((t + 1) * per - 1 - first, 0), 0),
                              first=first, per=tm // rows)))
    steps = batch * tps
    cast_rows = [w.shape[0] // steps for w in cast_ws]
    assert all(w.shape[0] % steps == 0 and r % 16 == 0 for w, r in zip(cast_ws, cast_rows))
    cast_specs = [pl.BlockSpec((r, w.shape[1]), lambda b, t: (b * tps + t, 0)) for w, r in zip(cast_ws, cast_rows)]
    return pl.pallas_call(
        functools.partial(_in_proj_att_kernel, first_kept=tuple(first_kept), n_cast=len(cast_ws)),
        grid=(batch, tps),
        in_specs=[
            pl.BlockSpec((tm, D_MODEL), lambda b, t: (b * tps + t, 0)),
            _resident((1, D_MODEL), lambda b, t: (0, 0)),
            _resident(w_att.shape, lambda b, t: (0, 0)),
        ] + cast_specs,
        out_specs=out_specs + cast_specs,
        out_shape=out_shape + [jax.ShapeDtypeStruct(w.shape, BF16) for w in cast_ws],
        scratch_shapes=[pltpu.VMEM((HEADS, tm, HEAD_DIM), F32)] * (2 + 2 * N_GROUPS),
        compiler_params=_cparams(2, 48),
        name="in_proj_att",
    )(x2d, g, w_att, *cast_ws)


def _in_proj_rest_kernel(x_ref, g_ref, w_ref, cos_ref, sin_ref, z_ref, c0, c1, c2,
                         rq_ref, rk_ref, rv_ref, gsw_ref, ga_ref, gr_ref, att_s_ref):
    h = _norm_rows(x_ref, g_ref)
    rows_left = list(range(z_ref.shape[0]))

    def side_row():
        if rows_left:
            r = rows_left.pop(0)
            att_s_ref[r] = _sample_attend_row(z_ref, (c0, c1, c2), r)

    def tile(j):
        lo = (j - N_ATT_TILES) * TN
        return _dot(h, w_ref[:, lo:lo + TN])

    def ret_qk(col, dst, scale):
        acc = tile(col)
        cos, sin = cos_ref[...], sin_ref[...]
        for hh in range(RET_HEADS):
            r = _rope(_head(acc, hh), cos, sin)
            dst[hh] = r if scale is None else r * scale

    def ret_wide(col, dst, e, act=None):
        acc = tile(col + e)
        for s in range(2):
            part = _head(acc, s, RET_DV)
            dst[2 * e + s] = (part if act is None else act(part)).astype(dst.dtype)

    def gate(col, dst, e):
        dst[:, e * TN:(e + 1) * TN] = tile(col + e)

    ret_qk(COL_RQ, rq_ref, None)
    side_row()
    ret_qk(COL_RK, rk_ref, RET_DK ** -0.5)
    side_row()
    for e in range(2):
        ret_wide(COL_RV, rv_ref, e)
        ret_wide(COL_GSW, gsw_ref, e, _silu)
        side_row()
        gate(COL_GA, ga_ref, e)
        gate(COL_GR, gr_ref, e)
    while rows_left:
        side_row()


def _in_proj_rest(x2d, g, w_rest, cos_t, sin_t, z, caches, layer, batch, seq, tm):
    assert seq % tm == 0
    tps = seq // tm
    steps = batch * tps
    db = z.shape[0]
    assert db % steps == 0
    side_rows = db // steps
    views, cache_specs = [], []
    for (win, dil), c in zip(DIL_GROUPS, caches):
        assert c.shape[1] == db and c.shape[2] == win and win // dil == BAND
        views.append(c.reshape(c.shape[0] * db, BAND, dil, 2 * HEADS, HEAD_DIM))
        cache_specs.append(pl.BlockSpec((side_rows, BAND, None, 2 * HEADS, HEAD_DIM),
                                        lambda b, t: (layer * steps + b * tps + t, 0, 0, 0, 0)))
    hm = lambda width, dt: (jax.ShapeDtypeStruct((batch, RET_HEADS, seq, width), dt),
                            pl.BlockSpec((None, RET_HEADS, tm, width), lambda b, t: (b, 0, t, 0)))
    nat = (jax.ShapeDtypeStruct((batch * seq, D_MODEL), F32),
           pl.BlockSpec((tm, D_MODEL), lambda b, t: (b * tps + t, 0)))
    outs = (hm(RET_DK, F32), hm(RET_DK, F32), hm(RET_DV, BF16), hm(RET_DV, F32), nat, nat)
    side_outs = ((jax.ShapeDtypeStruct((db, HEADS, HEAD_DIM), F32),
                  pl.BlockSpec((side_rows, HEADS, HEAD_DIM), lambda b, t: (b * tps + t, 0, 0))),)
    return pl.pallas_call(
        _in_proj_rest_kernel,
        grid=(batch, tps),
        in_specs=[
            pl.BlockSpec((tm, D_MODEL), lambda b, t: (b * tps + t, 0)),
            _resident((1, D_MODEL), lambda b, t: (0, 0)),
            _resident(w_rest.shape, lambda b, t: (0, 0)),
            pl.BlockSpec((tm, RET_DK), lambda b, t: (t, 0)),
            pl.BlockSpec((tm, RET_DK), lambda b, t: (t, 0)),
            pl.BlockSpec((side_rows, 1, IN_COLS), lambda b, t: (b * tps + t, 0, 0)),
        ] + cache_specs,
        out_specs=[spec for _, spec in outs + side_outs],
        out_shape=[shp for shp, _ in outs + side_outs],
        compiler_params=_cparams(2, 56),
        name="in_proj_rest",
    )(x2d, g, w_rest, cos_t, sin_t, z.reshape(db, 1, IN_COLS), *views)


def _in_proj_sample_kernel(x_ref, g_ref, w_ref, z_ref, w_att_ref, w_rest_ref, h_ref):
    j = pl.program_id(0)

    @pl.when(j == 0)
    def _():
        h_ref[...] = _norm_rows(x_ref, g_ref)

    w_bf = w_ref[...].astype(BF16)

    @pl.when(j < N_ATT_TILES)
    def _():
        w_att_ref[...] = w_bf

    @pl.when(j >= N_ATT_TILES)
    def _():
        w_rest_ref[...] = w_bf

    z_ref[...] = _dot(h_ref[...], w_bf)


def _in_proj_sample(x2d, g, w):
    m = x2d.shape[0]
    col_tile = lambda rows, f: pl.BlockSpec((rows, TN), lambda j: (0, f(j)))
    n_rest = N_COL_TILES - N_ATT_TILES
    return pl.pallas_call(
        _in_proj_sample_kernel,
        grid=(N_COL_TILES,),
        in_specs=[
            _resident((m, D_MODEL), lambda j: (0, 0)),
            _resident((1, D_MODEL), lambda j: (0, 0)),
            col_tile(D_MODEL, lambda j: j),
        ],
        out_specs=[col_tile(m, lambda j: j),
                   col_tile(D_MODEL, lambda j: jnp.minimum(j, N_ATT_TILES - 1)),
                   col_tile(D_MODEL, lambda j: jnp.maximum(j - N_ATT_TILES, 0))],
        out_shape=[jax.ShapeDtypeStruct((m, IN_COLS), F32),
                   jax.ShapeDtypeStruct((D_MODEL, N_ATT_TILES * TN), BF16),
                   jax.ShapeDtypeStruct((D_MODEL, n_rest * TN), BF16)],
        scratch_shapes=[pltpu.VMEM((m, D_MODEL), BF16)],
        compiler_params=_cparams(1, 24),
        name="in_proj_sample",
    )(x2d, g, w)


def _softmax_block(s, v):
    m = jnp.max(s, axis=1, keepdims=True)
    p = jnp.exp(s - m).astype(BF16)
    uv = _dot(p, jnp.concatenate([v, jnp.ones_like(v)], axis=1))
    return uv[:, :HEAD_DIM], jnp.broadcast_to(m, (s.shape[0], HEAD_DIM)), uv[:, HEAD_DIM:]


ATT_BLOCKS_PER_STEP = 16
ATT_ORDER = MAX_ROW_STRIDE


def _attn_prompt_kernel(q0, q1, q2, k0, k1, k2, v0, v1, v2, o_ref,
                        u0, u1, u2, m0, m1, m2, l0, l1, l2, *, seq):
    qs, ks, vs = (q0, q1, q2), (k0, k1, k2), (v0, v1, v2)
    us, ms, ls = (u0, u1, u2), (m0, m1, m2), (l0, l1, l2)
    order = ATT_ORDER
    n_out = seq // order
    row1 = lax.broadcasted_iota(jnp.int32, (BAND, BAND), 0)
    col1 = lax.broadcasted_iota(jnp.int32, (BAND, BAND), 1)
    causal = col1 <= row1
    row2 = lax.broadcasted_iota(jnp.int32, (BAND, 2 * BAND), 0)
    col2 = lax.broadcasted_iota(jnp.int32, (BAND, 2 * BAND), 1)
    band = (col2 >= row2) & (col2 <= row2 + BAND)

    aligned = lambda r: r if isinstance(r, int) else pl.multiple_of(r, BAND)

    for g, (_, dil) in enumerate(DIL_GROUPS):
        n = seq // dil
        nb = n // BAND

        def run_blocks(blocks, g=g, n=n, dil=dil):
            scores, values = [], []
            for c, blk, has_prev in blocks:
                own = pl.ds(aligned(blk * BAND), BAND)
                keys = pl.ds(aligned((blk - 1) * BAND), 2 * BAND) if has_prev else own
                mask = band if has_prev else causal
                scores.append(jnp.where(mask, _dot_nt(qs[g][c, own, :], ks[g][c, keys, :]), NEG_INF))
                values.append(vs[g][c, keys, :])
            results = [_softmax_block(s, v) for s, v in zip(scores, values)]
            for (c, blk, _), (u, m, l) in zip(blocks, results):
                if dil <= order:
                    rs = pl.ds(aligned(c * n + blk * BAND), BAND)
                else:
                    step = dil // order
                    rs = pl.ds((c % order) * n_out + c // order + blk * (BAND * step), BAND, stride=step)
                us[g][rs, :] = u
                ms[g][rs, :] = m
                ls[g][rs, :] = l

        per = ATT_BLOCKS_PER_STEP
        assert (dil * nb) % per == 0 and (per % nb == 0 or nb % per == 0)
        if nb >= per:
            def body(i, carry, run_blocks=run_blocks, steps=nb // per):
                c, i0 = i // steps, (i % steps) * per
                run_blocks([(c, i0 + e, True) for e in range(per)])
                return carry
            for c in range(dil):
                run_blocks([(c, e, e > 0) for e in range(per)])
                lax.fori_loop(c * (nb // per) + 1, (c + 1) * (nb // per), body, 0)
        else:
            def body(i, carry, run_blocks=run_blocks, nb=nb, cps=per // nb):
                run_blocks([(i * cps + e // nb, e % nb, e % nb > 0) for e in range(per)])
                return carry
            lax.fori_loop(0, dil * nb // per, body, 0)

    chunks = n_out // BAND

    def combine_one(j):
        c, t = j // chunks, j % chunks
        rows = []
        for _, dil in DIL_GROUPS:
            r = min(dil, order)
            step = order // r
            start = (c % r) * (seq // r) + c // r + t * (BAND * step)
            rows.append(pl.ds(aligned(start), BAND) if step == 1 else pl.ds(start, BAND, stride=step))
        m_g = [ms[g][rows[g], :] for g in range(N_GROUPS)]
        m = jnp.maximum(jnp.maximum(m_g[0], m_g[1]), m_g[2])
        num = jnp.zeros((BAND, HEAD_DIM), F32)
        den = jnp.zeros((BAND, HEAD_DIM), F32)
        for g in range(N_GROUPS):
            w = jnp.exp(m_g[g] - m)
            num = num + w * us[g][rows[g], :]
            den = den + w * ls[g][rows[g], :]
        return num / den

    per = 4
    assert (order * chunks) % per == 0

    def combine(i, carry):
        outs = [combine_one(i * per + e) for e in range(per)]
        for e, o in enumerate(outs):
            o_ref[pl.ds(aligned((i * per + e) * BAND), BAND), :] = o
        return carry

    lax.fori_loop(0, order * chunks // per, combine, 0)


def _attn_prompt(qkv, batch, seq):
    assert seq % (BAND * DIL_GROUPS[-1][1]) == 0
    assert all(ATT_ORDER % d == 0 or d % ATT_ORDER == 0 for _, d in DIL_GROUPS)
    in_specs = [pl.BlockSpec((None, None) + a.shape[2:], lambda b, h: (b, h, 0, 0, 0)) for a in qkv]
    out = pl.pallas_call(
        functools.partial(_attn_prompt_kernel, seq=seq),
        grid=(batch, HEADS),
        in_specs=in_specs,
        out_specs=pl.BlockSpec((None, None, seq, HEAD_DIM), lambda b, h: (b, h, 0, 0)),
        out_shape=jax.ShapeDtypeStruct((batch, HEADS, seq, HEAD_DIM), F32),
        scratch_shapes=[pltpu.VMEM((seq, HEAD_DIM), F32)] * 9,
        compiler_params=_cparams(2, 32),
        name="attend_prompt",
    )(*qkv)
    return out.reshape(batch, HEADS, ATT_ORDER, seq // ATT_ORDER, HEAD_DIM)


def _group_norm_gate(o, gn, gate):
    mu = jnp.mean(o, axis=-1, keepdims=True)
    d = o - mu
    var = jnp.mean(d * d, axis=-1, keepdims=True)
    return gate * (d * lax.rsqrt(var + EPS) * gn)


RET_CHUNKS_PER_STEP = 8


def _ret_prompt_kernel(lg_ref, q_ref, k_ref, v_ref, gate_ref, gn_ref, o_ref, s_out_ref, s_scr):
    c = RET_CHUNK
    rows = q_ref.shape[1]
    t = lax.broadcasted_iota(jnp.int32, (c, 1), 0).astype(F32)
    rel = (lax.broadcasted_iota(jnp.int32, (c, c), 0) - lax.broadcasted_iota(jnp.int32, (c, c), 1)).astype(F32)

    @pl.when(pl.program_id(1) == 0)
    def _():
        s_scr[...] = jnp.zeros_like(s_scr)

    consts = []
    for h in range(RET_HEADS):
        lg = lg_ref[h]
        consts.append((
            jnp.where(rel >= 0, jnp.exp(lg * jnp.maximum(rel, 0.0)), 0.0),
            jnp.exp(lg * (t + 1.0)),
            jnp.exp(lg * (c - 1.0 - t)),
            jnp.exp(lg * jnp.full((1, RET_DV), float(c), F32)),
        ))

    per = RET_CHUNKS_PER_STEP
    assert (rows // c) % per == 0

    def step(i, carry):
        rss = [pl.ds(pl.multiple_of((i * per + e) * c, c), c) for e in range(per)]
        local = {}
        for h in range(RET_HEADS):
            decay, inner, tail, _ = consts[h]
            for e, rs in enumerate(rss):
                q, k, vb = q_ref[h, rs, :], k_ref[h, rs, :], v_ref[h, rs, :]
                a = _dot_nt(q.astype(BF16), k.astype(BF16)) * decay
                local[h, e] = (_dot(a.astype(BF16), vb),
                               (q * inner).astype(BF16),
                               _dot((k * tail).T.astype(BF16), vb))
        outs = {}
        for h in range(RET_HEADS):
            s = s_scr[h]
            for e in range(per):
                intra, q_in, kv = local[h, e]
                outs[h, e] = intra + _dot(q_in, s.astype(BF16))
                s = consts[h][3] * s + kv
            s_scr[h] = s
        for h in range(RET_HEADS):
            gn = gn_ref[:, h * RET_DV:(h + 1) * RET_DV]
            for e, rs in enumerate(rss):
                o_ref[h, rs, :] = _group_norm_gate(outs[h, e], gn, gate_ref[h, rs, :]).astype(o_ref.dtype)
        return carry

    lax.fori_loop(0, rows // (c * per), step, 0)

    @pl.when(pl.program_id(1) == pl.num_programs(1) - 1)
    def _():
        s_out_ref[...] = s_scr[...]


def _ret_prompt(log_g, rq, rk, rv, gsw, gn, batch, seq, rows):
    assert seq % rows == 0 and rows % RET_CHUNK == 0
    hm = lambda width: pl.BlockSpec((None, RET_HEADS, rows, width), lambda b, t: (b, 0, t, 0))
    return pl.pallas_call(
        _ret_prompt_kernel,
        grid=(batch, seq // rows),
        in_specs=[
            pl.BlockSpec(memory_space=pltpu.SMEM),
            hm(RET_DK), hm(RET_DK), hm(RET_DV), hm(RET_DV),
            _resident((1, RET_V), lambda b, t: (0, 0)),
        ],
        out_specs=[
            hm(RET_DV),
            pl.BlockSpec((None, RET_HEADS, RET_DK, RET_DV), lambda b, t: (b, 0, 0, 0)),
        ],
        out_shape=[
            jax.ShapeDtypeStruct((batch, RET_HEADS, seq, RET_DV), BF16),
            jax.ShapeDtypeStruct((batch, RET_HEADS, RET_DK, RET_DV), F32),
        ],
        scratch_shapes=[pltpu.VMEM((RET_HEADS, RET_DK, RET_DV), F32)],
        compiler_params=_cparams(2, 40),
        name="retain_prompt",
    )(log_g, rq, rk, rv, gsw, gn)


def _sample_attend_row(z_ref, caches, r):
    half = BAND // 2

    def heads(tile0, g):
        base = (tile0 + g) * TN
        return jnp.concatenate(
            [z_ref[r, :, base + hh * HEAD_DIM:base + (hh + 1) * HEAD_DIM] for hh in range(HEADS)], axis=0)

    def paired(ref, lo):
        return jnp.concatenate([ref[r, 0:half, lo:lo + HEADS, :], ref[r, half:BAND, lo:lo + HEADS, :]], axis=1)

    twice = lambda a: jnp.concatenate([a, a], axis=0)
    fold = lambda a, op: op(a[0:HEADS], a[HEADS:2 * HEADS])

    us, ms, ls = [], [], []
    for g in range(N_GROUPS):
        q = heads(COL_Q, g) * (HEAD_DIM ** -0.5)
        k_new, v_new = heads(COL_K, g), heads(COL_V, g)
        k_buf, v_buf = paired(caches[g], 0), paired(caches[g], HEADS)
        s_buf = jnp.sum(k_buf * twice(q)[None], axis=-1, keepdims=True)
        s_new = jnp.sum(k_new * q, axis=-1, keepdims=True)
        m = jnp.maximum(fold(jnp.max(s_buf, axis=0), jnp.maximum), s_new)
        p_buf = jnp.exp(s_buf - twice(m)[None])
        p_new = jnp.exp(s_new - m)
        ls.append(fold(jnp.sum(p_buf, axis=0), jnp.add) + p_new)
        us.append(fold(jnp.sum(p_buf * v_buf, axis=0), jnp.add) + p_new * v_new)
        ms.append(m)
    m = jnp.maximum(jnp.maximum(ms[0], ms[1]), ms[2])
    num = jnp.zeros((HEADS, HEAD_DIM), F32)
    den = jnp.zeros((HEADS, 1), F32)
    for g in range(N_GROUPS):
        w = jnp.exp(ms[g] - m)
        num = num + w * us[g]
        den = den + w * ls[g]
    return num / den


def _sample_retain_head(lg_ref, z_ref, cos, sin, gn_ref, s_ref, ret_ref, s_out_ref, h):
    rows = z_ref.shape[0]
    eye = (lax.broadcasted_iota(jnp.int32, (RET_DK, RET_DK), 0)
           == lax.broadcasted_iota(jnp.int32, (RET_DK, RET_DK), 1))

    def column(v):
        return jnp.sum(jnp.where(eye, jnp.broadcast_to(v, (RET_DK, RET_DK)), 0.0), axis=1, keepdims=True)

    def zcols(tile0, width):
        return z_ref[:, 0, tile0 * TN + h * width:tile0 * TN + (h + 1) * width]

    gamma = jnp.exp(lg_ref[h] * jnp.ones((1, RET_DV), F32))
    q_h = _rope(zcols(COL_RQ, RET_DK), cos, sin)
    k_h = _rope(zcols(COL_RK, RET_DK), cos, sin) * (RET_DK ** -0.5)
    v_h = zcols(COL_RV, RET_DV)
    outs = []
    for r in range(rows):
        s_new = gamma * s_ref[r, h] + column(k_h[r:r + 1, :]) * v_h[r:r + 1, :]
        s_out_ref[r, h] = s_new
        outs.append(jnp.sum(column(q_h[r:r + 1, :]) * s_new, axis=0, keepdims=True))
    seg = slice(h * RET_DV, (h + 1) * RET_DV)
    ret_ref[:, 0, seg] = _group_norm_gate(jnp.concatenate(outs, axis=0), gn_ref[:, seg], _silu(zcols(COL_GSW, RET_DV)))


FF_CHUNK = 1024


def _tail_kernel(*refs, with_side):
    (att_ref, ret_ref, ga_ref, gr_ref, x_ref, wa_ref, wr_ref, wo_ref, gpost_mix_ref, gpre_mlp_ref,
     wu_ref, wd_ref, gpost_mlp_ref) = refs[:13]
    att_scr = refs[-1]
    side = []
    if with_side:
        lg_ref, z_ref, cos_ref, sin_ref, gn_ref, s_ref, y_ref, ret_s_ref, s_out_ref = refs[13:-1]
        side = [functools.partial(_sample_retain_head, lg_ref, z_ref, cos_ref[...], sin_ref[...], gn_ref, s_ref,
                                  ret_s_ref, s_out_ref, hh) for hh in range(RET_HEADS)]
    else:
        y_ref, = refs[13:-1]
    r = att_ref.shape[1]
    for hh in range(HEADS):
        for c in range(r):
            rs = pl.ds(c, att_ref.shape[2], stride=r) if r > 1 else slice(None)
            att_scr[hh, rs, :] = att_ref[hh, c]
    att = jnp.concatenate([att_scr[hh].astype(BF16) for hh in range(HEADS)], axis=1)
    ret = jnp.concatenate([ret_ref[hh].astype(BF16) for hh in range(RET_HEADS)], axis=1)
    m = _sigmoid(ga_ref[...]) * _dot(att, wa_ref[...]) + _sigmoid(gr_ref[...]) * _dot(ret, wr_ref[...])
    x1 = x_ref[...] + _rms(_dot(m.astype(BF16), wo_ref[...])) * gpost_mix_ref[...]
    h = (_rms(x1) * gpre_mlp_ref[...]).astype(BF16)

    n_chunks = D_FF // FF_CHUNK
    acc = jnp.zeros(x_ref.shape, F32)
    for c in range(n_chunks):
        cs = slice(c * FF_CHUNK, (c + 1) * FF_CHUNK)
        u = jnp.maximum(_dot(h, wu_ref[:, cs]), 0.0)
        acc = acc + _dot((u * u).astype(BF16), wd_ref[cs, :])
        for work in side[c::n_chunks]:
            work()
    y_ref[...] = x1 + _rms(acc) * gpost_mlp_ref[...]


def _tail(att, ret, ga, gr, x2d, lw, tm, side=None):
    batch, _, r, ni, _ = att.shape
    seq = r * ni
    m = batch * seq
    assert seq % tm == 0 and tm % r == 0 and x2d.shape[0] == m
    tps = seq // tm
    steps = batch * tps
    rows = pl.BlockSpec((tm, D_MODEL), lambda i: (i, 0))
    full = lambda a: _resident(a.shape, lambda i: (0,) * a.ndim)
    weights = [lw[k] for k in ("w_att_br", "w_ret_br", "w_out", "g_post_mix", "g_pre_mlp",
                               "w_up", "w_down", "g_post_mlp")]
    args = [att, ret, ga, gr, x2d] + weights
    in_specs = [pl.BlockSpec((None, HEADS, r, tm // r, HEAD_DIM), lambda i: (i // tps, 0, 0, i % tps, 0)),
                pl.BlockSpec((None, RET_HEADS, tm, RET_DV), lambda i: (i // tps, 0, i % tps, 0)),
                rows, rows, rows] + [full(w) for w in weights]
    out_shape = [jax.ShapeDtypeStruct((m, D_MODEL), F32)]
    out_specs = [rows]
    if side is not None:
        log_g, z, cos_s, sin_s, gn, state, layer = side
        db = z.shape[0]
        assert db % steps == 0
        side_rows = db // steps
        st_block = (side_rows, RET_HEADS, RET_DK, RET_DV)
        args += [log_g, z.reshape(db, 1, IN_COLS), cos_s, sin_s, gn,
                 state.reshape((state.shape[0] * db,) + state.shape[2:])]
        in_specs += [pl.BlockSpec(memory_space=pltpu.SMEM),
                     pl.BlockSpec((side_rows, 1, IN_COLS), lambda i: (i, 0, 0)),
                     full(cos_s), full(sin_s), full(gn),
                     pl.BlockSpec(st_block, lambda i: (layer * steps + i, 0, 0, 0))]
        out_shape += [jax.ShapeDtypeStruct((db, 1, RET_V), F32), jax.ShapeDtypeStruct((db,) + st_block[1:], F32)]
        out_specs += [pl.BlockSpec((side_rows, 1, RET_V), lambda i: (i, 0, 0)),
                      pl.BlockSpec(st_block, lambda i: (i, 0, 0, 0))]
    outs = pl.pallas_call(
        functools.partial(_tail_kernel, with_side=side is not None),
        grid=(steps,),
        in_specs=in_specs,
        out_specs=out_specs,
        out_shape=out_shape,
        scratch_shapes=[pltpu.VMEM((HEADS, tm, HEAD_DIM), F32)],
        compiler_params=_cparams(1, 56),
        name="tail",
    )(*args)
    if side is None:
        return outs[0]
    return outs[0], outs[1].reshape(db, RET_V), outs[2]


def _rope_tables(pos):
    half = RET_DK // 2
    inv = ROPE_BASE ** (-jnp.arange(half, dtype=F32) / half)
    ang = pos.astype(F32)[:, None] * inv[None, :]
    cos, sin = jnp.cos(ang), jnp.sin(ang)
    return jnp.concatenate([cos, cos], axis=-1), jnp.concatenate([-sin, sin], axis=-1)


IN_PROJ_TM = 512
TAIL_TM = 512
TAIL_WEIGHTS = ("w_att_br", "w_ret_br", "w_out", "w_up", "w_down")
RET_ROWS = 1024


def _layer(xp, xs, caches, state, layer, lw, log_g):
    batch, seq, _ = xp.shape
    db, t, _ = xs.shape
    assert t == 1
    xp2d, xs2d = xp.reshape(batch * seq, D_MODEL), xs.reshape(db, D_MODEL)

    z, w_att, w_rest = _in_proj_sample(xs2d, lw["g_pre_mix"], lw["w_in"])

    cos_p, sin_p = _rope_tables(jnp.arange(seq, dtype=jnp.int32))
    cos_s, sin_s = _rope_tables(PAST_LEN + jnp.arange(1, dtype=jnp.int32))
    outs = _in_proj_att(xp2d, lw["g_pre_mix"], w_att, [lw[k] for k in TAIL_WEIGHTS], batch, seq, IN_PROJ_TM)
    qkv = outs[:9]
    rows_p = [kv.reshape(batch, -1, 2, HEADS, HEAD_DIM) for kv in outs[9:12]]
    lw = dict(lw, **dict(zip(TAIL_WEIGHTS, outs[12:])))
    rq, rk, rv, gsw, ga, gr, att_s = _in_proj_rest(
        xp2d, lw["g_pre_mix"], w_rest, cos_p, sin_p, z, caches, layer, batch, seq, IN_PROJ_TM)
    att = _attn_prompt(qkv, batch, seq)
    ret, state_p = _ret_prompt(log_g, rq, rk, rv, gsw, lw["g_ret_norm"], batch, seq, RET_ROWS)
    yp, ret_s, state_s = _tail(att, ret, ga, gr, xp2d, lw, TAIL_TM,
                               side=(log_g, z, cos_s, sin_s, lw["g_ret_norm"], state, layer))

    att_s = jnp.transpose(att_s, (1, 0, 2)).reshape(1, HEADS, 1, db, HEAD_DIM)
    ret_s = jnp.transpose(ret_s.reshape(db, RET_HEADS, RET_DV), (1, 0, 2))[None]
    ga_s = z[:, COL_GA * TN:COL_GA * TN + D_MODEL]
    gr_s = z[:, COL_GR * TN:COL_GR * TN + D_MODEL]
    ys = _tail(att_s, ret_s, ga_s, gr_s, xs2d, lw, db)
    rows_s = []
    for g in range(N_GROUPS):
        k_new = z[:, (COL_K + g) * TN:(COL_K + g + 1) * TN]
        v_new = z[:, (COL_V + g) * TN:(COL_V + g + 1) * TN]
        rows_s.append(jnp.stack([k_new, v_new], axis=1).reshape(db, 1, 2, HEADS, HEAD_DIM))
    return (yp.reshape(batch, seq, D_MODEL), rows_p, state_p), (ys.reshape(db, 1, D_MODEL), rows_s, state_s)


def _stack(xs):
    return xs[0][None] if len(xs) == 1 else jnp.stack(xs, axis=0)


def kernel(x_prompt, x_sample, cache_kv_d1, cache_kv_d4, cache_kv_d16, state_ret, w_in, w_att_br, w_ret_br, w_out, w_up, w_down, g_ret_norm, g_pre_mix, g_post_mix, g_pre_mlp, g_post_mlp):
    depth = w_in.shape[0]
    log_g = jnp.log1p(-jnp.power(2.0, -5.0 - jnp.arange(RET_HEADS, dtype=F32)))
    caches = (cache_kv_d1, cache_kv_d4, cache_kv_d16)
    xp, xs = x_prompt, x_sample
    p_rows, s_rows = [[], [], []], [[], [], []]
    p_states, s_states = [], []
    for l in range(depth):
        lw = {
            "w_in": w_in[l], "w_up": w_up[l], "w_down": w_down[l],
            "w_att_br": w_att_br[l], "w_ret_br": w_ret_br[l], "w_out": w_out[l],
            "g_ret_norm": g_ret_norm[l].reshape(1, RET_V), "g_pre_mix": g_pre_mix[l].reshape(1, D_MODEL),
            "g_post_mix": g_post_mix[l].reshape(1, D_MODEL), "g_pre_mlp": g_pre_mlp[l].reshape(1, D_MODEL),
            "g_post_mlp": g_post_mlp[l].reshape(1, D_MODEL),
        }
        (xp, rows_p, sp), (xs, rows_s, ss) = _layer(xp, xs, caches, state_ret, l, lw, log_g)
        for g in range(N_GROUPS):
            p_rows[g].append(rows_p[g])
            s_rows[g].append(rows_s[g])
        p_states.append(sp)
        s_states.append(ss)
    return (xp, xs, _stack(p_rows[0]), _stack(p_rows[1]), _stack(p_rows[2]), _stack(p_states),
            _stack(s_rows[0]), _stack(s_rows[1]), _stack(s_rows[2]), _stack(s_states))
```

```python
import functools

import jax
import jax.numpy as jnp
from jax import lax
from jax.experimental import pallas as pl
from jax.experimental.pallas import tpu as pltpu

F32 = jnp.float32
BF16 = jnp.bfloat16

D_MODEL = 1024
PAST_LEN = 8192

DIL_GROUPS = ((128, 1), (512, 4), (2048, 16))
N_GROUPS = 3
HEADS = 4
HEAD_DIM = 128
ATT_OUT = HEADS * HEAD_DIM
ATT_COLS = N_GROUPS * ATT_OUT
BAND = 128

RET_HEADS = 4
RET_DK = 128
RET_DV = 256
RET_QK = RET_HEADS * RET_DK
RET_V = RET_HEADS * RET_DV
RET_CHUNK = 128
ROPE_BASE = 10000.0

D_FF = 4 * D_MODEL
IN_COLS = 3 * ATT_COLS + 2 * RET_QK + 2 * RET_V + 2 * D_MODEL
EPS = 1e-6
NEG_INF = -1e30

TN = 512
N_COL_TILES = IN_COLS // TN
COL_Q, COL_K, COL_V, COL_RQ, COL_RK, COL_RV, COL_GSW, COL_GA, COL_GR = 0, 3, 6, 9, 10, 11, 13, 15, 17

V7X_VMEM_BYTES = 64 * 1024 * 1024
MIB = 1024 * 1024
CALL_VMEM_BYTES = 56 * MIB
assert CALL_VMEM_BYTES < V7X_VMEM_BYTES


def _cparams(n_axes):
    return pltpu.CompilerParams(
        dimension_semantics=("arbitrary",) * n_axes,
        vmem_limit_bytes=CALL_VMEM_BYTES,
    )


def _resident(shape, index_map):
    return pl.BlockSpec(shape, index_map, pipeline_mode=pl.Buffered(1))


def _rms(x):
    return x * lax.rsqrt(jnp.mean(x * x, axis=-1, keepdims=True) + EPS)


def _sigmoid(x):
    return 1.0 / (1.0 + jnp.exp(-x))


def _silu(x):
    return x * _sigmoid(x)


def _dot(a, b):
    return jnp.dot(a, b, preferred_element_type=F32)


def _dot_nt(a, b):
    return lax.dot_general(a, b, (((1,), (1,)), ((), ())), preferred_element_type=F32)


def _rope(x, cos, sin):
    return x * cos + pltpu.roll(x, RET_DK // 2, 1) * sin


N_ATT_TILES = 3 * N_GROUPS
MAX_ROW_STRIDE = 4


def _norm_rows(x_ref, g_ref):
    return (_rms(x_ref[...]) * g_ref[...]).astype(BF16)


def _head(acc, hh, width=HEAD_DIM):
    return acc[:, hh * width:(hh + 1) * width]


def _in_proj_att_kernel(*refs, first_kept, n_cast):
    x_ref, g_ref, w_ref = refs[:3]
    cast_in, outs = refs[3:3 + n_cast], refs[3 + n_cast:]
    q0, q1, q2, k0, k1, k2, v0, v1, v2, kvo0, kvo1, kvo2 = outs[:12]
    cast_out = outs[12:12 + n_cast]
    scr_q, sk0, sk1, sk2, sv0, sv1, sv2, scr_mid = outs[12 + n_cast:]
    tm = x_ref.shape[0]
    h = _norm_rows(x_ref, g_ref)
    for src, dst in zip(cast_in, cast_out):
        dst[...] = src[...].astype(BF16)

    def park(j, scr, scale=None):
        acc = _dot(h, w_ref[:, j * TN:(j + 1) * TN])
        for hh in range(HEADS):
            scr[hh] = _head(acc, hh) if scale is None else _head(acc, hh) * scale

    def residues(scr, dst, dil):
        s = MAX_ROW_STRIDE
        if dil <= s:
            for hh in range(HEADS):
                for c in range(dil):
                    rows = pl.ds(c, tm // dil, stride=dil) if dil > 1 else slice(None)
                    dst[hh, c] = scr[hh, rows, :].astype(BF16)
            return
        assert dil % s == 0 and dil // s <= s
        for hh in range(HEADS):
            for c in range(s):
                scr_mid[hh, c * (tm // s):(c + 1) * (tm // s), :] = scr[hh, pl.ds(c, tm // s, stride=s), :]
        for hh in range(HEADS):
            for c in range(dil):
                start = (c % s) * (tm // s) + c // s
                dst[hh, c] = scr_mid[hh, pl.ds(start, tm // dil, stride=dil // s), :].astype(BF16)

    def kv_rows(g):
        kvo = (kvo0, kvo1, kvo2)[g]
        keep = kvo.shape[0] // (2 * HEADS)
        for which, scr in enumerate(((sk0, sk1, sk2)[g], (sv0, sv1, sv2)[g])):
            for hh in range(HEADS):
                kvo[pl.ds(which * HEADS + hh, keep, stride=2 * HEADS), :] = scr[hh, tm - keep:, :]

    for g in sorted(range(N_GROUPS), key=lambda g_: first_kept[g_]):
        dil = DIL_GROUPS[g][1]
        park(COL_K + g, (sk0, sk1, sk2)[g])
        residues((sk0, sk1, sk2)[g], (k0, k1, k2)[g], dil)
        park(COL_V + g, (sv0, sv1, sv2)[g])
        residues((sv0, sv1, sv2)[g], (v0, v1, v2)[g], dil)
        if first_kept[g] == 0:
            kv_rows(g)
        park(COL_Q + g, scr_q, HEAD_DIM ** -0.5)
        residues(scr_q, (q0, q1, q2)[g], dil)

    for g in range(N_GROUPS):
        if first_kept[g] > 0:
            pl.when(pl.program_id(1) >= first_kept[g])(functools.partial(kv_rows, g))


def _in_proj_att(x2d, g, w_att, cast_ws, batch, seq, tm):
    assert seq % tm == 0 and tm % (16 * DIL_GROUPS[-1][1]) == 0
    tps = seq // tm
    out_shape, out_specs, first_kept = [], [], []
    for _ in range(3):
        for _, dil in DIL_GROUPS:
            out_shape.append(jax.ShapeDtypeStruct((batch, HEADS, dil, seq // dil, HEAD_DIM), BF16))
            out_specs.append(pl.BlockSpec((None, HEADS, dil, tm // dil, HEAD_DIM), lambda b, t: (b, 0, 0, t, 0)))
    for win, _ in DIL_GROUPS:
        keep = min(win, seq)
        rows = min(keep, tm)
        assert keep % rows == 0
        first = (seq - keep) // rows
        first_kept.append((seq - keep) // tm)
        out_shape.append(jax.ShapeDtypeStruct((batch, keep * 2 * HEADS, HEAD_DIM), F32))
        out_specs.append(pl.BlockSpec(
            (None, rows * 2 * HEADS, HEAD_DIM),
            functools.partial(lambda b, t, first, per: (b, jnp.maximum((t + 1) * per - 1 - first, 0), 0),
                              first=first, per=tm // rows)))
    steps = batch * tps
    cast_rows = [w.shape[0] // steps for w in cast_ws]
    assert all(w.shape[0] % steps == 0 and r % 16 == 0 for w, r in zip(cast_ws, cast_rows))
    cast_specs = [pl.BlockSpec((r, w.shape[1]), lambda b, t: (b * tps + t, 0)) for w, r in zip(cast_ws, cast_rows)]
    return pl.pallas_call(
        functools.partial(_in_proj_att_kernel, first_kept=tuple(first_kept), n_cast=len(cast_ws)),
        grid=(batch, tps),
        in_specs=[
            pl.BlockSpec((tm, D_MODEL), lambda b, t: (b * tps + t, 0)),
            _resident((1, D_MODEL), lambda b, t: (0, 0)),
            _resident(w_att.shape, lambda b, t: (0, 0)),
        ] + cast_specs,
        out_specs=out_specs + cast_specs,
        out_shape=out_shape + [jax.ShapeDtypeStruct(w.shape, BF16) for w in cast_ws],
        scratch_shapes=[pltpu.VMEM((HEADS, tm, HEAD_DIM), F32)] * (2 + 2 * N_GROUPS),
        compiler_params=_cparams(2),
        name="in_proj_att",
    )(x2d, g, w_att, *cast_ws)


def _in_proj_rest_kernel(x_ref, g_ref, w_ref, cos_ref, sin_ref, z_ref, c0, c1, c2,
                         rq_ref, rk_ref, rv_ref, gsw_ref, ga_ref, gr_ref, att_s_ref):
    h = _norm_rows(x_ref, g_ref)
    rows_left = list(range(z_ref.shape[0]))

    def side_row():
        if rows_left:
            r = rows_left.pop(0)
            att_s_ref[r] = _sample_attend_row(z_ref, (c0, c1, c2), r)

    def tile(j):
        lo = (j - N_ATT_TILES) * TN
        return _dot(h, w_ref[:, lo:lo + TN])

    def ret_qk(col, dst, scale):
        acc = tile(col)
        cos, sin = cos_ref[...], sin_ref[...]
        for hh in range(RET_HEADS):
            r = _rope(_head(acc, hh), cos, sin)
            dst[hh] = r if scale is None else r * scale

    def ret_wide(col, dst, e, act=None):
        acc = tile(col + e)
        for s in range(2):
            part = _head(acc, s, RET_DV)
            dst[2 * e + s] = (part if act is None else act(part)).astype(dst.dtype)

    def gate(col, dst, e):
        dst[:, e * TN:(e + 1) * TN] = tile(col + e)

    ret_qk(COL_RQ, rq_ref, None)
    side_row()
    ret_qk(COL_RK, rk_ref, RET_DK ** -0.5)
    side_row()
    for e in range(2):
        ret_wide(COL_RV, rv_ref, e)
        ret_wide(COL_GSW, gsw_ref, e, _silu)
        side_row()
        gate(COL_GA, ga_ref, e)
        gate(COL_GR, gr_ref, e)
    while rows_left:
        side_row()


def _in_proj_rest(x2d, g, w_rest, cos_t, sin_t, z, caches, layer, batch, seq, tm):
    assert seq % tm == 0
    tps = seq // tm
    steps = batch * tps
    db = z.shape[0]
    assert db % steps == 0
    side_rows = db // steps
    views, cache_specs = [], []
    for (win, dil), c in zip(DIL_GROUPS, caches):
        assert c.shape[1] == db and c.shape[2] == win and win // dil == BAND
        views.append(c.reshape(c.shape[0] * db, BAND, dil, 2 * HEADS, HEAD_DIM))
        cache_specs.append(pl.BlockSpec((side_rows, BAND, None, 2 * HEADS, HEAD_DIM),
                                        lambda b, t: (layer * steps + b * tps + t, 0, 0, 0, 0)))
    hm = lambda width, dt: (jax.ShapeDtypeStruct((batch, RET_HEADS, seq, width), dt),
                            pl.BlockSpec((None, RET_HEADS, tm, width), lambda b, t: (b, 0, t, 0)))
    nat = (jax.ShapeDtypeStruct((batch * seq, D_MODEL), F32),
           pl.BlockSpec((tm, D_MODEL), lambda b, t: (b * tps + t, 0)))
    outs = (hm(RET_DK, F32), hm(RET_DK, F32), hm(RET_DV, BF16), hm(RET_DV, F32), nat, nat)
    side_outs = ((jax.ShapeDtypeStruct((db, HEADS, HEAD_DIM), F32),
                  pl.BlockSpec((side_rows, HEADS, HEAD_DIM), lambda b, t: (b * tps + t, 0, 0))),)
    return pl.pallas_call(
        _in_proj_rest_kernel,
        grid=(batch, tps),
        in_specs=[
            pl.BlockSpec((tm, D_MODEL), lambda b, t: (b * tps + t, 0)),
            _resident((1, D_MODEL), lambda b, t: (0, 0)),
            _resident(w_rest.shape, lambda b, t: (0, 0)),
            pl.BlockSpec((tm, RET_DK), lambda b, t: (t, 0)),
            pl.BlockSpec((tm, RET_DK), lambda b, t: (t, 0)),
            pl.BlockSpec((side_rows, 1, IN_COLS), lambda b, t: (b * tps + t, 0, 0)),
        ] + cache_specs,
        out_specs=[spec for _, spec in outs + side_outs],
        out_shape=[shp for shp, _ in outs + side_outs],
        compiler_params=_cparams(2),
        name="in_proj_rest",
    )(x2d, g, w_rest, cos_t, sin_t, z.reshape(db, 1, IN_COLS), *views)


def _in_proj_sample_kernel(x_ref, g_ref, w_ref, z_ref, w_att_ref, w_rest_ref, h_ref):
    j = pl.program_id(0)

    @pl.when(j == 0)
    def _():
        h_ref[...] = _norm_rows(x_ref, g_ref)

    w_bf = w_ref[...].astype(BF16)

    @pl.when(j < N_ATT_TILES)
    def _():
        w_att_ref[...] = w_bf

    @pl.when(j >= N_ATT_TILES)
    def _():
        w_rest_ref[...] = w_bf

    z_ref[...] = _dot(h_ref[...], w_bf)


def _in_proj_sample(x2d, g, w):
    m = x2d.shape[0]
    col_tile = lambda rows, f: pl.BlockSpec((rows, TN), lambda j: (0, f(j)))
    n_rest = N_COL_TILES - N_ATT_TILES
    return pl.pallas_call(
        _in_proj_sample_kernel,
        grid=(N_COL_TILES,),
        in_specs=[
            _resident((m, D_MODEL), lambda j: (0, 0)),
            _resident((1, D_MODEL), lambda j: (0, 0)),
            col_tile(D_MODEL, lambda j: j),
        ],
        out_specs=[col_tile(m, lambda j: j),
                   col_tile(D_MODEL, lambda j: jnp.minimum(j, N_ATT_TILES - 1)),
                   col_tile(D_MODEL, lambda j: jnp.maximum(j - N_ATT_TILES, 0))],
        out_shape=[jax.ShapeDtypeStruct((m, IN_COLS), F32),
                   jax.ShapeDtypeStruct((D_MODEL, N_ATT_TILES * TN), BF16),
                   jax.ShapeDtypeStruct((D_MODEL, n_rest * TN), BF16)],
        scratch_shapes=[pltpu.VMEM((m, D_MODEL), BF16)],
        compiler_params=_cparams(1),
        name="in_proj_sample",
    )(x2d, g, w)


def _softmax_block(s, v):
    m = jnp.max(s, axis=1, keepdims=True)
    p = jnp.exp(s - m).astype(BF16)
    uv = _dot(p, jnp.concatenate([v, jnp.ones_like(v)], axis=1))
    return uv[:, :HEAD_DIM], jnp.broadcast_to(m, (s.shape[0], HEAD_DIM)), uv[:, HEAD_DIM:]


ATT_BLOCKS_PER_STEP = 16
ATT_ORDER = MAX_ROW_STRIDE


def _attn_prompt_kernel(q0, q1, q2, k0, k1, k2, v0, v1, v2, o_ref,
                        u0, u1, u2, m0, m1, m2, l0, l1, l2, *, seq):
    qs, ks, vs = (q0, q1, q2), (k0, k1, k2), (v0, v1, v2)
    us, ms, ls = (u0, u1, u2), (m0, m1, m2), (l0, l1, l2)
    order = ATT_ORDER
    n_out = seq // order
    row1 = lax.broadcasted_iota(jnp.int32, (BAND, BAND), 0)
    col1 = lax.broadcasted_iota(jnp.int32, (BAND, BAND), 1)
    causal = col1 <= row1
    row2 = lax.broadcasted_iota(jnp.int32, (BAND, 2 * BAND), 0)
    col2 = lax.broadcasted_iota(jnp.int32, (BAND, 2 * BAND), 1)
    band = (col2 >= row2) & (col2 <= row2 + BAND)

    aligned = lambda r: r if isinstance(r, int) else pl.multiple_of(r, BAND)

    for g, (_, dil) in enumerate(DIL_GROUPS):
        n = seq // dil
        nb = n // BAND

        def run_blocks(blocks, g=g, n=n, dil=dil):
            scores, values = [], []
            for c, blk, has_prev in blocks:
                own = pl.ds(aligned(blk * BAND), BAND)
                keys = pl.ds(aligned((blk - 1) * BAND), 2 * BAND) if has_prev else own
                mask = band if has_prev else causal
                scores.append(jnp.where(mask, _dot_nt(qs[g][c, own, :], ks[g][c, keys, :]), NEG_INF))
                values.append(vs[g][c, keys, :])
            results = [_softmax_block(s, v) for s, v in zip(scores, values)]
            for (c, blk, _), (u, m, l) in zip(blocks, results):
                if dil <= order:
                    rs = pl.ds(aligned(c * n + blk * BAND), BAND)
                else:
                    step = dil // order
                    rs = pl.ds((c % order) * n_out + c // order + blk * (BAND * step), BAND, stride=step)
                us[g][rs, :] = u
                ms[g][rs, :] = m
                ls[g][rs, :] = l

        per = ATT_BLOCKS_PER_STEP
        assert (dil * nb) % per == 0 and (per % nb == 0 or nb % per == 0)
        if nb >= per:
            def body(i, carry, run_blocks=run_blocks, steps=nb // per):
                c, i0 = i // steps, (i % steps) * per
                run_blocks([(c, i0 + e, True) for e in range(per)])
                return carry
            for c in range(dil):
                run_blocks([(c, e, e > 0) for e in range(per)])
                lax.fori_loop(c * (nb // per) + 1, (c + 1) * (nb // per), body, 0)
        else:
            def body(i, carry, run_blocks=run_blocks, nb=nb, cps=per // nb):
                run_blocks([(i * cps + e // nb, e % nb, e % nb > 0) for e in range(per)])
                return carry
            lax.fori_loop(0, dil * nb // per, body, 0)

    chunks = n_out // BAND

    def combine_one(j):
        c, t = j // chunks, j % chunks
        rows = []
        for _, dil in DIL_GROUPS:
            r = min(dil, order)
            step = order // r
            start = (c % r) * (seq // r) + c // r + t * (BAND * step)
            rows.append(pl.ds(aligned(start), BAND) if step == 1 else pl.ds(start, BAND, stride=step))
        m_g = [ms[g][rows[g], :] for g in range(N_GROUPS)]
        m = jnp.maximum(jnp.maximum(m_g[0], m_g[1]), m_g[2])
        num = jnp.zeros((BAND, HEAD_DIM), F32)
        den = jnp.zeros((BAND, HEAD_DIM), F32)
        for g in range(N_GROUPS):
            w = jnp.exp(m_g[g] - m)
            num = num + w * us[g][rows[g], :]
            den = den + w * ls[g][rows[g], :]
        return num / den

    per = 4
    assert (order * chunks) % per == 0

    def combine(i, carry):
        outs = [combine_one(i * per + e) for e in range(per)]
        for e, o in enumerate(outs):
            o_ref[pl.ds(aligned((i * per + e) * BAND), BAND), :] = o
        return carry

    lax.fori_loop(0, order * chunks // per, combine, 0)


def _attn_prompt(qkv, batch, seq):
    assert seq % (BAND * DIL_GROUPS[-1][1]) == 0
    assert all(ATT_ORDER % d == 0 or d % ATT_ORDER == 0 for _, d in DIL_GROUPS)
    in_specs = [pl.BlockSpec((None, None) + a.shape[2:], lambda b, h: (b, h, 0, 0, 0)) for a in qkv]
    out = pl.pallas_call(
        functools.partial(_attn_prompt_kernel, seq=seq),
        grid=(batch, HEADS),
        in_specs=in_specs,
        out_specs=pl.BlockSpec((None, None, seq, HEAD_DIM), lambda b, h: (b, h, 0, 0)),
        out_shape=jax.ShapeDtypeStruct((batch, HEADS, seq, HEAD_DIM), F32),
        scratch_shapes=[pltpu.VMEM((seq, HEAD_DIM), F32)] * 9,
        compiler_params=_cparams(2),
        name="attend_prompt",
    )(*qkv)
    return out.reshape(batch, HEADS, ATT_ORDER, seq // ATT_ORDER, HEAD_DIM)


def _group_norm_gate(o, gn, gate):
    mu = jnp.mean(o, axis=-1, keepdims=True)
    d = o - mu
    var = jnp.mean(d * d, axis=-1, keepdims=True)
    return gate * (d * lax.rsqrt(var + EPS) * gn)


RET_CHUNKS_PER_STEP = 8


def _ret_prompt_kernel(lg_ref, q_ref, k_ref, v_ref, gate_ref, gn_ref, o_ref, s_out_ref, s_scr):
    c = RET_CHUNK
    rows = q_ref.shape[1]
    t = lax.broadcasted_iota(jnp.int32, (c, 1), 0).astype(F32)
    rel = (lax.broadcasted_iota(jnp.int32, (c, c), 0) - lax.broadcasted_iota(jnp.int32, (c, c), 1)).astype(F32)

    @pl.when(pl.program_id(1) == 0)
    def _():
        s_scr[...] = jnp.zeros_like(s_scr)

    consts = []
    for h in range(RET_HEADS):
        lg = lg_ref[h]
        consts.append((
            jnp.where(rel >= 0, jnp.exp(lg * jnp.maximum(rel, 0.0)), 0.0),
            jnp.exp(lg * (t + 1.0)),
            jnp.exp(lg * (c - 1.0 - t)),
            jnp.exp(lg * jnp.full((1, RET_DV), float(c), F32)),
        ))

    per = RET_CHUNKS_PER_STEP
    assert (rows // c) % per == 0

    def step(i, carry):
        rss = [pl.ds(pl.multiple_of((i * per + e) * c, c), c) for e in range(per)]
        local = {}
        for h in range(RET_HEADS):
            decay, inner, tail, _ = consts[h]
            for e, rs in enumerate(rss):
                q, k, vb = q_ref[h, rs, :], k_ref[h, rs, :], v_ref[h, rs, :]
                a = _dot_nt(q.astype(BF16), k.astype(BF16)) * decay
                local[h, e] = (_dot(a.astype(BF16), vb),
                               (q * inner).astype(BF16),
                               _dot((k * tail).T.astype(BF16), vb))
        outs = {}
        for h in range(RET_HEADS):
            s = s_scr[h]
            for e in range(per):
                intra, q_in, kv = local[h, e]
                outs[h, e] = intra + _dot(q_in, s.astype(BF16))
                s = consts[h][3] * s + kv
            s_scr[h] = s
        for h in range(RET_HEADS):
            gn = gn_ref[:, h * RET_DV:(h + 1) * RET_DV]
            for e, rs in enumerate(rss):
                o_ref[h, rs, :] = _group_norm_gate(outs[h, e], gn, gate_ref[h, rs, :]).astype(o_ref.dtype)
        return carry

    lax.fori_loop(0, rows // (c * per), step, 0)

    @pl.when(pl.program_id(1) == pl.num_programs(1) - 1)
    def _():
        s_out_ref[...] = s_scr[...]


def _ret_prompt(log_g, rq, rk, rv, gsw, gn, batch, seq, rows):
    assert seq % rows == 0 and rows % RET_CHUNK == 0
    hm = lambda width: pl.BlockSpec((None, RET_HEADS, rows, width), lambda b, t: (b, 0, t, 0))
    return pl.pallas_call(
        _ret_prompt_kernel,
        grid=(batch, seq // rows),
        in_specs=[
            pl.BlockSpec(memory_space=pltpu.SMEM),
            hm(RET_DK), hm(RET_DK), hm(RET_DV), hm(RET_DV),
            _resident((1, RET_V), lambda b, t: (0, 0)),
        ],
        out_specs=[
            hm(RET_DV),
            pl.BlockSpec((None, RET_HEADS, RET_DK, RET_DV), lambda b, t: (b, 0, 0, 0)),
        ],
        out_shape=[
            jax.ShapeDtypeStruct((batch, RET_HEADS, seq, RET_DV), BF16),
            jax.ShapeDtypeStruct((batch, RET_HEADS, RET_DK, RET_DV), F32),
        ],
        scratch_shapes=[pltpu.VMEM((RET_HEADS, RET_DK, RET_DV), F32)],
        compiler_params=_cparams(2),
        name="retain_prompt",
    )(log_g, rq, rk, rv, gsw, gn)


def _sample_attend_row(z_ref, caches, r):
    half = BAND // 2

    def heads(tile0, g):
        base = (tile0 + g) * TN
        return jnp.concatenate(
            [z_ref[r, :, base + hh * HEAD_DIM:base + (hh + 1) * HEAD_DIM] for hh in range(HEADS)], axis=0)

    def paired(ref, lo):
        return jnp.concatenate([ref[r, 0:half, lo:lo + HEADS, :], ref[r, half:BAND, lo:lo + HEADS, :]], axis=1)

    twice = lambda a: jnp.concatenate([a, a], axis=0)
    fold = lambda a, op: op(a[0:HEADS], a[HEADS:2 * HEADS])

    us, ms, ls = [], [], []
    for g in range(N_GROUPS):
        q = heads(COL_Q, g) * (HEAD_DIM ** -0.5)
        k_new, v_new = heads(COL_K, g), heads(COL_V, g)
        k_buf, v_buf = paired(caches[g], 0), paired(caches[g], HEADS)
        s_buf = jnp.sum(k_buf * twice(q)[None], axis=-1, keepdims=True)
        s_new = jnp.sum(k_new * q, axis=-1, keepdims=True)
        m = jnp.maximum(fold(jnp.max(s_buf, axis=0), jnp.maximum), s_new)
        p_buf = jnp.exp(s_buf - twice(m)[None])
        p_new = jnp.exp(s_new - m)
        ls.append(fold(jnp.sum(p_buf, axis=0), jnp.add) + p_new)
        us.append(fold(jnp.sum(p_buf * v_buf, axis=0), jnp.add) + p_new * v_new)
        ms.append(m)
    m = jnp.maximum(jnp.maximum(ms[0], ms[1]), ms[2])
    num = jnp.zeros((HEADS, HEAD_DIM), F32)
    den = jnp.zeros((HEADS, 1), F32)
    for g in range(N_GROUPS):
        w = jnp.exp(ms[g] - m)
        num = num + w * us[g]
        den = den + w * ls[g]
    return num / den


def _sample_retain_head(lg_ref, z_ref, cos, sin, gn_ref, s_ref, ret_ref, s_out_ref, h):
    rows = z_ref.shape[0]
    eye = (lax.broadcasted_iota(jnp.int32, (RET_DK, RET_DK), 0)
           == lax.broadcasted_iota(jnp.int32, (RET_DK, RET_DK), 1))

    def column(v):
        return jnp.sum(jnp.where(eye, jnp.broadcast_to(v, (RET_DK, RET_DK)), 0.0), axis=1, keepdims=True)

    def zcols(tile0, width):
        return z_ref[:, 0, tile0 * TN + h * width:tile0 * TN + (h + 1) * width]

    gamma = jnp.exp(lg_ref[h] * jnp.ones((1, RET_DV), F32))
    q_h = _rope(zcols(COL_RQ, RET_DK), cos, sin)
    k_h = _rope(zcols(COL_RK, RET_DK), cos, sin) * (RET_DK ** -0.5)
    v_h = zcols(COL_RV, RET_DV)
    outs = []
    for r in range(rows):
        s_new = gamma * s_ref[r, h] + column(k_h[r:r + 1, :]) * v_h[r:r + 1, :]
        s_out_ref[r, h] = s_new
        outs.append(jnp.sum(column(q_h[r:r + 1, :]) * s_new, axis=0, keepdims=True))
    seg = slice(h * RET_DV, (h + 1) * RET_DV)
    ret_ref[:, 0, seg] = _group_norm_gate(jnp.concatenate(outs, axis=0), gn_ref[:, seg], _silu(zcols(COL_GSW, RET_DV)))


FF_CHUNK = 1024


def _tail_kernel(*refs, with_side):
    (att_ref, ret_ref, ga_ref, gr_ref, x_ref, wa_ref, wr_ref, wo_ref, gpost_mix_ref, gpre_mlp_ref,
     wu_ref, wd_ref, gpost_mlp_ref) = refs[:13]
    att_scr = refs[-1]
    side = []
    if with_side:
        lg_ref, z_ref, cos_ref, sin_ref, gn_ref, s_ref, y_ref, ret_s_ref, s_out_ref = refs[13:-1]
        side = [functools.partial(_sample_retain_head, lg_ref, z_ref, cos_ref[...], sin_ref[...], gn_ref, s_ref,
                                  ret_s_ref, s_out_ref, hh) for hh in range(RET_HEADS)]
    else:
        y_ref, = refs[13:-1]
    r = att_ref.shape[1]
    for hh in range(HEADS):
        for c in range(r):
            rs = pl.ds(c, att_ref.shape[2], stride=r) if r > 1 else slice(None)
            att_scr[hh, rs, :] = att_ref[hh, c]
    att = jnp.concatenate([att_scr[hh].astype(BF16) for hh in range(HEADS)], axis=1)
    ret = jnp.concatenate([ret_ref[hh].astype(BF16) for hh in range(RET_HEADS)], axis=1)
    m = _sigmoid(ga_ref[...]) * _dot(att, wa_ref[...]) + _sigmoid(gr_ref[...]) * _dot(ret, wr_ref[...])
    x1 = x_ref[...] + _rms(_dot(m.astype(BF16), wo_ref[...])) * gpost_mix_ref[...]
    h = (_rms(x1) * gpre_mlp_ref[...]).astype(BF16)

    n_chunks = D_FF // FF_CHUNK
    acc = jnp.zeros(x_ref.shape, F32)
    for c in range(n_chunks):
        cs = slice(c * FF_CHUNK, (c + 1) * FF_CHUNK)
        u = jnp.maximum(_dot(h, wu_ref[:, cs]), 0.0)
        acc = acc + _dot((u * u).astype(BF16), wd_ref[cs, :])
        for work in side[c::n_chunks]:
            work()
    y_ref[...] = x1 + _rms(acc) * gpost_mlp_ref[...]


def _tail(att, ret, ga, gr, x2d, lw, tm, side=None):
    batch, _, r, ni, _ = att.shape
    seq = r * ni
    m = batch * seq
    assert seq % tm == 0 and tm % r == 0 and x2d.shape[0] == m
    tps = seq // tm
    steps = batch * tps
    rows = pl.BlockSpec((tm, D_MODEL), lambda i: (i, 0))
    full = lambda a: _resident(a.shape, lambda i: (0,) * a.ndim)
    weights = [lw[k] for k in ("w_att_br", "w_ret_br", "w_out", "g_post_mix", "g_pre_mlp",
                               "w_up", "w_down", "g_post_mlp")]
    args = [att, ret, ga, gr, x2d] + weights
    in_specs = [pl.BlockSpec((None, HEADS, r, tm // r, HEAD_DIM), lambda i: (i // tps, 0, 0, i % tps, 0)),
                pl.BlockSpec((None, RET_HEADS, tm, RET_DV), lambda i: (i // tps, 0, i % tps, 0)),
                rows, rows, rows] + [full(w) for w in weights]
    out_shape = [jax.ShapeDtypeStruct((m, D_MODEL), F32)]
    out_specs = [rows]
    if side is not None:
        log_g, z, cos_s, sin_s, gn, state, layer = side
        db = z.shape[0]
        assert db % steps == 0
        side_rows = db // steps
        st_block = (side_rows, RET_HEADS, RET_DK, RET_DV)
        args += [log_g, z.reshape(db, 1, IN_COLS), cos_s, sin_s, gn,
                 state.reshape((state.shape[0] * db,) + state.shape[2:])]
        in_specs += [pl.BlockSpec(memory_space=pltpu.SMEM),
                     pl.BlockSpec((side_rows, 1, IN_COLS), lambda i: (i, 0, 0)),
                     full(cos_s), full(sin_s), full(gn),
                     pl.BlockSpec(st_block, lambda i: (layer * steps + i, 0, 0, 0))]
        out_shape += [jax.ShapeDtypeStruct((db, 1, RET_V), F32), jax.ShapeDtypeStruct((db,) + st_block[1:], F32)]
        out_specs += [pl.BlockSpec((side_rows, 1, RET_V), lambda i: (i, 0, 0)),
                      pl.BlockSpec(st_block, lambda i: (i, 0, 0, 0))]
    outs = pl.pallas_call(
        functools.partial(_tail_kernel, with_side=side is not None),
        grid=(steps,),
        in_specs=in_specs,
        out_specs=out_specs,
        out_shape=out_shape,
        scratch_shapes=[pltpu.VMEM((HEADS, tm, HEAD_DIM), F32)],
        compiler_params=_cparams(1),
        name="tail",
    )(*args)
    if side is None:
        return outs[0]
    return outs[0], outs[1].reshape(db, RET_V), outs[2]


def _rope_tables(pos):
    half = RET_DK // 2
    inv = ROPE_BASE ** (-jnp.arange(half, dtype=F32) / half)
    ang = pos.astype(F32)[:, None] * inv[None, :]
    cos, sin = jnp.cos(ang), jnp.sin(ang)
    return jnp.concatenate([cos, cos], axis=-1), jnp.concatenate([-sin, sin], axis=-1)


IN_PROJ_TM = 512
TAIL_TM = 512
TAIL_WEIGHTS = ("w_att_br", "w_ret_br", "w_out", "w_up", "w_down")
RET_ROWS = 1024


def _layer(xp, xs, caches, state, layer, lw, log_g):
    batch, seq, _ = xp.shape
    db, t, _ = xs.shape
    assert t == 1
    xp2d, xs2d = xp.reshape(batch * seq, D_MODEL), xs.reshape(db, D_MODEL)

    z, w_att, w_rest = _in_proj_sample(xs2d, lw["g_pre_mix"], lw["w_in"])

    cos_p, sin_p = _rope_tables(jnp.arange(seq, dtype=jnp.int32))
    cos_s, sin_s = _rope_tables(PAST_LEN + jnp.arange(1, dtype=jnp.int32))
    outs = _in_proj_att(xp2d, lw["g_pre_mix"], w_att, [lw[k] for k in TAIL_WEIGHTS], batch, seq, IN_PROJ_TM)
    qkv = outs[:9]
    rows_p = [kv.reshape(batch, -1, 2, HEADS, HEAD_DIM) for kv in outs[9:12]]
    lw = dict(lw, **dict(zip(TAIL_WEIGHTS, outs[12:])))
    rq, rk, rv, gsw, ga, gr, att_s = _in_proj_rest(
        xp2d, lw["g_pre_mix"], w_rest, cos_p, sin_p, z, caches, layer, batch, seq, IN_PROJ_TM)
    att = _attn_prompt(qkv, batch, seq)
    ret, state_p = _ret_prompt(log_g, rq, rk, rv, gsw, lw["g_ret_norm"], batch, seq, RET_ROWS)
    yp, ret_s, state_s = _tail(att, ret, ga, gr, xp2d, lw, TAIL_TM,
                               side=(log_g, z, cos_s, sin_s, lw["g_ret_norm"], state, layer))

    att_s = jnp.transpose(att_s, (1, 0, 2)).reshape(1, HEADS, 1, db, HEAD_DIM)
    ret_s = jnp.transpose(ret_s.reshape(db, RET_HEADS, RET_DV), (1, 0, 2))[None]
    ga_s = z[:, COL_GA * TN:COL_GA * TN + D_MODEL]
    gr_s = z[:, COL_GR * TN:COL_GR * TN + D_MODEL]
    ys = _tail(att_s, ret_s, ga_s, gr_s, xs2d, lw, db)
    rows_s = []
    for g in range(N_GROUPS):
        k_new = z[:, (COL_K + g) * TN:(COL_K + g + 1) * TN]
        v_new = z[:, (COL_V + g) * TN:(COL_V + g + 1) * TN]
        rows_s.append(jnp.stack([k_new, v_new], axis=1).reshape(db, 1, 2, HEADS, HEAD_DIM))
    return (yp.reshape(batch, seq, D_MODEL), rows_p, state_p), (ys.reshape(db, 1, D_MODEL), rows_s, state_s)


def _stack(xs):
    return xs[0][None] if len(xs) == 1 else jnp.stack(xs, axis=0)


def kernel(x_prompt, x_sample, cache_kv_d1, cache_kv_d4, cache_kv_d16, state_ret, w_in, w_att_br, w_ret_br, w_out, w_up, w_down, g_ret_norm, g_pre_mix, g_post_mix, g_pre_mlp, g_post_mlp):
    depth = w_in.shape[0]
    log_g = jnp.log1p(-jnp.power(2.0, -5.0 - jnp.arange(RET_HEADS, dtype=F32)))
    caches = (cache_kv_d1, cache_kv_d4, cache_kv_d16)
    xp, xs = x_prompt, x_sample
    p_rows, s_rows = [[], [], []], [[], [], []]
    p_states, s_states = [], []
    for l in range(depth):
        lw = {
            "w_in": w_in[l], "w_up": w_up[l], "w_down": w_down[l],
            "w_att_br": w_att_br[l], "w_ret_br": w_ret_br[l], "w_out": w_out[l],
            "g_ret_norm": g_ret_norm[l].reshape(1, RET_V), "g_pre_mix": g_pre_mix[l].reshape(1, D_MODEL),
            "g_post_mix": g_post_mix[l].reshape(1, D_MODEL), "g_pre_mlp": g_pre_mlp[l].reshape(1, D_MODEL),
            "g_post_mlp": g_post_mlp[l].reshape(1, D_MODEL),
        }
        (xp, rows_p, sp), (xs, rows_s, ss) = _layer(xp, xs, caches, state_ret, l, lw, log_g)
        for g in range(N_GROUPS):
            p_rows[g].append(rows_p[g])
            s_rows[g].append(rows_s[g])
        p_states.append(sp)
        s_states.append(ss)
    return (xp, xs, _stack(p_rows[0]), _stack(p_rows[1]), _stack(p_rows[2]), _stack(p_states),
            _stack(s_rows[0]), _stack(s_rows[1]), _stack(s_rows[2]), _stack(s_states))
```

```python
import functools

import jax
import jax.numpy as jnp
from jax import lax
from jax.experimental import pallas as pl
from jax.experimental.pallas import tpu as pltpu

F32 = jnp.float32
BF16 = jnp.bfloat16

D_MODEL = 1024
PAST_LEN = 8192

DIL_GROUPS = ((128, 1), (512, 4), (2048, 16))
N_GROUPS = 3
HEADS = 4
HEAD_DIM = 128
ATT_OUT = HEADS * HEAD_DIM
ATT_COLS = N_GROUPS * ATT_OUT
BAND = 128

RET_HEADS = 4
RET_DK = 128
RET_DV = 256
RET_QK = RET_HEADS * RET_DK
RET_V = RET_HEADS * RET_DV
RET_CHUNK = 128
ROPE_BASE = 10000.0

D_FF = 4 * D_MODEL
IN_COLS = 3 * ATT_COLS + 2 * RET_QK + 2 * RET_V + 2 * D_MODEL
EPS = 1e-6
NEG_INF = -1e30
LOG2_E = 1.4426950408889634

TN = 512
N_COL_TILES = IN_COLS // TN
COL_Q, COL_K, COL_V, COL_RQ, COL_RK, COL_RV, COL_GSW, COL_GA, COL_GR = 0, 3, 6, 9, 10, 11, 13, 15, 17

V7X_VMEM_BYTES = 64 * 1024 * 1024
MIB = 1024 * 1024
CALL_VMEM_BYTES = 56 * MIB
assert CALL_VMEM_BYTES < V7X_VMEM_BYTES


def _cparams(n_axes):
    return pltpu.CompilerParams(
        dimension_semantics=("arbitrary",) * n_axes,
        vmem_limit_bytes=CALL_VMEM_BYTES,
    )


def _resident(shape, index_map):
    return pl.BlockSpec(shape, index_map, pipeline_mode=pl.Buffered(1))


def _rms(x):
    return x * lax.rsqrt(jnp.mean(x * x, axis=-1, keepdims=True) + EPS)


def _sigmoid(x):
    return 1.0 / (1.0 + jnp.exp(-x))


def _silu(x):
    return x * _sigmoid(x)


def _dot(a, b):
    return jnp.dot(a, b, preferred_element_type=F32)


def _dot_nt(a, b):
    return lax.dot_general(a, b, (((1,), (1,)), ((), ())), preferred_element_type=F32)


def _rope(x, cos, sin):
    return x * cos + pltpu.roll(x, RET_DK // 2, 1) * sin


N_ATT_TILES = 3 * N_GROUPS
MAX_ROW_STRIDE = 4


def _norm_rows(x_ref, g_ref):
    return (_rms(x_ref[...]) * g_ref[...]).astype(BF16)


def _head(acc, hh, width=HEAD_DIM):
    return acc[:, hh * width:(hh + 1) * width]


def _in_proj_att_kernel(*refs, first_kept, n_cast):
    x_ref, g_ref, w_ref = refs[:3]
    cast_in, outs = refs[3:3 + n_cast], refs[3 + n_cast:]
    q0, q1, q2, k0, k1, k2, v0, v1, v2, kvo0, kvo1, kvo2 = outs[:12]
    cast_out = outs[12:12 + n_cast]
    scr_q, sk0, sk1, sk2, sv0, sv1, sv2, scr_mid = outs[12 + n_cast:]
    tm = x_ref.shape[0]
    h = _norm_rows(x_ref, g_ref)
    for src, dst in zip(cast_in, cast_out):
        dst[...] = src[...].astype(BF16)

    def park(j, scr, scale=None):
        acc = _dot(h, w_ref[:, j * TN:(j + 1) * TN])
        for hh in range(HEADS):
            scr[hh] = _head(acc, hh) if scale is None else _head(acc, hh) * scale

    def residues(scr, dst, dil):
        s = MAX_ROW_STRIDE
        if dil <= s:
            for hh in range(HEADS):
                for c in range(dil):
                    rows = pl.ds(c, tm // dil, stride=dil) if dil > 1 else slice(None)
                    dst[hh, c] = scr[hh, rows, :].astype(BF16)
            return
        assert dil % s == 0 and dil // s <= s
        for hh in range(HEADS):
            for c in range(s):
                scr_mid[hh, c * (tm // s):(c + 1) * (tm // s), :] = scr[hh, pl.ds(c, tm // s, stride=s), :]
        for hh in range(HEADS):
            for c in range(dil):
                start = (c % s) * (tm // s) + c // s
                dst[hh, c] = scr_mid[hh, pl.ds(start, tm // dil, stride=dil // s), :].astype(BF16)

    def kv_rows(g):
        kvo = (kvo0, kvo1, kvo2)[g]
        keep = kvo.shape[0] // (2 * HEADS)
        for which, scr in enumerate(((sk0, sk1, sk2)[g], (sv0, sv1, sv2)[g])):
            for hh in range(HEADS):
                kvo[pl.ds(which * HEADS + hh, keep, stride=2 * HEADS), :] = scr[hh, tm - keep:, :]

    for g in sorted(range(N_GROUPS), key=lambda g_: first_kept[g_]):
        dil = DIL_GROUPS[g][1]
        park(COL_K + g, (sk0, sk1, sk2)[g])
        residues((sk0, sk1, sk2)[g], (k0, k1, k2)[g], dil)
        park(COL_V + g, (sv0, sv1, sv2)[g])
        residues((sv0, sv1, sv2)[g], (v0, v1, v2)[g], dil)
        if first_kept[g] == 0:
            kv_rows(g)
        park(COL_Q + g, scr_q, HEAD_DIM ** -0.5 * LOG2_E)
        residues(scr_q, (q0, q1, q2)[g], dil)

    for g in range(N_GROUPS):
        if first_kept[g] > 0:
            pl.when(pl.program_id(1) >= first_kept[g])(functools.partial(kv_rows, g))


def _in_proj_att(x2d, g, w_att, cast_ws, batch, seq, tm):
    assert seq % tm == 0 and tm % (16 * DIL_GROUPS[-1][1]) == 0
    tps = seq // tm
    out_shape, out_specs, first_kept = [], [], []
    for _ in range(3):
        for _, dil in DIL_GROUPS:
            out_shape.append(jax.ShapeDtypeStruct((batch, HEADS, dil, seq // dil, HEAD_DIM), BF16))
            out_specs.append(pl.BlockSpec((None, HEADS, dil, tm // dil, HEAD_DIM), lambda b, t: (b, 0, 0, t, 0)))
    for win, _ in DIL_GROUPS:
        keep = min(win, seq)
        rows = min(keep, tm)
        assert keep % rows == 0
        first = (seq - keep) // rows
        first_kept.append((seq - keep) // tm)
        out_shape.append(jax.ShapeDtypeStruct((batch, keep * 2 * HEADS, HEAD_DIM), F32))
        out_specs.append(pl.BlockSpec(
            (None, rows * 2 * HEADS, HEAD_DIM),
            functools.partial(lambda b, t, first, per: (b, jnp.maximum((t + 1) * per - 1 - first, 0), 0),
                              first=first, per=tm // rows)))
    steps = batch * tps
    cast_rows = [w.shape[0] // steps for w in cast_ws]
    assert all(w.shape[0] % steps == 0 and r % 16 == 0 for w, r in zip(cast_ws, cast_rows))
    cast_specs = [pl.BlockSpec((r, w.shape[1]), lambda b, t: (b * tps + t, 0)) for w, r in zip(cast_ws, cast_rows)]
    return pl.pallas_call(
        functools.partial(_in_proj_att_kernel, first_kept=tuple(first_kept), n_cast=len(cast_ws)),
        grid=(batch, tps),
        in_specs=[
            pl.BlockSpec((tm, D_MODEL), lambda b, t: (b * tps + t, 0)),
            _resident((1, D_MODEL), lambda b, t: (0, 0)),
            _resident(w_att.shape, lambda b, t: (0, 0)),
        ] + cast_specs,
        out_specs=out_specs + cast_specs,
        out_shape=out_shape + [jax.ShapeDtypeStruct(w.shape, BF16) for w in cast_ws],
        scratch_shapes=[pltpu.VMEM((HEADS, tm, HEAD_DIM), F32)] * (2 + 2 * N_GROUPS),
        compiler_params=_cparams(2),
        name="in_proj_att",
    )(x2d, g, w_att, *cast_ws)


def _in_proj_rest_kernel(x_ref, g_ref, w_ref, cos_ref, sin_ref, z_ref, c0, c1, c2,
                         rq_ref, rk_ref, rv_ref, gsw_ref, ga_ref, gr_ref, att_s_ref):
    h = _norm_rows(x_ref, g_ref)
    rows_left = list(range(z_ref.shape[0]))

    def side_row():
        if rows_left:
            r = rows_left.pop(0)
            att_s_ref[r] = _sample_attend_row(z_ref, (c0, c1, c2), r)

    def tile(j):
        lo = (j - N_ATT_TILES) * TN
        return _dot(h, w_ref[:, lo:lo + TN])

    def ret_qk(col, dst, scale):
        acc = tile(col)
        cos, sin = cos_ref[...], sin_ref[...]
        for hh in range(RET_HEADS):
            r = _rope(_head(acc, hh), cos, sin)
            dst[hh] = r if scale is None else r * scale

    def ret_wide(col, dst, e, act=None):
        acc = tile(col + e)
        for s in range(2):
            part = _head(acc, s, RET_DV)
            dst[2 * e + s] = (part if act is None else act(part)).astype(dst.dtype)

    def gate(col, dst, e):
        dst[:, e * TN:(e + 1) * TN] = tile(col + e)

    ret_qk(COL_RQ, rq_ref, None)
    side_row()
    ret_qk(COL_RK, rk_ref, RET_DK ** -0.5)
    side_row()
    for e in range(2):
        ret_wide(COL_RV, rv_ref, e)
        ret_wide(COL_GSW, gsw_ref, e, _silu)
        side_row()
        gate(COL_GA, ga_ref, e)
        gate(COL_GR, gr_ref, e)
    while rows_left:
        side_row()


def _in_proj_rest(x2d, g, w_rest, cos_t, sin_t, z, caches, layer, batch, seq, tm):
    assert seq % tm == 0
    tps = seq // tm
    steps = batch * tps
    db = z.shape[0]
    assert db % steps == 0
    side_rows = db // steps
    views, cache_specs = [], []
    for (win, dil), c in zip(DIL_GROUPS, caches):
        assert c.shape[1] == db and c.shape[2] == win and win // dil == BAND
        views.append(c.reshape(c.shape[0] * db, BAND, dil, 2 * HEADS, HEAD_DIM))
        cache_specs.append(pl.BlockSpec((side_rows, BAND, None, 2 * HEADS, HEAD_DIM),
                                        lambda b, t: (layer * steps + b * tps + t, 0, 0, 0, 0)))
    hm = lambda width, dt: (jax.ShapeDtypeStruct((batch, RET_HEADS, seq, width), dt),
                            pl.BlockSpec((None, RET_HEADS, tm, width), lambda b, t: (b, 0, t, 0)))
    nat = (jax.ShapeDtypeStruct((batch * seq, D_MODEL), F32),
           pl.BlockSpec((tm, D_MODEL), lambda b, t: (b * tps + t, 0)))
    outs = (hm(RET_DK, F32), hm(RET_DK, F32), hm(RET_DV, BF16), hm(RET_DV, F32), nat, nat)
    side_outs = ((jax.ShapeDtypeStruct((db, HEADS, HEAD_DIM), F32),
                  pl.BlockSpec((side_rows, HEADS, HEAD_DIM), lambda b, t: (b * tps + t, 0, 0))),)
    return pl.pallas_call(
        _in_proj_rest_kernel,
        grid=(batch, tps),
        in_specs=[
            pl.BlockSpec((tm, D_MODEL), lambda b, t: (b * tps + t, 0)),
            _resident((1, D_MODEL), lambda b, t: (0, 0)),
            _resident(w_rest.shape, lambda b, t: (0, 0)),
            pl.BlockSpec((tm, RET_DK), lambda b, t: (t, 0)),
            pl.BlockSpec((tm, RET_DK), lambda b, t: (t, 0)),
            pl.BlockSpec((side_rows, 1, IN_COLS), lambda b, t: (b * tps + t, 0, 0)),
        ] + cache_specs,
        out_specs=[spec for _, spec in outs + side_outs],
        out_shape=[shp for shp, _ in outs + side_outs],
        compiler_params=_cparams(2),
        name="in_proj_rest",
    )(x2d, g, w_rest, cos_t, sin_t, z.reshape(db, 1, IN_COLS), *views)


def _in_proj_sample_kernel(x_ref, g_ref, w_ref, z_ref, w_att_ref, w_rest_ref, h_ref):
    j = pl.program_id(0)

    @pl.when(j == 0)
    def _():
        h_ref[...] = _norm_rows(x_ref, g_ref)

    w_bf = w_ref[...].astype(BF16)

    @pl.when(j < N_ATT_TILES)
    def _():
        w_att_ref[...] = w_bf

    @pl.when(j >= N_ATT_TILES)
    def _():
        w_rest_ref[...] = w_bf

    z_ref[...] = _dot(h_ref[...], w_bf)


def _in_proj_sample(x2d, g, w):
    m = x2d.shape[0]
    col_tile = lambda rows, f: pl.BlockSpec((rows, TN), lambda j: (0, f(j)))
    n_rest = N_COL_TILES - N_ATT_TILES
    return pl.pallas_call(
        _in_proj_sample_kernel,
        grid=(N_COL_TILES,),
        in_specs=[
            _resident((m, D_MODEL), lambda j: (0, 0)),
            _resident((1, D_MODEL), lambda j: (0, 0)),
            col_tile(D_MODEL, lambda j: j),
        ],
        out_specs=[col_tile(m, lambda j: j),
                   col_tile(D_MODEL, lambda j: jnp.minimum(j, N_ATT_TILES - 1)),
                   col_tile(D_MODEL, lambda j: jnp.maximum(j - N_ATT_TILES, 0))],
        out_shape=[jax.ShapeDtypeStruct((m, IN_COLS), F32),
                   jax.ShapeDtypeStruct((D_MODEL, N_ATT_TILES * TN), BF16),
                   jax.ShapeDtypeStruct((D_MODEL, n_rest * TN), BF16)],
        scratch_shapes=[pltpu.VMEM((m, D_MODEL), BF16)],
        compiler_params=_cparams(1),
        name="in_proj_sample",
    )(x2d, g, w)


def _softmax_block(s, v):
    m = jnp.max(s, axis=1, keepdims=True)
    p = jnp.exp2(s - m).astype(BF16)
    uv = _dot(p, jnp.concatenate([v, jnp.ones_like(v)], axis=1))
    return uv[:, :HEAD_DIM], jnp.broadcast_to(m, (s.shape[0], HEAD_DIM)), uv[:, HEAD_DIM:]


ATT_BLOCKS_PER_STEP = 16
ATT_ORDER = MAX_ROW_STRIDE


def _attn_prompt_kernel(q0, q1, q2, k0, k1, k2, v0, v1, v2, o_ref,
                        u0, u1, u2, m0, m1, m2, l0, l1, l2, *, seq):
    qs, ks, vs = (q0, q1, q2), (k0, k1, k2), (v0, v1, v2)
    us, ms, ls = (u0, u1, u2), (m0, m1, m2), (l0, l1, l2)
    order = ATT_ORDER
    n_out = seq // order
    row1 = lax.broadcasted_iota(jnp.int32, (BAND, BAND), 0)
    col1 = lax.broadcasted_iota(jnp.int32, (BAND, BAND), 1)
    causal = col1 <= row1
    row2 = lax.broadcasted_iota(jnp.int32, (BAND, 2 * BAND), 0)
    col2 = lax.broadcasted_iota(jnp.int32, (BAND, 2 * BAND), 1)
    band = (col2 >= row2) & (col2 <= row2 + BAND)

    aligned = lambda r: r if isinstance(r, int) else pl.multiple_of(r, BAND)

    for g, (_, dil) in enumerate(DIL_GROUPS):
        n = seq // dil
        nb = n // BAND

        def run_blocks(blocks, g=g, n=n, dil=dil):
            scores, values = [], []
            for c, blk, has_prev in blocks:
                own = pl.ds(aligned(blk * BAND), BAND)
                keys = pl.ds(aligned((blk - 1) * BAND), 2 * BAND) if has_prev else own
                mask = band if has_prev else causal
                scores.append(jnp.where(mask, _dot_nt(qs[g][c, own, :], ks[g][c, keys, :]), NEG_INF))
                values.append(vs[g][c, keys, :])
            results = [_softmax_block(s, v) for s, v in zip(scores, values)]
            for (c, blk, _), (u, m, l) in zip(blocks, results):
                if dil <= order:
                    rs = pl.ds(aligned(c * n + blk * BAND), BAND)
                else:
                    step = dil // order
                    rs = pl.ds((c % order) * n_out + c // order + blk * (BAND * step), BAND, stride=step)
                us[g][rs, :] = u
                ms[g][rs, :] = m
                ls[g][rs, :] = l

        per = ATT_BLOCKS_PER_STEP
        assert (dil * nb) % per == 0 and (per % nb == 0 or nb % per == 0)
        if nb >= per:
            def body(i, carry, run_blocks=run_blocks, steps=nb // per):
                c, i0 = i // steps, (i % steps) * per
                run_blocks([(c, i0 + e, True) for e in range(per)])
                return carry
            for c in range(dil):
                run_blocks([(c, e, e > 0) for e in range(per)])
                lax.fori_loop(c * (nb // per) + 1, (c + 1) * (nb // per), body, 0)
        else:
            def body(i, carry, run_blocks=run_blocks, nb=nb, cps=per // nb):
                run_blocks([(i * cps + e // nb, e % nb, e % nb > 0) for e in range(per)])
                return carry
            lax.fori_loop(0, dil * nb // per, body, 0)

    chunks = n_out // BAND

    def combine_one(j):
        c, t = j // chunks, j % chunks
        rows = []
        for _, dil in DIL_GROUPS:
            r = min(dil, order)
            step = order // r
            start = (c % r) * (seq // r) + c // r + t * (BAND * step)
            rows.append(pl.ds(aligned(start), BAND) if step == 1 else pl.ds(start, BAND, stride=step))
        m_g = [ms[g][rows[g], :] for g in range(N_GROUPS)]
        m = jnp.maximum(jnp.maximum(m_g[0], m_g[1]), m_g[2])
        num = jnp.zeros((BAND, HEAD_DIM), F32)
        den = jnp.zeros((BAND, HEAD_DIM), F32)
        for g in range(N_GROUPS):
            w = jnp.exp2(m_g[g] - m)
            num = num + w * us[g][rows[g], :]
            den = den + w * ls[g][rows[g], :]
        return num / den

    per = 4
    assert (order * chunks) % per == 0

    def combine(i, carry):
        outs = [combine_one(i * per + e) for e in range(per)]
        for e, o in enumerate(outs):
            o_ref[pl.ds(aligned((i * per + e) * BAND), BAND), :] = o
        return carry

    lax.fori_loop(0, order * chunks // per, combine, 0)


def _attn_prompt(qkv, batch, seq):
    assert seq % (BAND * DIL_GROUPS[-1][1]) == 0
    assert all(ATT_ORDER % d == 0 or d % ATT_ORDER == 0 for _, d in DIL_GROUPS)
    in_specs = [pl.BlockSpec((None, None) + a.shape[2:], lambda b, h: (b, h, 0, 0, 0)) for a in qkv]
    out = pl.pallas_call(
        functools.partial(_attn_prompt_kernel, seq=seq),
        grid=(batch, HEADS),
        in_specs=in_specs,
        out_specs=pl.BlockSpec((None, None, seq, HEAD_DIM), lambda b, h: (b, h, 0, 0)),
        out_shape=jax.ShapeDtypeStruct((batch, HEADS, seq, HEAD_DIM), F32),
        scratch_shapes=[pltpu.VMEM((seq, HEAD_DIM), F32)] * 9,
        compiler_params=_cparams(2),
        name="attend_prompt",
    )(*qkv)
    return out.reshape(batch, HEADS, ATT_ORDER, seq // ATT_ORDER, HEAD_DIM)


def _group_norm_gate(o, gn, gate):
    mu = jnp.mean(o, axis=-1, keepdims=True)
    d = o - mu
    var = jnp.mean(d * d, axis=-1, keepdims=True)
    return gate * (d * lax.rsqrt(var + EPS) * gn)


RET_CHUNKS_PER_STEP = 8


def _ret_prompt_kernel(lg_ref, q_ref, k_ref, v_ref, gate_ref, gn_ref, o_ref, s_out_ref, s_scr):
    c = RET_CHUNK
    rows = q_ref.shape[1]
    t = lax.broadcasted_iota(jnp.int32, (c, 1), 0).astype(F32)
    rel = (lax.broadcasted_iota(jnp.int32, (c, c), 0) - lax.broadcasted_iota(jnp.int32, (c, c), 1)).astype(F32)

    @pl.when(pl.program_id(1) == 0)
    def _():
        s_scr[...] = jnp.zeros_like(s_scr)

    consts = []
    for h in range(RET_HEADS):
        lg = lg_ref[h]
        consts.append((
            jnp.where(rel >= 0, jnp.exp(lg * jnp.maximum(rel, 0.0)), 0.0),
            jnp.exp(lg * (t + 1.0)),
            jnp.exp(lg * (c - 1.0 - t)),
            jnp.exp(lg * jnp.full((1, RET_DV), float(c), F32)),
        ))

    per = RET_CHUNKS_PER_STEP
    assert (rows // c) % per == 0

    def step(i, carry):
        rss = [pl.ds(pl.multiple_of((i * per + e) * c, c), c) for e in range(per)]
        local = {}
        for h in range(RET_HEADS):
            decay, inner, tail, _ = consts[h]
            for e, rs in enumerate(rss):
                q, k, vb = q_ref[h, rs, :], k_ref[h, rs, :], v_ref[h, rs, :]
                a = _dot_nt(q.astype(BF16), k.astype(BF16)) * decay
                local[h, e] = (_dot(a.astype(BF16), vb),
                               (q * inner).astype(BF16),
                               _dot((k * tail).T.astype(BF16), vb))
        outs = {}
        for h in range(RET_HEADS):
            s = s_scr[h]
            for e in range(per):
                intra, q_in, kv = local[h, e]
                outs[h, e] = intra + _dot(q_in, s.astype(BF16))
                s = consts[h][3] * s + kv
            s_scr[h] = s
        for h in range(RET_HEADS):
            gn = gn_ref[:, h * RET_DV:(h + 1) * RET_DV]
            for e, rs in enumerate(rss):
                o_ref[h, rs, :] = _group_norm_gate(outs[h, e], gn, gate_ref[h, rs, :]).astype(o_ref.dtype)
        return carry

    lax.fori_loop(0, rows // (c * per), step, 0)

    @pl.when(pl.program_id(1) == pl.num_programs(1) - 1)
    def _():
        s_out_ref[...] = s_scr[...]


def _ret_prompt(log_g, rq, rk, rv, gsw, gn, batch, seq, rows):
    assert seq % rows == 0 and rows % RET_CHUNK == 0
    hm = lambda width: pl.BlockSpec((None, RET_HEADS, rows, width), lambda b, t: (b, 0, t, 0))
    return pl.pallas_call(
        _ret_prompt_kernel,
        grid=(batch, seq // rows),
        in_specs=[
            pl.BlockSpec(memory_space=pltpu.SMEM),
            hm(RET_DK), hm(RET_DK), hm(RET_DV), hm(RET_DV),
            _resident((1, RET_V), lambda b, t: (0, 0)),
        ],
        out_specs=[
            hm(RET_DV),
            pl.BlockSpec((None, RET_HEADS, RET_DK, RET_DV), lambda b, t: (b, 0, 0, 0)),
        ],
        out_shape=[
            jax.ShapeDtypeStruct((batch, RET_HEADS, seq, RET_DV), BF16),
            jax.ShapeDtypeStruct((batch, RET_HEADS, RET_DK, RET_DV), F32),
        ],
        scratch_shapes=[pltpu.VMEM((RET_HEADS, RET_DK, RET_DV), F32)],
        compiler_params=_cparams(2),
        name="retain_prompt",
    )(log_g, rq, rk, rv, gsw, gn)


def _sample_attend_row(z_ref, caches, r):
    half = BAND // 2

    def heads(tile0, g):
        base = (tile0 + g) * TN
        return jnp.concatenate(
            [z_ref[r, :, base + hh * HEAD_DIM:base + (hh + 1) * HEAD_DIM] for hh in range(HEADS)], axis=0)

    def paired(ref, lo):
        return jnp.concatenate([ref[r, 0:half, lo:lo + HEADS, :], ref[r, half:BAND, lo:lo + HEADS, :]], axis=1)

    twice = lambda a: jnp.concatenate([a, a], axis=0)
    fold = lambda a, op: op(a[0:HEADS], a[HEADS:2 * HEADS])

    us, ms, ls = [], [], []
    for g in range(N_GROUPS):
        q = heads(COL_Q, g) * (HEAD_DIM ** -0.5)
        k_new, v_new = heads(COL_K, g), heads(COL_V, g)
        k_buf, v_buf = paired(caches[g], 0), paired(caches[g], HEADS)
        s_buf = jnp.sum(k_buf * twice(q)[None], axis=-1, keepdims=True)
        s_new = jnp.sum(k_new * q, axis=-1, keepdims=True)
        m = jnp.maximum(fold(jnp.max(s_buf, axis=0), jnp.maximum), s_new)
        p_buf = jnp.exp(s_buf - twice(m)[None])
        p_new = jnp.exp(s_new - m)
        ls.append(fold(jnp.sum(p_buf, axis=0), jnp.add) + p_new)
        us.append(fold(jnp.sum(p_buf * v_buf, axis=0), jnp.add) + p_new * v_new)
        ms.append(m)
    m = jnp.maximum(jnp.maximum(ms[0], ms[1]), ms[2])
    num = jnp.zeros((HEADS, HEAD_DIM), F32)
    den = jnp.zeros((HEADS, 1), F32)
    for g in range(N_GROUPS):
        w = jnp.exp(ms[g] - m)
        num = num + w * us[g]
        den = den + w * ls[g]
    return num / den


def _sample_retain_head(lg_ref, z_ref, cos, sin, gn_ref, s_ref, ret_ref, s_out_ref, h):
    rows = z_ref.shape[0]
    eye = (lax.broadcasted_iota(jnp.int32, (RET_DK, RET_DK), 0)
           == lax.broadcasted_iota(jnp.int32, (RET_DK, RET_DK), 1))

    def column(v):
        return jnp.sum(jnp.where(eye, jnp.broadcast_to(v, (RET_DK, RET_DK)), 0.0), axis=1, keepdims=True)

    def zcols(tile0, width):
        return z_ref[:, 0, tile0 * TN + h * width:tile0 * TN + (h + 1) * width]

    gamma = jnp.exp(lg_ref[h] * jnp.ones((1, RET_DV), F32))
    q_h = _rope(zcols(COL_RQ, RET_DK), cos, sin)
    k_h = _rope(zcols(COL_RK, RET_DK), cos, sin) * (RET_DK ** -0.5)
    v_h = zcols(COL_RV, RET_DV)
    outs = []
    for r in range(rows):
        s_new = gamma * s_ref[r, h] + column(k_h[r:r + 1, :]) * v_h[r:r + 1, :]
        s_out_ref[r, h] = s_new
        outs.append(jnp.sum(column(q_h[r:r + 1, :]) * s_new, axis=0, keepdims=True))
    seg = slice(h * RET_DV, (h + 1) * RET_DV)
    ret_ref[:, 0, seg] = _group_norm_gate(jnp.concatenate(outs, axis=0), gn_ref[:, seg], _silu(zcols(COL_GSW, RET_DV)))


FF_CHUNK = 1024


def _tail_kernel(*refs, with_side):
    (att_ref, ret_ref, ga_ref, gr_ref, x_ref, wa_ref, wr_ref, wo_ref, gpost_mix_ref, gpre_mlp_ref,
     wu_ref, wd_ref, gpost_mlp_ref) = refs[:13]
    att_scr = refs[-1]
    side = []
    if with_side:
        lg_ref, z_ref, cos_ref, sin_ref, gn_ref, s_ref, y_ref, ret_s_ref, s_out_ref = refs[13:-1]
        side = [functools.partial(_sample_retain_head, lg_ref, z_ref, cos_ref[...], sin_ref[...], gn_ref, s_ref,
                                  ret_s_ref, s_out_ref, hh) for hh in range(RET_HEADS)]
    else:
        y_ref, = refs[13:-1]
    r = att_ref.shape[1]
    for hh in range(HEADS):
        for c in range(r):
            rs = pl.ds(c, att_ref.shape[2], stride=r) if r > 1 else slice(None)
            att_scr[hh, rs, :] = att_ref[hh, c]
    att = jnp.concatenate([att_scr[hh].astype(BF16) for hh in range(HEADS)], axis=1)
    ret = jnp.concatenate([ret_ref[hh].astype(BF16) for hh in range(RET_HEADS)], axis=1)
    m = _sigmoid(ga_ref[...]) * _dot(att, wa_ref[...]) + _sigmoid(gr_ref[...]) * _dot(ret, wr_ref[...])
    x1 = x_ref[...] + _rms(_dot(m.astype(BF16), wo_ref[...])) * gpost_mix_ref[...]
    h = (_rms(x1) * gpre_mlp_ref[...]).astype(BF16)

    n_chunks = D_FF // FF_CHUNK
    acc = jnp.zeros(x_ref.shape, F32)
    for c in range(n_chunks):
        cs = slice(c * FF_CHUNK, (c + 1) * FF_CHUNK)
        u = jnp.maximum(_dot(h, wu_ref[:, cs]), 0.0)
        acc = acc + _dot((u * u).astype(BF16), wd_ref[cs, :])
        for work in side[c::n_chunks]:
            work()
    y_ref[...] = x1 + _rms(acc) * gpost_mlp_ref[...]


def _tail(att, ret, ga, gr, x2d, lw, tm, side=None):
    batch, _, r, ni, _ = att.shape
    seq = r * ni
    m = batch * seq
    assert seq % tm == 0 and tm % r == 0 and x2d.shape[0] == m
    tps = seq // tm
    steps = batch * tps
    rows = pl.BlockSpec((tm, D_MODEL), lambda i: (i, 0))
    full = lambda a: _resident(a.shape, lambda i: (0,) * a.ndim)
    weights = [lw[k] for k in ("w_att_br", "w_ret_br", "w_out", "g_post_mix", "g_pre_mlp",
                               "w_up", "w_down", "g_post_mlp")]
    args = [att, ret, ga, gr, x2d] + weights
    in_specs = [pl.BlockSpec((None, HEADS, r, tm // r, HEAD_DIM), lambda i: (i // tps, 0, 0, i % tps, 0)),
                pl.BlockSpec((None, RET_HEADS, tm, RET_DV), lambda i: (i // tps, 0, i % tps, 0)),
                rows, rows, rows] + [full(w) for w in weights]
    out_shape = [jax.ShapeDtypeStruct((m, D_MODEL), F32)]
    out_specs = [rows]
    if side is not None:
        log_g, z, cos_s, sin_s, gn, state, layer = side
        db = z.shape[0]
        assert db % steps == 0
        side_rows = db // steps
        st_block = (side_rows, RET_HEADS, RET_DK, RET_DV)
        args += [log_g, z.reshape(db, 1, IN_COLS), cos_s, sin_s, gn,
                 state.reshape((state.shape[0] * db,) + state.shape[2:])]
        in_specs += [pl.BlockSpec(memory_space=pltpu.SMEM),
                     pl.BlockSpec((side_rows, 1, IN_COLS), lambda i: (i, 0, 0)),
                     full(cos_s), full(sin_s), full(gn),
                     pl.BlockSpec(st_block, lambda i: (layer * steps + i, 0, 0, 0))]
        out_shape += [jax.ShapeDtypeStruct((db, 1, RET_V), F32), jax.ShapeDtypeStruct((db,) + st_block[1:], F32)]
        out_specs += [pl.BlockSpec((side_rows, 1, RET_V), lambda i: (i, 0, 0)),
                      pl.BlockSpec(st_block, lambda i: (i, 0, 0, 0))]
    outs = pl.pallas_call(
        functools.partial(_tail_kernel, with_side=side is not None),
        grid=(steps,),
        in_specs=in_specs,
        out_specs=out_specs,
        out_shape=out_shape,
        scratch_shapes=[pltpu.VMEM((HEADS, tm, HEAD_DIM), F32)],
        compiler_params=_cparams(1),
        name="tail",
    )(*args)
    if side is None:
        return outs[0]
    return outs[0], outs[1].reshape(db, RET_V), outs[2]


def _rope_tables(pos):
    half = RET_DK // 2
    inv = ROPE_BASE ** (-jnp.arange(half, dtype=F32) / half)
    ang = pos.astype(F32)[:, None] * inv[None, :]
    cos, sin = jnp.cos(ang), jnp.sin(ang)
    return jnp.concatenate([cos, cos], axis=-1), jnp.concatenate([-sin, sin], axis=-1)


IN_PROJ_TM = 512
TAIL_TM = 512
TAIL_WEIGHTS = ("w_att_br", "w_ret_br", "w_out", "w_up", "w_down")
RET_ROWS = 1024


def _layer(xp, xs, caches, state, layer, lw, log_g):
    batch, seq, _ = xp.shape
    db, t, _ = xs.shape
    assert t == 1
    xp2d, xs2d = xp.reshape(batch * seq, D_MODEL), xs.reshape(db, D_MODEL)

    z, w_att, w_rest = _in_proj_sample(xs2d, lw["g_pre_mix"], lw["w_in"])

    cos_p, sin_p = _rope_tables(jnp.arange(seq, dtype=jnp.int32))
    cos_s, sin_s = _rope_tables(PAST_LEN + jnp.arange(1, dtype=jnp.int32))
    outs = _in_proj_att(xp2d, lw["g_pre_mix"], w_att, [lw[k] for k in TAIL_WEIGHTS], batch, seq, IN_PROJ_TM)
    qkv = outs[:9]
    rows_p = [kv.reshape(batch, -1, 2, HEADS, HEAD_DIM) for kv in outs[9:12]]
    lw = dict(lw, **dict(zip(TAIL_WEIGHTS, outs[12:])))
    rq, rk, rv, gsw, ga, gr, att_s = _in_proj_rest(
        xp2d, lw["g_pre_mix"], w_rest, cos_p, sin_p, z, caches, layer, batch, seq, IN_PROJ_TM)
    att = _attn_prompt(qkv, batch, seq)
    ret, state_p = _ret_prompt(log_g, rq, rk, rv, gsw, lw["g_ret_norm"], batch, seq, RET_ROWS)
    yp, ret_s, state_s = _tail(att, ret, ga, gr, xp2d, lw, TAIL_TM,
                               side=(log_g, z, cos_s, sin_s, lw["g_ret_norm"], state, layer))

    att_s = jnp.transpose(att_s, (1, 0, 2)).reshape(1, HEADS, 1, db, HEAD_DIM)
    ret_s = jnp.transpose(ret_s.reshape(db, RET_HEADS, RET_DV), (1, 0, 2))[None]
    ga_s = z[:, COL_GA * TN:COL_GA * TN + D_MODEL]
    gr_s = z[:, COL_GR * TN:COL_GR * TN + D_MODEL]
    ys = _tail(att_s, ret_s, ga_s, gr_s, xs2d, lw, db)
    rows_s = []
    for g in range(N_GROUPS):
        k_new = z[:, (COL_K + g) * TN:(COL_K + g + 1) * TN]
        v_new = z[:, (COL_V + g) * TN:(COL_V + g + 1) * TN]
        rows_s.append(jnp.stack([k_new, v_new], axis=1).reshape(db, 1, 2, HEADS, HEAD_DIM))
    return (yp.reshape(batch, seq, D_MODEL), rows_p, state_p), (ys.reshape(db, 1, D_MODEL), rows_s, state_s)


def _stack(xs):
    return xs[0][None] if len(xs) == 1 else jnp.stack(xs, axis=0)


def kernel(x_prompt, x_sample, cache_kv_d1, cache_kv_d4, cache_kv_d16, state_ret, w_in, w_att_br, w_ret_br, w_out, w_up, w_down, g_ret_norm, g_pre_mix, g_post_mix, g_pre_mlp, g_post_mlp):
    depth = w_in.shape[0]
    log_g = jnp.log1p(-jnp.power(2.0, -5.0 - jnp.arange(RET_HEADS, dtype=F32)))
    caches = (cache_kv_d1, cache_kv_d4, cache_kv_d16)
    xp, xs = x_prompt, x_sample
    p_rows, s_rows = [[], [], []], [[], [], []]
    p_states, s_states = [], []
    for l in range(depth):
        lw = {
            "w_in": w_in[l], "w_up": w_up[l], "w_down": w_down[l],
            "w_att_br": w_att_br[l], "w_ret_br": w_ret_br[l], "w_out": w_out[l],
            "g_ret_norm": g_ret_norm[l].reshape(1, RET_V), "g_pre_mix": g_pre_mix[l].reshape(1, D_MODEL),
            "g_post_mix": g_post_mix[l].reshape(1, D_MODEL), "g_pre_mlp": g_pre_mlp[l].reshape(1, D_MODEL),
            "g_post_mlp": g_post_mlp[l].reshape(1, D_MODEL),
        }
        (xp, rows_p, sp), (xs, rows_s, ss) = _layer(xp, xs, caches, state_ret, l, lw, log_g)
        for g in range(N_GROUPS):
            p_rows[g].append(rows_p[g])
            s_rows[g].append(rows_s[g])
        p_states.append(sp)
        s_states.append(ss)
    return (xp, xs, _stack(p_rows[0]), _stack(p_rows[1]), _stack(p_rows[2]), _stack(p_states),
            _stack(s_rows[0]), _stack(s_rows[1]), _stack(s_rows[2]), _stack(s_states))
```
